```python
import math
import jax, jax.numpy as jnp
from jax import lax
import numpy as np

D_MODEL = 1024
BATCH = 4
SEQ = 4096
DEPTH = 1
DEC_BATCH = 128
DEC_SEQ = 4
PAST_LEN = 2048
PAGE_SIZE = 128

N_META = 16
MIX_WIDTH = D_MODEL
POOL_WIDTH = MIX_WIDTH // 2
POOL_WINDOWS = (2, 4, 8, 16)
POOL_GROUP = POOL_WIDTH // len(POOL_WINDOWS)
POOL_HIST = max(POOL_WINDOWS) - 1
ATTN_WIDTH = MIX_WIDTH - POOL_WIDTH
HEAD_DIM = 64
N_HEADS = ATTN_WIDTH // HEAD_DIM
SB_BLOCK = 128
SB_BIAS_HI = -2.0
SB_BIAS_LO = -9.0
N_GROUPS = 4
EXPERTS_PER_GROUP = 8
N_EXPERTS = N_GROUPS * EXPERTS_PER_GROUP
TOP_K = 2
D_EXPERT = D_MODEL // 2
MOE_BLOCK = 64
EPS = 1e-6

kernel_name = 'hymba_pool_stickbreak_hmoe_step'


def rms_norm(x, g):
    xf = x.astype(jnp.float32)
    y = xf * lax.rsqrt(jnp.mean(xf * xf, axis=-1, keepdims=True) + EPS)
    return (y * g.astype(jnp.float32)).astype(x.dtype)


def mixer_projections(x, norm_g, w_in, g_q, g_k):
    B, T = x.shape[:2]
    n = rms_norm(x, norm_g)
    proj = jnp.einsum('btd,de->bte', n, w_in)
    u, q, k, v = jnp.split(proj, [POOL_WIDTH, POOL_WIDTH + ATTN_WIDTH, POOL_WIDTH + 2 * ATTN_WIDTH], axis=-1)
    heads = lambda a: a.reshape(B, T, N_HEADS, HEAD_DIM)
    q = rms_norm(heads(q), g_q)
    k = rms_norm(heads(k), g_k)
    return u, q, k, heads(v)


def pool_mixer(u, hist, p0, w_pool, pool_scale):
    T = u.shape[1]
    ext = jnp.concatenate([hist.astype(u.dtype), u], axis=1).astype(jnp.float32)
    cs = jnp.concatenate([jnp.zeros_like(ext[:, :1]), jnp.cumsum(ext, axis=1)], axis=1)
    pos = p0 + jnp.arange(T)
    outs = []
    for g, w in enumerate(POOL_WINDOWS):
        sl = slice(g * POOL_GROUP, (g + 1) * POOL_GROUP)
        win_sum = cs[:, POOL_HIST + 1:POOL_HIST + 1 + T, sl] - cs[:, POOL_HIST + 1 - w:POOL_HIST + 1 - w + T, sl]
        count = jnp.minimum(w, pos + 1).astype(jnp.float32)
        d = win_sum / count[None, :, None] - ext[:, POOL_HIST:, sl]
        outs.append(jnp.einsum('btc,ce->bte', d, w_pool[g].astype(jnp.float32)))
    y = jnp.concatenate(outs, axis=-1) * pool_scale.astype(jnp.float32)
    return y.astype(u.dtype), ext[:, -POOL_HIST:].astype(u.dtype)


def stick_breaking(q, k, v, q_pos, k_pos, sb_bias):
    z = jnp.einsum('bqhd,bkhd->bhqk', q, k).astype(jnp.float32) * (HEAD_DIM ** -0.5)
    z = z + sb_bias.astype(jnp.float32)[None, :, None, None]
    visible = k_pos[None, :] < q_pos[:, None]
    log_beta = jax.nn.log_sigmoid(z)
    log_keep = jnp.where(visible, jax.nn.log_sigmoid(-z), 0.0)
    later = lax.cumsum(log_keep, axis=3, reverse=True) - log_keep
    a = jnp.where(visible, jnp.exp(log_beta + later), 0.0)
    return jnp.einsum('bhqk,bkhd->bqhd', a.astype(v.dtype), v)


def stick_breaking_prompt(q, k, v, sb_bias):
    B, L = q.shape[:2]
    lead = (-N_META) % SB_BLOCK
    nb = -(-(L + lead) // SB_BLOCK)
    tail = nb * SB_BLOCK - L - lead
    qp = jnp.pad(q, ((0, 0), (lead, tail), (0, 0), (0, 0)))
    k_pos = jnp.arange(L)

    def block(b):
        qb = lax.dynamic_slice_in_dim(qp, b * SB_BLOCK, SB_BLOCK, axis=1)
        q_pos = b * SB_BLOCK - lead + jnp.arange(SB_BLOCK)
        return stick_breaking(qb, k, v, q_pos, k_pos, sb_bias)

    o = lax.map(block, jnp.arange(nb))
    o = jnp.moveaxis(o, 0, 1).reshape(B, nb * SB_BLOCK, N_HEADS, HEAD_DIM)
    return o[:, lead:lead + L]


def merge_heads(pool_o, attn, w_out):
    B, T = pool_o.shape[:2]
    cat = jnp.concatenate([pool_o, attn.reshape(B, T, ATTN_WIDTH)], axis=-1)
    return jnp.einsum('bte,ed->btd', cat, w_out)


def hier_moe(h, w_group, w_expert, w_gate, w_up, w_down):
    T, D = h.shape
    hf = h.astype(jnp.float32)
    p_group = jax.nn.softmax(hf @ w_group.astype(jnp.float32), axis=-1)
    g_sel = jnp.argmax(p_group, axis=-1)
    g_w = jnp.take_along_axis(p_group, g_sel[:, None], axis=-1)[:, 0]
    logits_e = (hf @ w_expert.astype(jnp.float32)).reshape(T, N_GROUPS, EXPERTS_PER_GROUP)
    le = jnp.take_along_axis(logits_e, g_sel[:, None, None], axis=1)[:, 0]
    top_v, top_i = lax.top_k(le, TOP_K)
    gate = g_w[:, None] * jax.nn.softmax(top_v, axis=-1)
    expert_id = (g_sel[:, None] * EXPERTS_PER_GROUP + top_i).astype(jnp.int32)
    M = T * TOP_K
    e_flat = expert_id.reshape(M)
    tok = jnp.repeat(jnp.arange(T, dtype=jnp.int32), TOP_K)
    wt = gate.reshape(M)
    order = jnp.argsort(e_flat)
    e_s = e_flat[order]
    counts = jnp.bincount(e_flat, length=N_EXPERTS)
    padded = (counts + MOE_BLOCK - 1) // MOE_BLOCK * MOE_BLOCK
    pad_end = jnp.cumsum(padded)
    pad_start = pad_end - padded
    start = jnp.cumsum(counts) - counts
    dest = pad_start[e_s] + jnp.arange(M) - start[e_s]
    P = (-(-M // MOE_BLOCK) + N_EXPERTS) * MOE_BLOCK
    nblk = P // MOE_BLOCK
    row_tok = jnp.full((P,), T, jnp.int32).at[dest].set(tok[order])
    row_w = jnp.zeros((P,), h.dtype).at[dest].set(wt[order].astype(h.dtype))
    blk_exp = jnp.clip(jnp.searchsorted(pad_end, jnp.arange(nblk) * MOE_BLOCK, side='right'), 0, N_EXPERTS - 1)
    h_pad = jnp.concatenate([h, jnp.zeros((1, D), h.dtype)], axis=0)
    xs = h_pad[row_tok].reshape(nblk, MOE_BLOCK, D)

    def expert_block(args):
        xb, e = args
        return (jax.nn.silu(xb @ w_gate[e]) * (xb @ w_up[e])) @ w_down[e]

    ys = lax.map(expert_block, (xs, blk_exp)).reshape(P, D)
    out = jax.ops.segment_sum(ys * row_w[:, None], row_tok, num_segments=T + 1)[:T]
    return out.astype(h.dtype)


def channel_mixer(x, norm_g, w_group, w_expert, w_gate, w_up, w_down):
    B, T, D = x.shape
    n = rms_norm(x, norm_g).reshape(B * T, D)
    return x + hier_moe(n, w_group, w_expert, w_gate, w_up, w_down).reshape(B, T, D)


def setup_inputs(seed: int = 0) -> dict:
    key = jax.random.key(seed)
    ks = jax.random.split(key, 24)
    f32 = jnp.float32
    n_pages = PAST_LEN // PAGE_SIZE
    n_used = DEC_BATCH * n_pages
    n_phys = n_used + n_used // 4
    nrm = lambda k, shape, s: jax.random.normal(k, shape, f32) * s
    page_table = jax.random.permutation(ks[5], n_phys)[:n_used].reshape(DEC_BATCH, n_pages).astype(jnp.int32)
    sb_bias = jnp.broadcast_to(jnp.linspace(SB_BIAS_HI, SB_BIAS_LO, N_HEADS, dtype=f32), (DEPTH, N_HEADS)) + nrm(ks[20], (DEPTH, N_HEADS), 0.05)
    return {
        'x_prompt': nrm(ks[0], (BATCH, SEQ, D_MODEL), 1.0),
        'x_sample': nrm(ks[1], (DEC_BATCH, DEC_SEQ, D_MODEL), 1.0),
        'cache_k': nrm(ks[2], (DEPTH, n_phys, PAGE_SIZE, N_HEADS, HEAD_DIM), 1.0),
        'cache_v': nrm(ks[3], (DEPTH, n_phys, PAGE_SIZE, N_HEADS, HEAD_DIM), 1.0),
        'state_pool': nrm(ks[4], (DEPTH, DEC_BATCH, POOL_HIST, POOL_WIDTH), 1.0),
        'page_table': page_table,
        'meta_tokens': nrm(ks[6], (N_META, D_MODEL), 1.0),
        'norm_mix': 1.0 + nrm(ks[7], (DEPTH, D_MODEL), 0.02),
        'w_in': nrm(ks[8], (DEPTH, D_MODEL, POOL_WIDTH + 3 * ATTN_WIDTH), D_MODEL ** -0.5),
        'g_q': 1.0 + nrm(ks[9], (DEPTH, HEAD_DIM), 0.02),
        'g_k': 1.0 + nrm(ks[10], (DEPTH, HEAD_DIM), 0.02),
        'sb_bias': sb_bias,
        'w_pool': nrm(ks[11], (DEPTH, len(POOL_WINDOWS), POOL_GROUP, POOL_GROUP), POOL_GROUP ** -0.5),
        'pool_scale': 1.0 + nrm(ks[12], (DEPTH, POOL_WIDTH), 0.02),
        'w_out': nrm(ks[13], (DEPTH, MIX_WIDTH, D_MODEL), MIX_WIDTH ** -0.5),
        'norm_ffn': 1.0 + nrm(ks[14], (DEPTH, D_MODEL), 0.02),
        'w_group': nrm(ks[15], (DEPTH, D_MODEL, N_GROUPS), D_MODEL ** -0.5),
        'w_expert': nrm(ks[16], (DEPTH, D_MODEL, N_EXPERTS), D_MODEL ** -0.5),
        'w_gate': nrm(ks[17], (DEPTH, N_EXPERTS, D_MODEL, D_EXPERT), D_MODEL ** -0.5),
        'w_up': nrm(ks[18], (DEPTH, N_EXPERTS, D_MODEL, D_EXPERT), D_MODEL ** -0.5),
        'w_down': nrm(ks[19], (DEPTH, N_EXPERTS, D_EXPERT, D_MODEL), D_EXPERT ** -0.5),
    }


def reference(x_prompt, x_sample, cache_k, cache_v, state_pool, page_table, meta_tokens,
              norm_mix, w_in, g_q, g_k, sb_bias, w_pool, pool_scale, w_out, norm_ffn,
              w_group, w_expert, w_gate, w_up, w_down):
    B = x_prompt.shape[0]
    DB, Ts = x_sample.shape[:2]
    past = page_table.shape[1] * PAGE_SIZE
    meta = jnp.broadcast_to(meta_tokens[None].astype(x_prompt.dtype), (B, N_META, D_MODEL))
    xp = jnp.concatenate([meta, x_prompt], axis=1)
    xs = x_sample
    kp_l, vp_l, pp_l, ks_l, vs_l, ps_l = [], [], [], [], [], []
    for l in range(DEPTH):
        u, q, k, v = mixer_projections(xp, norm_mix[l], w_in[l], g_q[l], g_k[l])
        pool_o, pool_tail = pool_mixer(u, jnp.zeros((B, POOL_HIST, POOL_WIDTH), u.dtype), 0, w_pool[l], pool_scale[l])
        attn = stick_breaking_prompt(q, k, v, sb_bias[l])
        xp = xp + merge_heads(pool_o, attn, w_out[l])
        xp = channel_mixer(xp, norm_ffn[l], w_group[l], w_expert[l], w_gate[l], w_up[l], w_down[l])
        kp_l.append(k)
        vp_l.append(v)
        pp_l.append(pool_tail)
        u, q, k, v = mixer_projections(xs, norm_mix[l], w_in[l], g_q[l], g_k[l])
        pool_o, pool_tail = pool_mixer(u, state_pool[l], PAST_LEN, w_pool[l], pool_scale[l])
        past_k = cache_k[l][page_table].reshape(DB, past, N_HEADS, HEAD_DIM).astype(k.dtype)
        past_v = cache_v[l][page_table].reshape(DB, past, N_HEADS, HEAD_DIM).astype(v.dtype)
        k_all = jnp.concatenate([past_k, k], axis=1)
        v_all = jnp.concatenate([past_v, v], axis=1)
        attn = stick_breaking(q, k_all, v_all, past + jnp.arange(Ts), jnp.arange(past + Ts), sb_bias[l])
        xs = xs + merge_heads(pool_o, attn, w_out[l])
        xs = channel_mixer(xs, norm_ffn[l], w_group[l], w_expert[l], w_gate[l], w_up[l], w_down[l])
        ks_l.append(k)
        vs_l.append(v)
        ps_l.append(pool_tail)
    y_prompt = xp[:, N_META:]
    y_sample = xs
    k_prompt = jnp.stack(kp_l)
    v_prompt = jnp.stack(vp_l)
    pool_prompt = jnp.stack(pp_l)
    k_sample = jnp.stack(ks_l)
    v_sample = jnp.stack(vs_l)
    pool_sample = jnp.stack(ps_l)
    return (y_prompt, y_sample, k_prompt, v_prompt, pool_prompt, k_sample, v_sample, pool_sample)
```

```python
import functools

import jax
import jax.numpy as jnp
from jax import lax
from jax.experimental import pallas as pl
from jax.experimental.pallas import tpu as pltpu

F32 = jnp.float32
BF16 = jnp.bfloat16

D_MODEL = 1024
POOL_WIDTH = 512
ATTN_WIDTH = 512
HEAD_DIM = 64
N_HEADS = 8
N_PAIRS = N_HEADS // 2
POOL_WINDOWS = (2, 4, 8, 16)
POOL_GROUP = 128
POOL_HIST = 15
HIST_ROWS = 16
N_META = 16
PAGE = 128
N_GROUPS = 4
EXPERTS_PER_GROUP = 8
N_EXPERTS = 32
D_EXPERT = 512
EPS = 1e-6
NEG = -1e30
LANES = 128
ATT_BLOCK = 128
PROJ_ROWS = 512
MERGE_ROWS = 256
EXPERT_ROWS = 256
VMEM_LIMIT = 48 * 1024 * 1024


def _cparams(sem, vmem=VMEM_LIMIT):
    return pltpu.CompilerParams(dimension_semantics=sem, vmem_limit_bytes=vmem)


def _rms_rows(x, g):
    ms = jnp.mean(x * x, axis=-1, keepdims=True)
    return (x * lax.rsqrt(ms + EPS)) * g


def _head_norm(a, g, hsum):
    sq = a * a
    hi = sq.astype(BF16)
    lo = (sq - hi.astype(F32)).astype(BF16)
    ssum = (jnp.dot(hi, hsum, preferred_element_type=F32)
            + jnp.dot(lo, hsum, preferred_element_type=F32))
    return (a * lax.rsqrt(ssum * (1.0 / HEAD_DIM) + EPS)) * g


def _project(x, nrm, w_in, gq, gk, hsum):
    n = _rms_rows(x, nrm)
    proj = jnp.dot(n.astype(BF16), w_in, preferred_element_type=F32)
    u = proj[:, 0:POOL_WIDTH]
    q = _head_norm(proj[:, POOL_WIDTH:POOL_WIDTH + ATTN_WIDTH], gq, hsum)
    k = _head_norm(proj[:, POOL_WIDTH + ATTN_WIDTH:POOL_WIDTH + 2 * ATTN_WIDTH], gk, hsum)
    v = proj[:, POOL_WIDTH + 2 * ATTN_WIDTH:]
    return u, q, k, v


def _split_heads(a, out_ref):
    rows = a.shape[0]
    lane = lax.broadcasted_iota(jnp.int32, (rows, LANES), 1)
    first = lane < HEAD_DIM
    for p in range(N_PAIRS):
        a2 = a[:, p * LANES:(p + 1) * LANES]
        out_ref[:, 2 * p * LANES:(2 * p + 1) * LANES] = jnp.where(first, a2, 0.0).astype(out_ref.dtype)
        out_ref[:, (2 * p + 1) * LANES:(2 * p + 2) * LANES] = jnp.where(first, 0.0, a2).astype(out_ref.dtype)


def _stick_tile(q2, kxb, vxb, brow, tt, carry, mask):
    half = kxb.shape[1] // 2
    kx = jnp.concatenate([kxb[:, :half], kxb[:, half:]], axis=0)
    vx = jnp.concatenate([vxb[:, :half], vxb[:, half:]], axis=0)
    z = lax.dot_general(q2, kx, (((1,), (1,)), ((), ())), preferred_element_type=F32) + brow
    if mask is not None:
        z = jnp.where(mask, z, NEG)
    soft = jnp.log(1.0 + jnp.exp(-jnp.abs(z)))
    log_beta = jnp.minimum(z, 0.0) - soft
    log_keep = log_beta - z
    sums = jnp.dot(log_keep.astype(BF16), tt, preferred_element_type=F32)
    nk = z.shape[1]
    a = jnp.exp(log_beta + sums[:, :nk] + carry)
    out = jnp.dot(a.astype(BF16), vx, preferred_element_type=F32)
    return out, carry + sums[:, nk:]


def _proj_seq_kernel(x_ref, hist_ref, nrm_ref, win_ref, gq_ref, gk_ref, hsum_ref, wpool_ref, ps_ref,
                     q_ref, k_ref, v_ref, kx_ref, vx_ref, po_ref, tail_ref, ext_ref, *, tm, p0, full_count):
    blk = pl.program_id(1)
    u, q, k, v = _project(x_ref[...], nrm_ref[...], win_ref[...], gq_ref[...], gk_ref[...], hsum_ref[...])
    q_ref[...] = q.astype(q_ref.dtype)
    k_ref[...] = k
    v_ref[...] = v
    _split_heads(k, kx_ref)
    _split_heads(v, vx_ref)

    @pl.when(blk == 0)
    def _():
        ext_ref[0:HIST_ROWS, :] = hist_ref[...]

    ext_ref[HIST_ROWS:HIST_ROWS + tm, :] = u
    for g, w in enumerate(POOL_WINDOWS):
        e = ext_ref[:, g * POOL_GROUP:(g + 1) * POOL_GROUP]
        s = e
        sh = 1
        while sh < w:
            s = s + pltpu.roll(s, sh, axis=0)
            sh *= 2
        win = s[HIST_ROWS:, :]
        tok = e[HIST_ROWS:, :]
        if full_count:
            d = win * (1.0 / w) - tok
        else:
            pos = p0 + lax.broadcasted_iota(jnp.int32, (tm, 1), 0)
            cnt = jnp.minimum(w, pos + 1).astype(F32)
            d = win / cnt - tok
        y = jnp.dot(d.astype(BF16), wpool_ref[g], preferred_element_type=F32)
        y = y * ps_ref[:, g * POOL_GROUP:(g + 1) * POOL_GROUP]
        po_ref[:, g * POOL_GROUP:(g + 1) * POOL_GROUP] = y.astype(po_ref.dtype)
    last = ext_ref[tm:tm + HIST_ROWS, :]
    ext_ref[0:HIST_ROWS, :] = last

    @pl.when(blk == pl.num_programs(1) - 1)
    def _():
        tail_ref[...] = last


def _proj_seq(x, hist, cw, *, tm, p0, full_count):
    b, l, _ = x.shape
    nb = l // tm
    const = lambda shape: pl.BlockSpec(shape, lambda i, j: (0,) * len(shape))
    row = lambda width: pl.BlockSpec((None, tm, width), lambda i, j: (i, j, 0))
    kern = functools.partial(_proj_seq_kernel, tm=tm, p0=p0, full_count=full_count)
    return pl.pallas_call(
        kern,
        grid=(b, nb),
        in_specs=[row(D_MODEL), const((HIST_ROWS, POOL_WIDTH)), const((1, D_MODEL)),
                  const((D_MODEL, 2 * D_MODEL)), const((1, ATTN_WIDTH)), const((1, ATTN_WIDTH)),
                  const((ATTN_WIDTH, ATTN_WIDTH)), const((4, POOL_GROUP, POOL_GROUP)), const((1, POOL_WIDTH))],
        out_specs=[row(ATTN_WIDTH), row(ATTN_WIDTH), row(ATTN_WIDTH), row(2 * ATTN_WIDTH), row(2 * ATTN_WIDTH),
                   row(POOL_WIDTH), pl.BlockSpec((None, HIST_ROWS, POOL_WIDTH), lambda i, j: (i, 0, 0))],
        out_shape=[jax.ShapeDtypeStruct((b, l, ATTN_WIDTH), BF16),
                   jax.ShapeDtypeStruct((b, l, ATTN_WIDTH), F32),
                   jax.ShapeDtypeStruct((b, l, ATTN_WIDTH), F32),
                   jax.ShapeDtypeStruct((b, l, 2 * ATTN_WIDTH), BF16),
                   jax.ShapeDtypeStruct((b, l, 2 * ATTN_WIDTH), BF16),
                   jax.ShapeDtypeStruct((b, l, POOL_WIDTH), BF16),
                   jax.ShapeDtypeStruct((b, HIST_ROWS, POOL_WIDTH), F32)],
        scratch_shapes=[pltpu.VMEM((HIST_ROWS + tm, POOL_WIDTH), F32)],
        compiler_params=_cparams(("arbitrary", "arbitrary")),
        name="proj_seq",
    )(x, hist, cw["nrm"], cw["w_in"], cw["gq"], cw["gk"], cw["hsum"], cw["w_pool"], cw["pool_scale"])


def _proj_step_kernel(x_ref, hist_ref, nrm_ref, win_ref, gq_ref, gk_ref, hsum_ref, wpool_ref, ps_ref,
                      q_ref, k_ref, v_ref, po_ref, tail_ref, *, nseq, nt):
    u, q, k, v = _project(x_ref[...], nrm_ref[...], win_ref[...], gq_ref[...], gk_ref[...], hsum_ref[...])
    q_ref[...] = q.astype(q_ref.dtype)
    k_ref[...] = k
    v_ref[...] = v
    ext = [hist_ref[r] for r in range(POOL_HIST)] + [u[t * nseq:(t + 1) * nseq, :] for t in range(nt)]
    for r in range(POOL_HIST):
        tail_ref[r] = ext[nt + r]
    for g, w in enumerate(POOL_WINDOWS):
        sl = slice(g * POOL_GROUP, (g + 1) * POOL_GROUP)
        ds = []
        for t in range(nt):
            win = ext[POOL_HIST + t][:, sl]
            for i in range(1, w):
                win = win + ext[POOL_HIST + t - i][:, sl]
            ds.append(win * (1.0 / w) - ext[POOL_HIST + t][:, sl])
        d = jnp.concatenate(ds, axis=0)
        y = jnp.dot(d.astype(BF16), wpool_ref[g], preferred_element_type=F32) * ps_ref[:, sl]
        po_ref[:, sl] = y.astype(po_ref.dtype)


def _proj_step(x_tm, hist_tm, cw, *, nseq, nt):
    rows = nseq * nt
    kern = functools.partial(_proj_step_kernel, nseq=nseq, nt=nt)
    return pl.pallas_call(
        kern,
        out_shape=[jax.ShapeDtypeStruct((rows, ATTN_WIDTH), BF16),
                   jax.ShapeDtypeStruct((rows, ATTN_WIDTH), F32),
                   jax.ShapeDtypeStruct((rows, ATTN_WIDTH), F32),
                   jax.ShapeDtypeStruct((rows, POOL_WIDTH), BF16),
                   jax.ShapeDtypeStruct((POOL_HIST, nseq, POOL_WIDTH), F32)],
        compiler_params=pltpu.CompilerParams(vmem_limit_bytes=VMEM_LIMIT),
        name="proj_step",
    )(x_tm, hist_tm, cw["nrm"], cw["w_in"], cw["gq"], cw["gk"], cw["hsum"], cw["w_pool"], cw["pool_scale"])


def _attn_prompt_kernel(q_ref, kx_ref, vx_ref, kxm_ref, vxm_ref, bias_ref, tt_ref, o_ref, carry_ref, acc_ref):
    qi = pl.program_id(2)
    q2 = q_ref[...]
    tt = tt_ref[...]
    brow = bias_ref[0:1, :]
    nq = q2.shape[0]

    def rows(j):
        return pl.ds(pl.multiple_of(j * ATT_BLOCK, ATT_BLOCK), ATT_BLOCK)

    r_i = lax.broadcasted_iota(jnp.int32, (nq, 2 * ATT_BLOCK), 0)
    c_i = lax.broadcasted_iota(jnp.int32, (nq, 2 * ATT_BLOCK), 1) & (ATT_BLOCK - 1)
    out, carry = _stick_tile(q2, kx_ref[rows(qi), :], vx_ref[rows(qi), :], brow, tt,
                             jnp.zeros((nq, 2 * ATT_BLOCK), F32), c_i < r_i)
    acc_ref[...] = out
    carry_ref[...] = carry

    def body(t, _):
        j = qi - 1 - t
        out, carry = _stick_tile(q2, kx_ref[rows(j), :], vx_ref[rows(j), :], brow, tt, carry_ref[...], None)
        acc_ref[...] += out
        carry_ref[...] = carry
        return 0

    lax.fori_loop(0, qi, body, 0)
    out, _ = _stick_tile(q2, kxm_ref[...], vxm_ref[...], bias_ref[1:2, :], tt, carry_ref[...], None)
    o_ref[...] = (acc_ref[...] + out).astype(o_ref.dtype)


def _attn_prompt(q, kx, vx, kxm, vxm, bias_rows, tt):
    b, l, _ = q.shape
    nq = l // ATT_BLOCK
    return pl.pallas_call(
        _attn_prompt_kernel,
        grid=(b, N_PAIRS, nq),
        in_specs=[pl.BlockSpec((None, ATT_BLOCK, LANES), lambda i, p, j: (i, j, p)),
                  pl.BlockSpec((None, l, 2 * LANES), lambda i, p, j: (i, 0, p)),
                  pl.BlockSpec((None, l, 2 * LANES), lambda i, p, j: (i, 0, p)),
                  pl.BlockSpec((ATT_BLOCK, 2 * LANES), lambda i, p, j: (0, p)),
                  pl.BlockSpec((ATT_BLOCK, 2 * LANES), lambda i, p, j: (0, p)),
                  pl.BlockSpec((None, 2, 2 * LANES), lambda i, p, j: (p, 0, 0)),
                  pl.BlockSpec((2 * ATT_BLOCK, 4 * ATT_BLOCK), lambda i, p, j: (0, 0))],
        out_specs=pl.BlockSpec((None, ATT_BLOCK, LANES), lambda i, p, j: (i, j, p)),
        out_shape=jax.ShapeDtypeStruct((b, l, ATTN_WIDTH), BF16),
        scratch_shapes=[pltpu.VMEM((ATT_BLOCK, 2 * ATT_BLOCK), F32), pltpu.VMEM((ATT_BLOCK, LANES), F32)],
        compiler_params=_cparams(("arbitrary", "arbitrary", "arbitrary")),
        name="attn_prompt",
    )(q, kx, vx, kxm, vxm, bias_rows, tt)


def _attn_sample_kernel(pt_ref, qbd_ref, kn_ref, vn_ref, bias_ref, tt_ref, *refs, n_pages, nt):
    del pt_ref
    k_refs = refs[:n_pages]
    v_refs = refs[n_pages:2 * n_pages]
    o_ref = refs[2 * n_pages]
    qbd = qbd_ref[...]
    tt = tt_ref[...]
    bias = bias_ref[...]
    nc = qbd.shape[0]

    def tile(kb, vb, carry, mask):
        z = lax.dot_general(qbd, kb, (((1,), (1,)), ((), ())), preferred_element_type=F32) + bias
        if mask is not None:
            z = jnp.where(mask, z, NEG)
        soft = jnp.log(1.0 + jnp.exp(-jnp.abs(z)))
        log_beta = jnp.minimum(z, 0.0) - soft
        log_keep = log_beta - z
        sums = jnp.dot(log_keep.astype(BF16), tt, preferred_element_type=F32)
        a = jnp.exp(log_beta + sums[:, :PAGE] + carry)
        return jnp.dot(a.astype(BF16), vb, preferred_element_type=F32), carry + sums[:, PAGE:]

    pad = jnp.zeros((PAGE - kn_ref.shape[0], ATTN_WIDTH), F32)
    kn = jnp.concatenate([kn_ref[...], pad], axis=0).astype(BF16)
    vn = jnp.concatenate([vn_ref[...], pad], axis=0).astype(BF16)
    t_i = lax.broadcasted_iota(jnp.int32, (nc, PAGE), 0) >> 3
    k_i = lax.broadcasted_iota(jnp.int32, (nc, PAGE), 1)
    acc, carry = tile(kn, vn, jnp.zeros((nc, PAGE), F32), k_i < t_i)
    for j in range(n_pages - 1, -1, -1):
        out, carry = tile(k_refs[j][...].astype(BF16), v_refs[j][...].astype(BF16), carry, None)
        acc = acc + out
    h_i = lax.broadcasted_iota(jnp.int32, (nc, ATTN_WIDTH), 0) & (N_HEADS - 1)
    l_i = lax.broadcasted_iota(jnp.int32, (nc, ATTN_WIDTH), 1) >> 6
    acc = jnp.where(h_i == l_i, acc, 0.0)
    for t in range(nt):
        o_ref[t:t + 1, :] = jnp.sum(acc[t * N_HEADS:(t + 1) * N_HEADS, :], axis=0, keepdims=True)


def _attn_sample(page_table, qbd, kn, vn, bias_col, tt, cache_k, cache_v):
    nseq, n_pages = page_table.shape
    nt = qbd.shape[1] // N_HEADS
    pt = page_table.reshape(-1)

    def page_spec(j):
        return pl.BlockSpec((None, PAGE, ATTN_WIDTH), lambda i, pt_ref: (pt_ref[i * n_pages + j], 0, 0))

    grid_spec = pltpu.PrefetchScalarGridSpec(
        num_scalar_prefetch=1,
        grid=(nseq,),
        in_specs=[pl.BlockSpec((None, nt * N_HEADS, ATTN_WIDTH), lambda i, pt_ref: (i, 0, 0)),
                  pl.BlockSpec((None, 8, ATTN_WIDTH), lambda i, pt_ref: (i, 0, 0)),
                  pl.BlockSpec((None, 8, ATTN_WIDTH), lambda i, pt_ref: (i, 0, 0)),
                  pl.BlockSpec((nt * N_HEADS, 1), lambda i, pt_ref: (0, 0)),
                  pl.BlockSpec((PAGE, 2 * PAGE), lambda i, pt_ref: (0, 0))]
                 + [page_spec(j) for j in range(n_pages)] + [page_spec(j) for j in range(n_pages)],
        out_specs=pl.BlockSpec((None, nt, ATTN_WIDTH), lambda i, pt_ref: (i, 0, 0)),
    )
    kern = functools.partial(_attn_sample_kernel, n_pages=n_pages, nt=nt)
    return pl.pallas_call(
        kern,
        grid_spec=grid_spec,
        out_shape=jax.ShapeDtypeStruct((nseq, nt, ATTN_WIDTH), F32),
        compiler_params=_cparams(("arbitrary",)),
        name="attn_sample",
    )(pt, qbd, kn, vn, bias_col, tt, *([cache_k] * n_pages), *([cache_v] * n_pages))


def _merge_router_kernel(x_ref, po_ref, at_ref, wop_ref, woa_ref, nrm_ref, wrh_ref, wrl_ref, tri_ref, cin_ref,
                         h_in_ref, x1_ref, h_ref, meta_ref, cout_ref, carry_ref):
    del h_in_ref
    i = pl.program_id(0)

    @pl.when(i == 0)
    def _():
        carry_ref[...] = cin_ref[...]

    y = (jnp.dot(po_ref[...].astype(BF16), wop_ref[...], preferred_element_type=F32)
         + jnp.dot(at_ref[...].astype(BF16), woa_ref[...], preferred_element_type=F32))
    x1 = x_ref[...] + y
    x1_ref[...] = x1
    h = _rms_rows(x1, nrm_ref[...])
    h_ref[...] = h
    hi = h.astype(BF16)
    lo = (h - hi.astype(F32)).astype(BF16)
    logits = (jnp.dot(hi, wrh_ref[...], preferred_element_type=F32)
              + jnp.dot(lo, wrh_ref[...], preferred_element_type=F32)
              + jnp.dot(hi, wrl_ref[...], preferred_element_type=F32))
    tm = logits.shape[0]
    lane = lax.broadcasted_iota(jnp.int32, (tm, LANES), 1).astype(F32)
    ninf = -jnp.inf
    is_g = lane < N_GROUPS
    gl = jnp.where(is_g, logits, ninf)
    gmax = jnp.max(gl, axis=-1, keepdims=True)
    g_sel = jnp.min(jnp.where(gl == gmax, lane, float(LANES)), axis=-1, keepdims=True)
    psum = jnp.sum(jnp.where(is_g, jnp.exp(logits - gmax), 0.0), axis=-1, keepdims=True)
    g_w = 1.0 / psum
    e_lo = N_GROUPS + g_sel * EXPERTS_PER_GROUP
    in_group = (lane >= e_lo) & (lane < e_lo + EXPERTS_PER_GROUP)
    el = jnp.where(in_group, logits, ninf)
    v1 = jnp.max(el, axis=-1, keepdims=True)
    i1 = jnp.min(jnp.where(el == v1, lane, float(LANES)), axis=-1, keepdims=True)
    el2 = jnp.where(lane == i1, ninf, el)
    v2 = jnp.max(el2, axis=-1, keepdims=True)
    i2 = jnp.min(jnp.where(el2 == v2, lane, float(LANES)), axis=-1, keepdims=True)
    e21 = jnp.exp(v2 - v1)
    den = 1.0 + e21
    gate1 = g_w * (1.0 / den)
    gate2 = g_w * (e21 / den)
    e1 = i1 - N_GROUPS
    e2 = i2 - N_GROUPS
    onehot = jnp.where((lane == e1) | (lane == e2), 1.0, 0.0)
    before = jnp.dot(tri_ref[...], onehot.astype(BF16), preferred_element_type=F32) + carry_ref[...]
    rank1 = jnp.sum(jnp.where(lane == e1, before, 0.0), axis=-1, keepdims=True)
    rank2 = jnp.sum(jnp.where(lane == e2, before, 0.0), axis=-1, keepdims=True)
    carry = carry_ref[...] + jnp.sum(onehot, axis=0, keepdims=True)
    carry_ref[...] = carry
    cout_ref[...] = carry
    meta = jnp.where(lane == 0, e1, 0.0)
    meta = jnp.where(lane == 1, e2, meta)
    meta = jnp.where(lane == 2, rank1, meta)
    meta = jnp.where(lane == 3, rank2, meta)
    meta = jnp.where(lane == 4, gate1, meta)
    meta = jnp.where(lane == 5, gate2, meta)
    meta_ref[...] = meta


def _merge_router(x, pool_o, attn, cw, carry_in, h_all, row_off):
    t = x.shape[0]
    tm = MERGE_ROWS
    nb = t // tm
    boff = row_off // tm
    const = lambda shape: pl.BlockSpec(shape, lambda i: (0,) * len(shape))
    row = lambda width: pl.BlockSpec((tm, width), lambda i: (i, 0))
    return pl.pallas_call(
        _merge_router_kernel,
        grid=(nb,),
        in_specs=[row(D_MODEL), row(POOL_WIDTH), row(ATTN_WIDTH), const((POOL_WIDTH, D_MODEL)),
                  const((ATTN_WIDTH, D_MODEL)), const((1, D_MODEL)), const((D_MODEL, LANES)),
                  const((D_MODEL, LANES)), const((tm, tm)), const((1, LANES)),
                  pl.BlockSpec(memory_space=pl.ANY)],
        out_specs=[row(D_MODEL), pl.BlockSpec((tm, D_MODEL), lambda i: (i + boff, 0)), row(LANES),
                   const((1, LANES))],
        out_shape=[jax.ShapeDtypeStruct((t, D_MODEL), F32),
                   jax.ShapeDtypeStruct(h_all.shape, F32),
                   jax.ShapeDtypeStruct((t, LANES), F32),
                   jax.ShapeDtypeStruct((1, LANES), F32)],
        scratch_shapes=[pltpu.VMEM((1, LANES), F32)],
        input_output_aliases={10: 1},
        compiler_params=_cparams(("arbitrary",)),
        name="merge_router",
    )(x, pool_o, attn, cw["w_out_pool"], cw["w_out_attn"], cw["nrm_ffn"], cw["w_router_hi"],
      cw["w_router_lo"], cw["tri_merge"], carry_in, h_all)


def _expert_kernel(bexp_ref, nval_ref, src_ref, dst_ref, h_hbm, wg_ref, wu_ref, wd_ref, ys_in,
                   ys_hbm, xbuf, obuf, wgb, wub, wdb, gsem, ssem):
    del ys_in
    i = pl.program_id(0)
    n = pl.multiple_of(nval_ref[i], 8)

    @pl.when(i == 0)
    def _():
        xbuf[...] = jnp.zeros_like(xbuf)

    @pl.when(n > 0)
    def _():
        def gather(r, _):
            pltpu.make_async_copy(h_hbm.at[pl.ds(src_ref[0, 0, r], 1), :], xbuf.at[pl.ds(r, 1), :], gsem).start()
            return 0

        lax.fori_loop(0, n, gather, 0)

        changed = jnp.logical_or(i == 0, bexp_ref[i] != bexp_ref[jnp.maximum(i - 1, 0)])

        @pl.when(changed)
        def _():
            wgb[...] = wg_ref[...].astype(BF16)
            wub[...] = wu_ref[...].astype(BF16)
            wdb[...] = wd_ref[...].astype(BF16)

        pltpu.make_async_copy(h_hbm.at[pl.ds(0, n), :], xbuf.at[pl.ds(0, n), :], gsem).wait()
        xb = xbuf[...].astype(BF16)
        g = jnp.dot(xb, wgb[...], preferred_element_type=F32)
        u = jnp.dot(xb, wub[...], preferred_element_type=F32)
        mid = (g * jax.nn.sigmoid(g)) * u
        obuf[...] = jnp.dot(mid.astype(BF16), wdb[...], preferred_element_type=F32)

        def scatter(r, _):
            pltpu.make_async_copy(obuf.at[pl.ds(r, 1), :], ys_hbm.at[pl.ds(dst_ref[0, 0, r], 1), :], ssem).start()
            return 0

        lax.fori_loop(0, n, scatter, 0)
        pltpu.make_async_copy(obuf.at[pl.ds(0, n), :], ys_hbm.at[pl.ds(0, n), :], ssem).wait()


def _experts(blk_exp, nvalid, row_src, row_dst, h_all, w_gate, w_up, w_down, ys_init):
    nblk = blk_exp.shape[0]
    rows = EXPERT_ROWS
    idx_spec = pl.BlockSpec((1, 1, rows), lambda i, be, nv: (i, 0, 0), memory_space=pltpu.SMEM)
    grid_spec = pltpu.PrefetchScalarGridSpec(
        num_scalar_prefetch=2,
        grid=(nblk,),
        in_specs=[idx_spec, idx_spec,
                  pl.BlockSpec(memory_space=pl.ANY),
                  pl.BlockSpec((None, D_MODEL, D_EXPERT), lambda i, be, nv: (be[i], 0, 0)),
                  pl.BlockSpec((None, D_MODEL, D_EXPERT), lambda i, be, nv: (be[i], 0, 0)),
                  pl.BlockSpec((None, D_EXPERT, D_MODEL), lambda i, be, nv: (be[i], 0, 0)),
                  pl.BlockSpec(memory_space=pl.ANY)],
        out_specs=pl.BlockSpec(memory_space=pl.ANY),
        scratch_shapes=[pltpu.VMEM((rows, D_MODEL), F32), pltpu.VMEM((rows, D_MODEL), F32),
                        pltpu.VMEM((D_MODEL, D_EXPERT), BF16), pltpu.VMEM((D_MODEL, D_EXPERT), BF16),
                        pltpu.VMEM((D_EXPERT, D_MODEL), BF16),
                        pltpu.SemaphoreType.DMA(()), pltpu.SemaphoreType.DMA(())],
    )
    return pl.pallas_call(
        _expert_kernel,
        grid_spec=grid_spec,
        out_shape=jax.ShapeDtypeStruct(ys_init.shape, F32),
        input_output_aliases={8: 0},
        compiler_params=_cparams(("arbitrary",)),
        name="experts",
    )(blk_exp, nvalid, row_src, row_dst, h_all, w_gate, w_up, w_down, ys_init)


def _combine_kernel(x1_ref, y0_ref, y1_ref, meta_ref, o_ref):
    meta = meta_ref[...]
    o_ref[...] = x1_ref[...] + meta[:, 4:5] * y0_ref[...] + meta[:, 5:6] * y1_ref[...]


def _combine(x1, ys, meta, row_off, t_all):
    t = x1.shape[0]
    tm = MERGE_ROWS
    boff = row_off // tm
    boff1 = (t_all + row_off) // tm
    return pl.pallas_call(
        _combine_kernel,
        grid=(t // tm,),
        in_specs=[pl.BlockSpec((tm, D_MODEL), lambda i: (i, 0)),
                  pl.BlockSpec((tm, D_MODEL), lambda i: (i + boff, 0)),
                  pl.BlockSpec((tm, D_MODEL), lambda i: (i + boff1, 0)),
                  pl.BlockSpec((tm, LANES), lambda i: (i, 0))],
        out_specs=pl.BlockSpec((tm, D_MODEL), lambda i: (i, 0)),
        out_shape=jax.ShapeDtypeStruct((t, D_MODEL), F32),
        compiler_params=_cparams(("arbitrary",)),
        name="combine",
    )(x1, ys, ys, meta)


def _prep_weights(norm_mix, w_in, g_q, g_k, w_pool, pool_scale, w_out, norm_ffn, w_group, w_expert):
    lane = jnp.arange(ATTN_WIDTH)
    hsum = (lane[:, None] // HEAD_DIM == lane[None, :] // HEAD_DIM).astype(BF16)
    w_router = jnp.zeros((D_MODEL, LANES), F32)
    w_router = w_router.at[:, 0:N_GROUPS].set(w_group).at[:, N_GROUPS:N_GROUPS + N_EXPERTS].set(w_expert)
    w_router_hi = w_router.astype(BF16)
    w_router_lo = (w_router - w_router_hi.astype(F32)).astype(BF16)
    r = jnp.arange(MERGE_ROWS)
    return {
        "nrm": norm_mix.reshape(1, D_MODEL),
        "w_in": w_in.astype(BF16),
        "gq": (jnp.tile(g_q, N_HEADS) * (HEAD_DIM ** -0.5)).reshape(1, ATTN_WIDTH),
        "gk": jnp.tile(g_k, N_HEADS).reshape(1, ATTN_WIDTH),
        "hsum": hsum,
        "w_pool": w_pool.astype(BF16),
        "pool_scale": pool_scale.reshape(1, POOL_WIDTH),
        "w_out_pool": w_out[:POOL_WIDTH].astype(BF16),
        "w_out_attn": w_out[POOL_WIDTH:].astype(BF16),
        "nrm_ffn": norm_ffn.reshape(1, D_MODEL),
        "w_router_hi": w_router_hi,
        "w_router_lo": w_router_lo,
        "tri_merge": (r[None, :] < r[:, None]).astype(BF16),
    }


def _later_sums(n_keys, n_heads):
    idx = jnp.arange(n_heads * n_keys)
    same = idx[:, None] // n_keys == idx[None, :] // n_keys
    later = same & (idx[:, None] > idx[None, :])
    return jnp.concatenate([later, same], axis=1).astype(BF16)


def kernel(x_prompt, x_sample, cache_k, cache_v, state_pool, page_table, meta_tokens, norm_mix, w_in, g_q, g_k,
           sb_bias, w_pool, pool_scale, w_out, norm_ffn, w_group, w_expert, w_gate, w_up, w_down):
    nb, seq, _ = x_prompt.shape
    nseq, nt, _ = x_sample.shape
    n_phys = cache_k.shape[1]
    t_prompt = nb * seq
    t_sample = nseq * nt
    t_all = t_prompt + t_sample

    cw = _prep_weights(norm_mix[0], w_in[0], g_q[0], g_k[0], w_pool[0], pool_scale[0], w_out[0], norm_ffn[0],
                       w_group[0], w_expert[0])
    bias = sb_bias[0].astype(F32)

    zero_hist = jnp.zeros((HIST_ROWS, POOL_WIDTH), F32)
    _, k_m, v_m, kx_m, vx_m, _, tail_m = _proj_seq(meta_tokens[None], zero_hist, cw, tm=N_META, p0=0,
                                                   full_count=False)
    pad_rows = ((0, ATT_BLOCK - N_META), (0, 0))
    kxm = jnp.pad(kx_m[0], pad_rows)
    vxm = jnp.pad(vx_m[0], pad_rows)

    q_p, k_p, v_p, kx_p, vx_p, po_p, tail_p = _proj_seq(x_prompt, tail_m[0], cw, tm=PROJ_ROWS, p0=N_META,
                                                        full_count=True)
    pair_bias = jnp.repeat(bias.reshape(N_PAIRS, 2), ATT_BLOCK, axis=1)
    col = jnp.arange(2 * ATT_BLOCK) % ATT_BLOCK
    meta_bias = jnp.where(col[None, :] < N_META, pair_bias, NEG)
    bias_rows = jnp.stack([pair_bias, meta_bias], axis=1)
    attn_p = _attn_prompt(q_p, kx_p, vx_p, kxm, vxm, bias_rows, _later_sums(ATT_BLOCK, 2))

    x_s_tm = jnp.transpose(x_sample, (1, 0, 2)).reshape(t_sample, D_MODEL)
    hist_tm = jnp.transpose(state_pool[0], (1, 0, 2))
    q_s, k_s, v_s, po_s, tail_s = _proj_step(x_s_tm, hist_tm, cw, nseq=nseq, nt=nt)
    to_seq = lambda a: jnp.transpose(a.reshape(nt, nseq, a.shape[-1]), (1, 0, 2))
    q_sb, k_sb, v_sb, po_sb = to_seq(q_s), to_seq(k_s), to_seq(v_s), to_seq(po_s)
    head_of_lane = jnp.arange(ATTN_WIDTH) // HEAD_DIM
    head_mask = head_of_lane[None, :] == jnp.arange(N_HEADS)[:, None]
    qbd = jnp.where(head_mask[None, None], q_sb[:, :, None, :], 0).reshape(nseq, nt * N_HEADS, ATTN_WIDTH)
    pad_new = ((0, 0), (0, 8 - nt), (0, 0))
    kn = jnp.pad(k_sb, pad_new)
    vn = jnp.pad(v_sb, pad_new)
    bias_col = jnp.tile(bias, nt).reshape(nt * N_HEADS, 1)
    attn_s = _attn_sample(page_table, qbd, kn, vn, bias_col, _later_sums(PAGE, 1),
                          cache_k[0].reshape(n_phys, PAGE, ATTN_WIDTH), cache_v[0].reshape(n_phys, PAGE, ATTN_WIDTH))

    h_all = jnp.zeros((t_all, D_MODEL), F32)
    carry0 = jnp.zeros((1, LANES), F32)
    x1_p, h_all, meta_p, carry1 = _merge_router(x_prompt.reshape(t_prompt, D_MODEL),
                                                po_p.reshape(t_prompt, POOL_WIDTH),
                                                attn_p.reshape(t_prompt, ATTN_WIDTH), cw, carry0, h_all, 0)
    x1_s, h_all, meta_s, carry2 = _merge_router(x_sample.reshape(t_sample, D_MODEL),
                                                po_sb.reshape(t_sample, POOL_WIDTH),
                                                attn_s.reshape(t_sample, ATTN_WIDTH), cw, carry1, h_all, t_prompt)

    meta_all = jnp.concatenate([meta_p[:, :4], meta_s[:, :4]], axis=0)
    pair_exp = meta_all[:, 0:2].astype(jnp.int32)
    pair_rank = meta_all[:, 2:4].astype(jnp.int32)
    counts = carry2[0, :N_EXPERTS].astype(jnp.int32)
    rows = EXPERT_ROWS
    padded = (counts + rows - 1) // rows * rows
    pad_end = jnp.cumsum(padded)
    pad_start = pad_end - padded
    nblk = -(-(2 * t_all) // rows) + N_EXPERTS
    dest = (pad_start[pair_exp] + pair_rank).reshape(-1)
    tok = jnp.repeat(jnp.arange(t_all, dtype=jnp.int32), 2)
    slot = jnp.tile(jnp.arange(2, dtype=jnp.int32), t_all)
    row_src = jnp.zeros((nblk * rows,), jnp.int32).at[dest].set(tok)
    spare = 2 * t_all + (jnp.arange(nblk * rows, dtype=jnp.int32) & 7)
    row_dst = spare.at[dest].set(slot * t_all + tok)
    blk_row0 = jnp.arange(nblk, dtype=jnp.int32) * rows
    blk_exp = jnp.clip(jnp.searchsorted(pad_end, blk_row0, side="right"), 0, N_EXPERTS - 1).astype(jnp.int32)
    nvalid = jnp.clip(counts[blk_exp] - (blk_row0 - pad_start[blk_exp]), 0, rows)
    nvalid = jnp.where(blk_row0 < pad_end[-1], (nvalid + 7) // 8 * 8, 0).astype(jnp.int32)

    ys = _experts(blk_exp, nvalid, row_src.reshape(nblk, 1, rows), row_dst.reshape(nblk, 1, rows), h_all,
                  w_gate[0], w_up[0], w_down[0], jnp.zeros((2 * t_all + 8, D_MODEL), F32))

    y_prompt = _combine(x1_p, ys, meta_p, 0, t_all).reshape(nb, seq, D_MODEL)
    y_sample = _combine(x1_s, ys, meta_s, t_prompt, t_all).reshape(nseq, nt, D_MODEL)

    heads = lambda a: a.reshape(a.shape[:-1] + (N_HEADS, HEAD_DIM))
    k_prompt = jnp.concatenate([jnp.broadcast_to(k_m, (nb, N_META, ATTN_WIDTH)), k_p], axis=1)
    v_prompt = jnp.concatenate([jnp.broadcast_to(v_m, (nb, N_META, ATTN_WIDTH)), v_p], axis=1)
    pool_sample = jnp.transpose(tail_s, (1, 0, 2))
    return (y_prompt, y_sample, heads(k_prompt)[None], heads(v_prompt)[None], tail_p[:, 1:][None],
            heads(k_sb)[None], heads(v_sb)[None], pool_sample[None])
```

```python
import functools

import jax
import jax.numpy as jnp
from jax import lax
from jax.experimental import pallas as pl
from jax.experimental.pallas import tpu as pltpu

F32 = jnp.float32
BF16 = jnp.bfloat16

D_MODEL = 1024
POOL_WIDTH = 512
ATTN_WIDTH = 512
HEAD_DIM = 64
N_HEADS = 8
N_PAIRS = N_HEADS // 2
POOL_WINDOWS = (2, 4, 8, 16)
POOL_GROUP = 128
POOL_HIST = 15
HIST_ROWS = 16
N_META = 16
PAGE = 128
N_GROUPS = 4
EXPERTS_PER_GROUP = 8
N_EXPERTS = 32
D_EXPERT = 512
EPS = 1e-6
NEG = -1e30
LOG2E = 1.4426950408889634
LANES = 128
ATT_BLOCK = 128
PROJ_ROWS = 512
MERGE_ROWS = 256
EXPERT_ROWS = 256
VMEM_LIMIT = 56 * 1024 * 1024


def _cparams(sem, vmem=VMEM_LIMIT):
    return pltpu.CompilerParams(dimension_semantics=sem, vmem_limit_bytes=vmem)


def _rms_rows(x, g):
    ms = jnp.mean(x * x, axis=-1, keepdims=True)
    return (x * lax.rsqrt(ms + EPS)) * g


def _head_norm(a, g, hsum):
    sq = a * a
    hi = sq.astype(BF16)
    lo = (sq - hi.astype(F32)).astype(BF16)
    ssum = (jnp.dot(hi, hsum, preferred_element_type=F32)
            + jnp.dot(lo, hsum, preferred_element_type=F32))
    return (a * lax.rsqrt(ssum * (1.0 / HEAD_DIM) + EPS)) * g


def _project(x, nrm, w_in, gq, gk, hsum):
    n = _rms_rows(x, nrm)
    proj = jnp.dot(n.astype(BF16), w_in, preferred_element_type=F32)
    u = proj[:, 0:POOL_WIDTH]
    q = _head_norm(proj[:, POOL_WIDTH:POOL_WIDTH + ATTN_WIDTH], gq, hsum)
    k = _head_norm(proj[:, POOL_WIDTH + ATTN_WIDTH:POOL_WIDTH + 2 * ATTN_WIDTH], gk, hsum)
    v = proj[:, POOL_WIDTH + 2 * ATTN_WIDTH:]
    return u, q, k, v


def _split_heads(a, out_ref):
    rows = a.shape[0]
    lane = lax.broadcasted_iota(jnp.int32, (rows, LANES), 1)
    first = lane < HEAD_DIM
    for p in range(N_PAIRS):
        a2 = a[:, p * LANES:(p + 1) * LANES]
        out_ref[:, 2 * p * LANES:(2 * p + 1) * LANES] = jnp.where(first, a2, 0.0).astype(out_ref.dtype)
        out_ref[:, (2 * p + 1) * LANES:(2 * p + 2) * LANES] = jnp.where(first, 0.0, a2).astype(out_ref.dtype)


def _log2_gates(z):
    soft = jnp.log(1.0 + jnp.exp2(-jnp.abs(z))) * LOG2E
    log_beta = jnp.minimum(z, 0.0) - soft
    return log_beta, log_beta - z


def _stick_tiles(q2s, kxbs, vxbs, brows, tt, carries, mask):
    def stack_heads(b):
        half = b.shape[1] // 2
        return jnp.concatenate([b[:, :half], b[:, half:]], axis=0)

    nt_dims = (((1,), (1,)), ((), ()))
    zs = [lax.dot_general(q2, stack_heads(kxb), nt_dims, preferred_element_type=F32) + brow
          for q2, kxb, brow in zip(q2s, kxbs, brows)]
    if mask is not None:
        zs = [jnp.where(mask, z, NEG) for z in zs]
    gates = [_log2_gates(z) for z in zs]
    sums = [jnp.dot(log_keep.astype(BF16), tt, preferred_element_type=F32) for _, log_keep in gates]
    nk = zs[0].shape[1]
    probs = [jnp.exp2(log_beta + s[:, :nk] + carry).astype(BF16)
             for (log_beta, _), s, carry in zip(gates, sums, carries)]
    outs = [jnp.dot(a, stack_heads(vxb), preferred_element_type=F32) for a, vxb in zip(probs, vxbs)]
    return [(out, carry + s[:, nk:]) for out, carry, s in zip(outs, carries, sums)]


def _proj_seq_kernel(x_ref, hist_ref, nrm_ref, win_ref, gq_ref, gk_ref, hsum_ref, wpool_ref, ps_ref,
                     q_ref, k_ref, v_ref, kx_ref, vx_ref, po_ref, tail_ref, ext_ref, *, tm, p0, full_count):
    blk = pl.program_id(1)
    u, q, k, v = _project(x_ref[...], nrm_ref[...], win_ref[...], gq_ref[...], gk_ref[...], hsum_ref[...])
    q_ref[...] = q.astype(q_ref.dtype)
    k_ref[...] = k
    v_ref[...] = v
    _split_heads(k, kx_ref)
    _split_heads(v, vx_ref)

    @pl.when(blk == 0)
    def _():
        ext_ref[0:HIST_ROWS, :] = hist_ref[...]

    ext_ref[HIST_ROWS:HIST_ROWS + tm, :] = u
    for g, w in enumerate(POOL_WINDOWS):
        e = ext_ref[:, g * POOL_GROUP:(g + 1) * POOL_GROUP]
        s = e
        sh = 1
        while sh < w:
            s = s + pltpu.roll(s, sh, axis=0)
            sh *= 2
        win = s[HIST_ROWS:, :]
        tok = e[HIST_ROWS:, :]
        if full_count:
            d = win * (1.0 / w) - tok
        else:
            pos = p0 + lax.broadcasted_iota(jnp.int32, (tm, 1), 0)
            cnt = jnp.minimum(w, pos + 1).astype(F32)
            d = win / cnt - tok
        y = jnp.dot(d.astype(BF16), wpool_ref[g], preferred_element_type=F32)
        y = y * ps_ref[:, g * POOL_GROUP:(g + 1) * POOL_GROUP]
        po_ref[:, g * POOL_GROUP:(g + 1) * POOL_GROUP] = y.astype(po_ref.dtype)
    last = ext_ref[tm:tm + HIST_ROWS, :]
    ext_ref[0:HIST_ROWS, :] = last

    @pl.when(blk == pl.num_programs(1) - 1)
    def _():
        tail_ref[...] = last


def _proj_seq(x, hist, cw, *, tm, p0, full_count):
    b, l, _ = x.shape
    nb = l // tm
    const = lambda shape: pl.BlockSpec(shape, lambda i, j: (0,) * len(shape))
    row = lambda width: pl.BlockSpec((None, tm, width), lambda i, j: (i, j, 0))
    kern = functools.partial(_proj_seq_kernel, tm=tm, p0=p0, full_count=full_count)
    return pl.pallas_call(
        kern,
        grid=(b, nb),
        in_specs=[row(D_MODEL), const((HIST_ROWS, POOL_WIDTH)), const((1, D_MODEL)),
                  const((D_MODEL, 2 * D_MODEL)), const((1, ATTN_WIDTH)), const((1, ATTN_WIDTH)),
                  const((ATTN_WIDTH, ATTN_WIDTH)), const((4, POOL_GROUP, POOL_GROUP)), const((1, POOL_WIDTH))],
        out_specs=[row(ATTN_WIDTH), row(ATTN_WIDTH), row(ATTN_WIDTH), row(2 * ATTN_WIDTH), row(2 * ATTN_WIDTH),
                   row(POOL_WIDTH), pl.BlockSpec((None, HIST_ROWS, POOL_WIDTH), lambda i, j: (i, 0, 0))],
        out_shape=[jax.ShapeDtypeStruct((b, l, ATTN_WIDTH), BF16),
                   jax.ShapeDtypeStruct((b, l, ATTN_WIDTH), F32),
                   jax.ShapeDtypeStruct((b, l, ATTN_WIDTH), F32),
                   jax.ShapeDtypeStruct((b, l, 2 * ATTN_WIDTH), BF16),
                   jax.ShapeDtypeStruct((b, l, 2 * ATTN_WIDTH), BF16),
                   jax.ShapeDtypeStruct((b, l, POOL_WIDTH), BF16),
                   jax.ShapeDtypeStruct((b, HIST_ROWS, POOL_WIDTH), F32)],
        scratch_shapes=[pltpu.VMEM((HIST_ROWS + tm, POOL_WIDTH), F32)],
        compiler_params=_cparams(("arbitrary", "arbitrary")),
        name="proj_seq",
    )(x, hist, cw["nrm"], cw["w_in"], cw["gq"], cw["gk"], cw["hsum"], cw["w_pool"], cw["pool_scale"])


def _proj_step_kernel(x_ref, hist_ref, nrm_ref, win_ref, gq_ref, gk_ref, hsum_ref, wpool_ref, ps_ref,
                      q_ref, k_ref, v_ref, po_ref, tail_ref, *, nseq, nt):
    u, q, k, v = _project(x_ref[...], nrm_ref[...], win_ref[...], gq_ref[...], gk_ref[...], hsum_ref[...])
    q_ref[...] = q.astype(q_ref.dtype)
    k_ref[...] = k
    v_ref[...] = v
    ext = [hist_ref[r] for r in range(POOL_HIST)] + [u[t * nseq:(t + 1) * nseq, :] for t in range(nt)]
    for r in range(POOL_HIST):
        tail_ref[r] = ext[nt + r]
    for g, w in enumerate(POOL_WINDOWS):
        sl = slice(g * POOL_GROUP, (g + 1) * POOL_GROUP)
        ds = []
        for t in range(nt):
            win = ext[POOL_HIST + t][:, sl]
            for i in range(1, w):
                win = win + ext[POOL_HIST + t - i][:, sl]
            ds.append(win * (1.0 / w) - ext[POOL_HIST + t][:, sl])
        d = jnp.concatenate(ds, axis=0)
        y = jnp.dot(d.astype(BF16), wpool_ref[g], preferred_element_type=F32) * ps_ref[:, sl]
        po_ref[:, sl] = y.astype(po_ref.dtype)


def _proj_step(x_tm, hist_tm, cw, *, nseq, nt):
    rows = nseq * nt
    kern = functools.partial(_proj_step_kernel, nseq=nseq, nt=nt)
    return pl.pallas_call(
        kern,
        out_shape=[jax.ShapeDtypeStruct((rows, ATTN_WIDTH), BF16),
                   jax.ShapeDtypeStruct((rows, ATTN_WIDTH), F32),
                   jax.ShapeDtypeStruct((rows, ATTN_WIDTH), F32),
                   jax.ShapeDtypeStruct((rows, POOL_WIDTH), BF16),
                   jax.ShapeDtypeStruct((POOL_HIST, nseq, POOL_WIDTH), F32)],
        compiler_params=pltpu.CompilerParams(vmem_limit_bytes=VMEM_LIMIT),
        name="proj_step",
    )(x_tm, hist_tm, cw["nrm"], cw["w_in"], cw["gq"], cw["gk"], cw["hsum"], cw["w_pool"], cw["pool_scale"])


def _attn_prompt_kernel(q_ref, kx_ref, vx_ref, kxm_ref, vxm_ref, bias_ref, tt_ref, o_ref, carry_ref, acc_ref):
    qi = pl.program_id(1)
    tt = tt_ref[...]
    nq = ATT_BLOCK

    def rows(j):
        return pl.ds(pl.multiple_of(j * ATT_BLOCK, ATT_BLOCK), ATT_BLOCK)

    def all_pairs(k_of, v_of, bias_row, mask, first, last):
        lanes = [slice(p * LANES, (p + 1) * LANES) for p in range(N_PAIRS)]
        wide = [slice(2 * p * LANES, 2 * (p + 1) * LANES) for p in range(N_PAIRS)]
        carries = [jnp.zeros((nq, 2 * ATT_BLOCK), F32) if first else carry_ref[p] for p in range(N_PAIRS)]
        accs = [None if first else acc_ref[:, lanes[p]] for p in range(N_PAIRS)]
        res = _stick_tiles([q_ref[:, l] for l in lanes], [k_of(w) for w in wide], [v_of(w) for w in wide],
                           [bias_ref[bias_row:bias_row + 1, w] for w in wide], tt, carries, mask)
        for p, (out, carry) in enumerate(res):
            acc = out if first else accs[p] + out
            if last:
                o_ref[:, lanes[p]] = acc.astype(o_ref.dtype)
            else:
                carry_ref[p] = carry
                acc_ref[:, lanes[p]] = acc

    r_i = lax.broadcasted_iota(jnp.int32, (nq, 2 * ATT_BLOCK), 0)
    c_i = lax.broadcasted_iota(jnp.int32, (nq, 2 * ATT_BLOCK), 1) & (ATT_BLOCK - 1)
    all_pairs(lambda w: kx_ref[rows(qi), w], lambda w: vx_ref[rows(qi), w], 0, c_i < r_i, True, False)

    def body(t, _):
        j = qi - 1 - t
        all_pairs(lambda w: kx_ref[rows(j), w], lambda w: vx_ref[rows(j), w], 0, None, False, False)
        return 0

    lax.fori_loop(0, qi, body, 0)
    all_pairs(lambda w: kxm_ref[:, w], lambda w: vxm_ref[:, w], 1, None, False, True)


def _attn_prompt(q, kx, vx, kxm, vxm, bias_rows, tt):
    b, l, _ = q.shape
    nq = l // ATT_BLOCK
    once = pl.Buffered(1)
    return pl.pallas_call(
        _attn_prompt_kernel,
        grid=(b, nq),
        in_specs=[pl.BlockSpec((None, ATT_BLOCK, ATTN_WIDTH), lambda i, j: (i, j, 0)),
                  pl.BlockSpec((None, l, 2 * ATTN_WIDTH), lambda i, j: (i, 0, 0), pipeline_mode=once),
                  pl.BlockSpec((None, l, 2 * ATTN_WIDTH), lambda i, j: (i, 0, 0), pipeline_mode=once),
                  pl.BlockSpec((ATT_BLOCK, 2 * ATTN_WIDTH), lambda i, j: (0, 0)),
                  pl.BlockSpec((ATT_BLOCK, 2 * ATTN_WIDTH), lambda i, j: (0, 0)),
                  pl.BlockSpec((2, 2 * ATTN_WIDTH), lambda i, j: (0, 0)),
                  pl.BlockSpec((2 * ATT_BLOCK, 4 * ATT_BLOCK), lambda i, j: (0, 0))],
        out_specs=pl.BlockSpec((None, ATT_BLOCK, ATTN_WIDTH), lambda i, j: (i, j, 0)),
        out_shape=jax.ShapeDtypeStruct((b, l, ATTN_WIDTH), BF16),
        scratch_shapes=[pltpu.VMEM((N_PAIRS, ATT_BLOCK, 2 * ATT_BLOCK), F32),
                        pltpu.VMEM((ATT_BLOCK, ATTN_WIDTH), F32)],
        compiler_params=_cparams(("arbitrary", "arbitrary")),
        name="attn_prompt",
    )(q, kx, vx, kxm, vxm, bias_rows, tt)


def _attn_sample_kernel(pt_ref, qbd_ref, kn_ref, vn_ref, bias_ref, tt_ref, *refs, n_pages, nt):
    del pt_ref
    k_refs = refs[:n_pages]
    v_refs = refs[n_pages:2 * n_pages]
    o_ref = refs[2 * n_pages]
    qbd = qbd_ref[...]
    tt = tt_ref[...]
    bias = bias_ref[...]
    nc = qbd.shape[0]
    nn = (((1,), (0,)), ((), ()))
    nt_dims = (((1,), (1,)), ((), ()))

    pad = jnp.zeros((PAGE - kn_ref.shape[0], ATTN_WIDTH), F32)
    kn = jnp.concatenate([kn_ref[...], pad], axis=0).astype(BF16)
    vn = jnp.concatenate([vn_ref[...], pad], axis=0).astype(BF16)
    t_i = lax.broadcasted_iota(jnp.int32, (nc, PAGE), 0) >> 3
    k_i = lax.broadcasted_iota(jnp.int32, (nc, PAGE), 1)
    z_new = lax.dot_general(qbd, kn, nt_dims, preferred_element_type=F32) + bias
    zs = [jnp.where(k_i < t_i, z_new, NEG)]
    for j in range(n_pages - 1, -1, -1):
        zs.append(lax.dot_general(qbd, k_refs[j][...].astype(BF16), nn, preferred_element_type=F32) + bias)
    gates = [_log2_gates(z) for z in zs]
    sums = [jnp.dot(lk.astype(BF16), tt, preferred_element_type=F32) for _, lk in gates]
    carry = jnp.zeros((nc, PAGE), F32)
    acc = None
    for i, ((lb, _), s) in enumerate(zip(gates, sums)):
        a = jnp.exp2(lb + s[:, :PAGE] + carry).astype(BF16)
        carry = carry + s[:, PAGE:]
        if i == 0:
            acc = jnp.dot(a, vn, preferred_element_type=F32)
        else:
            vt = v_refs[n_pages - i][...].astype(BF16)
            acc = acc + lax.dot_general(a, vt, nt_dims, preferred_element_type=F32)
    h_i = lax.broadcasted_iota(jnp.int32, (nc, ATTN_WIDTH), 0) & (N_HEADS - 1)
    l_i = lax.broadcasted_iota(jnp.int32, (nc, ATTN_WIDTH), 1) >> 6
    acc = jnp.where(h_i == l_i, acc, 0.0)
    for t in range(nt):
        o_ref[t:t + 1, :] = jnp.sum(acc[t * N_HEADS:(t + 1) * N_HEADS, :], axis=0, keepdims=True)


def _attn_sample(page_table, qbd, kn, vn, bias_col, tt, cache_k, cache_v):
    nseq, n_pages = page_table.shape
    nt = qbd.shape[1] // N_HEADS
    pt = page_table.reshape(-1)

    def page_spec(j):
        return pl.BlockSpec((None, ATTN_WIDTH, PAGE), lambda i, pt_ref: (pt_ref[i * n_pages + j], 0, 0))

    grid_spec = pltpu.PrefetchScalarGridSpec(
        num_scalar_prefetch=1,
        grid=(nseq,),
        in_specs=[pl.BlockSpec((None, nt * N_HEADS, ATTN_WIDTH), lambda i, pt_ref: (i, 0, 0)),
                  pl.BlockSpec((None, 8, ATTN_WIDTH), lambda i, pt_ref: (i, 0, 0)),
                  pl.BlockSpec((None, 8, ATTN_WIDTH), lambda i, pt_ref: (i, 0, 0)),
                  pl.BlockSpec((nt * N_HEADS, 1), lambda i, pt_ref: (0, 0)),
                  pl.BlockSpec((PAGE, 2 * PAGE), lambda i, pt_ref: (0, 0))]
                 + [page_spec(j) for j in range(n_pages)] + [page_spec(j) for j in range(n_pages)],
        out_specs=pl.BlockSpec((None, nt, ATTN_WIDTH), lambda i, pt_ref: (i, 0, 0)),
    )
    kern = functools.partial(_attn_sample_kernel, n_pages=n_pages, nt=nt)
    return pl.pallas_call(
        kern,
        grid_spec=grid_spec,
        out_shape=jax.ShapeDtypeStruct((nseq, nt, ATTN_WIDTH), F32),
        compiler_params=_cparams(("arbitrary",)),
        name="attn_sample",
    )(pt, qbd, kn, vn, bias_col, tt, *([cache_k] * n_pages), *([cache_v] * n_pages))


def _merge_router_kernel(xp_ref, xs_ref, pop_ref, pos_ref, atp_ref, ats_ref, wop_ref, woa_ref, nrm_ref, wrh_ref,
                         wrl_ref, tri_ref, x1_ref, h_ref, meta_ref, cout_ref, carry_ref, *, nb_prompt):
    i = pl.program_id(0)

    @pl.when(i == 0)
    def _():
        carry_ref[...] = jnp.zeros_like(carry_ref)

    is_p = i < nb_prompt
    x = jnp.where(is_p, xp_ref[...], xs_ref[...])
    po = jnp.where(is_p, pop_ref[...], pos_ref[...])
    at = jnp.where(is_p, atp_ref[...], ats_ref[...])
    y = (jnp.dot(po, wop_ref[...], preferred_element_type=F32)
         + jnp.dot(at, woa_ref[...], preferred_element_type=F32))
    x1 = x + y
    x1_ref[...] = x1
    h = _rms_rows(x1, nrm_ref[...])
    h_ref[...] = h
    hi = h.astype(BF16)
    lo = (h - hi.astype(F32)).astype(BF16)
    logits = (jnp.dot(hi, wrh_ref[...], preferred_element_type=F32)
              + jnp.dot(lo, wrh_ref[...], preferred_element_type=F32)
              + jnp.dot(hi, wrl_ref[...], preferred_element_type=F32))
    tm = logits.shape[0]
    lane = lax.broadcasted_iota(jnp.int32, (tm, LANES), 1).astype(F32)
    ninf = -jnp.inf
    is_g = lane < N_GROUPS
    gl = jnp.where(is_g, logits, ninf)
    gmax = jnp.max(gl, axis=-1, keepdims=True)
    g_sel = jnp.min(jnp.where(gl == gmax, lane, float(LANES)), axis=-1, keepdims=True)
    psum = jnp.sum(jnp.where(is_g, jnp.exp(logits - gmax), 0.0), axis=-1, keepdims=True)
    g_w = 1.0 / psum
    e_lo = N_GROUPS + g_sel * EXPERTS_PER_GROUP
    in_group = (lane >= e_lo) & (lane < e_lo + EXPERTS_PER_GROUP)
    el = jnp.where(in_group, logits, ninf)
    v1 = jnp.max(el, axis=-1, keepdims=True)
    i1 = jnp.min(jnp.where(el == v1, lane, float(LANES)), axis=-1, keepdims=True)
    el2 = jnp.where(lane == i1, ninf, el)
    v2 = jnp.max(el2, axis=-1, keepdims=True)
    i2 = jnp.min(jnp.where(el2 == v2, lane, float(LANES)), axis=-1, keepdims=True)
    e21 = jnp.exp(v2 - v1)
    den = 1.0 + e21
    gate1 = g_w * (1.0 / den)
    gate2 = g_w * (e21 / den)
    e1 = i1 - N_GROUPS
    e2 = i2 - N_GROUPS
    onehot = jnp.where((lane == e1) | (lane == e2), 1.0, 0.0)
    before = jnp.dot(tri_ref[...], onehot.astype(BF16), preferred_element_type=F32) + carry_ref[...]
    rank1 = jnp.sum(jnp.where(lane == e1, before, 0.0), axis=-1, keepdims=True)
    rank2 = jnp.sum(jnp.where(lane == e2, before, 0.0), axis=-1, keepdims=True)
    carry = carry_ref[...] + jnp.sum(onehot, axis=0, keepdims=True)
    carry_ref[...] = carry
    cout_ref[...] = carry
    meta = jnp.where(lane == 0, e1, 0.0)
    meta = jnp.where(lane == 1, e2, meta)
    meta = jnp.where(lane == 2, rank1, meta)
    meta = jnp.where(lane == 3, rank2, meta)
    meta = jnp.where(lane == 4, gate1, meta)
    meta = jnp.where(lane == 5, gate2, meta)
    meta_ref[...] = meta


def _merge_router(x_p, x_s, po_p, po_s, at_p, at_s, cw):
    tm = MERGE_ROWS
    nbp = x_p.shape[0] // tm
    nbs = x_s.shape[0] // tm
    const = lambda shape: pl.BlockSpec(shape, lambda i: (0,) * len(shape))
    prow = lambda width: pl.BlockSpec((tm, width), lambda i: (jnp.minimum(i, nbp - 1), 0))
    srow = lambda width: pl.BlockSpec((tm, width), lambda i: (jnp.maximum(i - nbp, 0), 0))
    row = lambda width: pl.BlockSpec((tm, width), lambda i: (i, 0))
    t_all = (nbp + nbs) * tm
    return pl.pallas_call(
        functools.partial(_merge_router_kernel, nb_prompt=nbp),
        grid=(nbp + nbs,),
        in_specs=[prow(D_MODEL), srow(D_MODEL), prow(POOL_WIDTH), srow(POOL_WIDTH), prow(ATTN_WIDTH),
                  srow(ATTN_WIDTH), const((POOL_WIDTH, D_MODEL)), const((ATTN_WIDTH, D_MODEL)),
                  const((1, D_MODEL)), const((D_MODEL, LANES)), const((D_MODEL, LANES)), const((tm, tm))],
        out_specs=[row(D_MODEL), row(D_MODEL), row(LANES), const((1, LANES))],
        out_shape=[jax.ShapeDtypeStruct((t_all, D_MODEL), F32),
                   jax.ShapeDtypeStruct((t_all, D_MODEL), F32),
                   jax.ShapeDtypeStruct((t_all, LANES), F32),
                   jax.ShapeDtypeStruct((1, LANES), F32)],
        scratch_shapes=[pltpu.VMEM((1, LANES), F32)],
        compiler_params=_cparams(("arbitrary",)),
        name="merge_router",
    )(x_p, x_s, po_p, po_s, at_p, at_s, cw["w_out_pool"], cw["w_out_attn"], cw["nrm_ffn"], cw["w_router_hi"],
      cw["w_router_lo"], cw["tri_merge"])


def _expert_kernel(bexp_ref, nval_ref, src_ref, nsrc_ref, dst_ref, h_hbm, wg_ref, wu_ref, wd_ref,
                   ys_hbm, xbuf, obuf, wgb, wub, wdb, gsem, ssem):
    i = pl.program_id(0)
    last = pl.num_programs(0) - 1
    slot = i & 1
    other = 1 - slot
    n_cur = pl.multiple_of(nval_ref[i], 8)
    n_next = pl.multiple_of(nval_ref[jnp.minimum(i + 1, last)], 8)
    n_prev = pl.multiple_of(nval_ref[jnp.maximum(i - 1, 0)], 8)
    n_prev2 = pl.multiple_of(nval_ref[jnp.maximum(i - 2, 0)], 8)

    def each_row(n, start_one):
        def body(g, _):
            for k in range(8):
                start_one(g * 8 + k)
            return 0

        lax.fori_loop(0, lax.shift_right_logical(n, 3), body, 0)

    def start_gather(idx_ref, s, n):
        each_row(n, lambda r: pltpu.make_async_copy(h_hbm.at[pl.ds(idx_ref[0, 0, r], 1), :],
                                                    xbuf.at[s, pl.ds(r, 1), :], gsem.at[s]).start())

    def wait_gather(s, n):
        pltpu.make_async_copy(h_hbm.at[pl.ds(0, n), :], xbuf.at[s, pl.ds(0, n), :], gsem.at[s]).wait()

    def wait_scatter(s, n):
        pltpu.make_async_copy(obuf.at[s, pl.ds(0, n), :], ys_hbm.at[pl.ds(0, n), :], ssem.at[s]).wait()

    @pl.when(i == 0)
    def _():
        xbuf[...] = jnp.zeros_like(xbuf)
        spare = pltpu.make_async_copy(xbuf.at[0, pl.ds(0, 8), :], ys_hbm.at[pl.ds(ys_hbm.shape[0] - 8, 8), :],
                                      ssem.at[0])
        spare.start()
        spare.wait()

        @pl.when(n_cur > 0)
        def _():
            start_gather(src_ref, 0, n_cur)

    @pl.when((i < last) & (n_next > 0))
    def _():
        start_gather(nsrc_ref, other, n_next)

    @pl.when((i >= 2) & (n_prev2 > 0))
    def _():
        wait_scatter(slot, n_prev2)

    @pl.when(n_cur > 0)
    def _():
        changed = jnp.logical_or(i == 0, bexp_ref[i] != bexp_ref[jnp.maximum(i - 1, 0)])

        @pl.when(changed)
        def _():
            wgb[...] = wg_ref[...].astype(BF16)
            wub[...] = wu_ref[...].astype(BF16)
            wdb[...] = wd_ref[...].astype(BF16)

        wait_gather(slot, n_cur)
        xb = xbuf[slot].astype(BF16)
        g = jnp.dot(xb, wgb[...], preferred_element_type=F32)
        u = jnp.dot(xb, wub[...], preferred_element_type=F32)
        mid = (g * jax.nn.sigmoid(g)) * u
        obuf[slot] = jnp.dot(mid.astype(BF16), wdb[...], preferred_element_type=F32)
        each_row(n_cur, lambda r: pltpu.make_async_copy(obuf.at[slot, pl.ds(r, 1), :],
                                                        ys_hbm.at[pl.ds(dst_ref[0, 0, r], 1), :],
                                                        ssem.at[slot]).start())

    @pl.when(i == last)
    def _():
        @pl.when((i >= 1) & (n_prev > 0))
        def _():
            wait_scatter(other, n_prev)

        @pl.when(n_cur > 0)
        def _():
            wait_scatter(slot, n_cur)


def _experts(blk_exp, nvalid, row_src, row_dst, h_all, w_gate, w_up, w_down, ys_rows):
    nblk = blk_exp.shape[0]
    rows = EXPERT_ROWS
    idx_spec = pl.BlockSpec((1, 1, rows), lambda i, be, nv: (i, 0, 0), memory_space=pltpu.SMEM)
    nxt_spec = pl.BlockSpec((1, 1, rows), lambda i, be, nv: (jnp.minimum(i + 1, nblk - 1), 0, 0),
                            memory_space=pltpu.SMEM)
    grid_spec = pltpu.PrefetchScalarGridSpec(
        num_scalar_prefetch=2,
        grid=(nblk,),
        in_specs=[idx_spec, nxt_spec, idx_spec,
                  pl.BlockSpec(memory_space=pl.ANY),
                  pl.BlockSpec((None, D_MODEL, D_EXPERT), lambda i, be, nv: (be[i], 0, 0)),
                  pl.BlockSpec((None, D_MODEL, D_EXPERT), lambda i, be, nv: (be[i], 0, 0)),
                  pl.BlockSpec((None, D_EXPERT, D_MODEL), lambda i, be, nv: (be[i], 0, 0))],
        out_specs=pl.BlockSpec(memory_space=pl.ANY),
        scratch_shapes=[pltpu.VMEM((2, rows, D_MODEL), F32), pltpu.VMEM((2, rows, D_MODEL), F32),
                        pltpu.VMEM((D_MODEL, D_EXPERT), BF16), pltpu.VMEM((D_MODEL, D_EXPERT), BF16),
                        pltpu.VMEM((D_EXPERT, D_MODEL), BF16),
                        pltpu.SemaphoreType.DMA((2,)), pltpu.SemaphoreType.DMA((2,))],
    )
    return pl.pallas_call(
        _expert_kernel,
        grid_spec=grid_spec,
        out_shape=jax.ShapeDtypeStruct((ys_rows, D_MODEL), F32),
        compiler_params=_cparams(("arbitrary",)),
        name="experts",
    )(blk_exp, nvalid, row_src, row_src, row_dst, h_all, w_gate, w_up, w_down)


def _combine_kernel(x1_ref, y0_ref, y1_ref, meta_ref, op_ref, os_ref, *, nb_prompt):
    meta = meta_ref[...]
    y = x1_ref[...] + meta[:, 4:5] * y0_ref[...] + meta[:, 5:6] * y1_ref[...]
    i = pl.program_id(0)

    @pl.when(i < nb_prompt)
    def _():
        op_ref[...] = y

    @pl.when(i >= nb_prompt)
    def _():
        os_ref[...] = y


def _combine(x1, ys, meta, t_prompt):
    t_all = x1.shape[0]
    tm = MERGE_ROWS
    nb = t_all // tm
    nbp = t_prompt // tm
    return pl.pallas_call(
        functools.partial(_combine_kernel, nb_prompt=nbp),
        grid=(nb,),
        in_specs=[pl.BlockSpec((tm, D_MODEL), lambda i: (i, 0)),
                  pl.BlockSpec((tm, D_MODEL), lambda i: (i, 0)),
                  pl.BlockSpec((tm, D_MODEL), lambda i: (i + nb, 0)),
                  pl.BlockSpec((tm, LANES), lambda i: (i, 0))],
        out_specs=[pl.BlockSpec((tm, D_MODEL), lambda i: (jnp.minimum(i, nbp - 1), 0)),
                   pl.BlockSpec((tm, D_MODEL), lambda i: (jnp.maximum(i - nbp, 0), 0))],
        out_shape=[jax.ShapeDtypeStruct((t_prompt, D_MODEL), F32),
                   jax.ShapeDtypeStruct((t_all - t_prompt, D_MODEL), F32)],
        compiler_params=_cparams(("arbitrary",)),
        name="combine",
    )(x1, ys, ys, meta)


def _prep_weights(norm_mix, w_in, g_q, g_k, w_pool, pool_scale, w_out, norm_ffn, w_group, w_expert):
    lane = jnp.arange(ATTN_WIDTH)
    hsum = (lane[:, None] // HEAD_DIM == lane[None, :] // HEAD_DIM).astype(BF16)
    w_router = jnp.zeros((D_MODEL, LANES), F32)
    w_router = w_router.at[:, 0:N_GROUPS].set(w_group).at[:, N_GROUPS:N_GROUPS + N_EXPERTS].set(w_expert)
    w_router_hi = w_router.astype(BF16)
    w_router_lo = (w_router - w_router_hi.astype(F32)).astype(BF16)
    r = jnp.arange(MERGE_ROWS)
    return {
        "nrm": norm_mix.reshape(1, D_MODEL),
        "w_in": w_in.astype(BF16),
        "gq": (jnp.tile(g_q, N_HEADS) * (HEAD_DIM ** -0.5 * LOG2E)).reshape(1, ATTN_WIDTH),
        "gk": jnp.tile(g_k, N_HEADS).reshape(1, ATTN_WIDTH),
        "hsum": hsum,
        "w_pool": w_pool.astype(BF16),
        "pool_scale": pool_scale.reshape(1, POOL_WIDTH),
        "w_out_pool": w_out[:POOL_WIDTH].astype(BF16),
        "w_out_attn": w_out[POOL_WIDTH:].astype(BF16),
        "nrm_ffn": norm_ffn.reshape(1, D_MODEL),
        "w_router_hi": w_router_hi,
        "w_router_lo": w_router_lo,
        "tri_merge": (r[None, :] < r[:, None]).astype(BF16),
    }


def _later_sums(n_keys, n_heads):
    idx = jnp.arange(n_heads * n_keys)
    same = idx[:, None] // n_keys == idx[None, :] // n_keys
    later = same & (idx[:, None] > idx[None, :])
    return jnp.concatenate([later, same], axis=1).astype(BF16)


def kernel(x_prompt, x_sample, cache_k, cache_v, state_pool, page_table, meta_tokens, norm_mix, w_in, g_q, g_k,
           sb_bias, w_pool, pool_scale, w_out, norm_ffn, w_group, w_expert, w_gate, w_up, w_down):
    nb, seq, _ = x_prompt.shape
    nseq, nt, _ = x_sample.shape
    n_phys = cache_k.shape[1]
    t_prompt = nb * seq
    t_sample = nseq * nt
    t_all = t_prompt + t_sample

    cw = _prep_weights(norm_mix[0], w_in[0], g_q[0], g_k[0], w_pool[0], pool_scale[0], w_out[0], norm_ffn[0],
                       w_group[0], w_expert[0])
    bias = sb_bias[0].astype(F32) * LOG2E

    zero_hist = jnp.zeros((HIST_ROWS, POOL_WIDTH), F32)
    _, k_m, v_m, kx_m, vx_m, _, tail_m = _proj_seq(meta_tokens[None], zero_hist, cw, tm=N_META, p0=0,
                                                   full_count=False)
    pad_rows = ((0, ATT_BLOCK - N_META), (0, 0))
    kxm = jnp.pad(kx_m[0], pad_rows)
    vxm = jnp.pad(vx_m[0], pad_rows)

    q_p, k_p, v_p, kx_p, vx_p, po_p, tail_p = _proj_seq(x_prompt, tail_m[0], cw, tm=PROJ_ROWS, p0=N_META,
                                                        full_count=True)
    head_bias = jnp.repeat(bias, ATT_BLOCK)
    col = jnp.arange(N_HEADS * ATT_BLOCK) % ATT_BLOCK
    bias_rows = jnp.stack([head_bias, jnp.where(col < N_META, head_bias, NEG)])
    attn_p = _attn_prompt(q_p, kx_p, vx_p, kxm, vxm, bias_rows, _later_sums(ATT_BLOCK, 2))

    x_s_tm = jnp.transpose(x_sample, (1, 0, 2)).reshape(t_sample, D_MODEL)
    hist_tm = jnp.transpose(state_pool[0], (1, 0, 2))
    q_s, k_s, v_s, po_s, tail_s = _proj_step(x_s_tm, hist_tm, cw, nseq=nseq, nt=nt)
    to_seq = lambda a: jnp.transpose(a.reshape(nt, nseq, a.shape[-1]), (1, 0, 2))
    q_sb, k_sb, v_sb, po_sb = to_seq(q_s), to_seq(k_s), to_seq(v_s), to_seq(po_s)
    head_of_lane = jnp.arange(ATTN_WIDTH) // HEAD_DIM
    head_mask = head_of_lane[None, :] == jnp.arange(N_HEADS)[:, None]
    qbd = jnp.where(head_mask[None, None], q_sb[:, :, None, :], 0).reshape(nseq, nt * N_HEADS, ATTN_WIDTH)
    pad_new = ((0, 0), (0, 8 - nt), (0, 0))
    kn = jnp.pad(k_sb, pad_new)
    vn = jnp.pad(v_sb, pad_new)
    bias_col = jnp.tile(bias, nt).reshape(nt * N_HEADS, 1)
    page_t = lambda c: jnp.transpose(c[0], (0, 2, 3, 1)).reshape(n_phys, ATTN_WIDTH, PAGE)
    attn_s = _attn_sample(page_table, qbd, kn, vn, bias_col, _later_sums(PAGE, 1), page_t(cache_k), page_t(cache_v))

    x1, h_all, meta, counts_f = _merge_router(
        x_prompt.reshape(t_prompt, D_MODEL), x_sample.reshape(t_sample, D_MODEL),
        po_p.reshape(t_prompt, POOL_WIDTH), po_sb.reshape(t_sample, POOL_WIDTH),
        attn_p.reshape(t_prompt, ATTN_WIDTH), attn_s.reshape(t_sample, ATTN_WIDTH).astype(BF16), cw)

    pair_exp = meta[:, 0:2].astype(jnp.int32)
    pair_rank = meta[:, 2:4].astype(jnp.int32)
    counts = counts_f[0, :N_EXPERTS].astype(jnp.int32)
    rows = EXPERT_ROWS
    padded = (counts + rows - 1) // rows * rows
    pad_end = jnp.cumsum(padded)
    pad_start = pad_end - padded
    nblk = -(-(2 * t_all) // rows) + N_EXPERTS
    dest = (pad_start[pair_exp] + pair_rank).reshape(-1)
    pair_id = jnp.arange(2 * t_all, dtype=jnp.int32)
    row_pair = jnp.full((nblk * rows,), -1, jnp.int32).at[dest].set(pair_id)
    row_src = jnp.maximum(row_pair, 0) >> 1
    spare = 2 * t_all + (jnp.arange(nblk * rows, dtype=jnp.int32) & 7)
    row_dst = jnp.where(row_pair >= 0, (row_pair & 1) * t_all + (row_pair >> 1), spare)
    blk_row0 = jnp.arange(nblk, dtype=jnp.int32) * rows
    blk_exp = jnp.minimum(jnp.sum(blk_row0[:, None] >= pad_end[None, :], axis=1), N_EXPERTS - 1).astype(jnp.int32)
    nvalid = jnp.clip(counts[blk_exp] - (blk_row0 - pad_start[blk_exp]), 0, rows)
    nvalid = jnp.where(blk_row0 < pad_end[-1], (nvalid + 7) // 8 * 8, 0).astype(jnp.int32)

    ys = _experts(blk_exp, nvalid, row_src.reshape(nblk, 1, rows), row_dst.reshape(nblk, 1, rows), h_all,
                  w_gate[0], w_up[0], w_down[0], 2 * t_all + 8)

    y_prompt, y_sample = _combine(x1, ys, meta, t_prompt)
    y_prompt = y_prompt.reshape(nb, seq, D_MODEL)
    y_sample = y_sample.reshape(nseq, nt, D_MODEL)

    heads = lambda a: a.reshape(a.shape[:-1] + (N_HEADS, HEAD_DIM))
    k_prompt = jnp.concatenate([jnp.broadcast_to(k_m, (nb, N_META, ATTN_WIDTH)), k_p], axis=1)
    v_prompt = jnp.concatenate([jnp.broadcast_to(v_m, (nb, N_META, ATTN_WIDTH)), v_p], axis=1)
    pool_sample = jnp.transpose(tail_s, (1, 0, 2))
    return (y_prompt, y_sample, heads(k_prompt)[None], heads(v_prompt)[None], tail_p[:, 1:][None],
            heads(k_sb)[None], heads(v_sb)[None], pool_sample[None])
```

```python
import functools

import jax
import jax.numpy as jnp
from jax import lax
from jax.experimental import pallas as pl
from jax.experimental.pallas import tpu as pltpu

F32 = jnp.float32
BF16 = jnp.bfloat16

D_MODEL = 1024
POOL_WIDTH = 512
ATTN_WIDTH = 512
HEAD_DIM = 64
N_HEADS = 8
N_PAIRS = N_HEADS // 2
POOL_WINDOWS = (2, 4, 8, 16)
POOL_GROUP = 128
POOL_HIST = 15
HIST_ROWS = 16
N_META = 16
PAGE = 128
N_GROUPS = 4
EXPERTS_PER_GROUP = 8
N_EXPERTS = 32
D_EXPERT = 512
EPS = 1e-6
NEG = -1e30
LOG2E = 1.4426950408889634
LANES = 128
ATT_BLOCK = 128
PROJ_ROWS = 512
MERGE_ROWS = 256
EXPERT_ROWS = 256
VMEM_LIMIT = 56 * 1024 * 1024


def _cparams(sem, vmem=VMEM_LIMIT):
    return pltpu.CompilerParams(dimension_semantics=sem, vmem_limit_bytes=vmem)


def _rms_rows(x, g):
    ms = jnp.mean(x * x, axis=-1, keepdims=True)
    return (x * lax.rsqrt(ms + EPS)) * g


def _head_norm(a, g, hsum):
    sq = a * a
    hi = sq.astype(BF16)
    lo = (sq - hi.astype(F32)).astype(BF16)
    ssum = (jnp.dot(hi, hsum, preferred_element_type=F32)
            + jnp.dot(lo, hsum, preferred_element_type=F32))
    return (a * lax.rsqrt(ssum * (1.0 / HEAD_DIM) + EPS)) * g


def _project(x, nrm, w_in, gq, gk, hsum):
    n = _rms_rows(x, nrm)
    proj = jnp.dot(n.astype(BF16), w_in, preferred_element_type=F32)
    u = proj[:, 0:POOL_WIDTH]
    q = _head_norm(proj[:, POOL_WIDTH:POOL_WIDTH + ATTN_WIDTH], gq, hsum)
    k = _head_norm(proj[:, POOL_WIDTH + ATTN_WIDTH:POOL_WIDTH + 2 * ATTN_WIDTH], gk, hsum)
    v = proj[:, POOL_WIDTH + 2 * ATTN_WIDTH:]
    return u, q, k, v


def _split_heads(a, out_ref):
    rows = a.shape[0]
    lane = lax.broadcasted_iota(jnp.int32, (rows, LANES), 1)
    first = lane < HEAD_DIM
    for p in range(N_PAIRS):
        a2 = a[:, p * LANES:(p + 1) * LANES]
        out_ref[:, 2 * p * LANES:(2 * p + 1) * LANES] = jnp.where(first, a2, 0.0).astype(out_ref.dtype)
        out_ref[:, (2 * p + 1) * LANES:(2 * p + 2) * LANES] = jnp.where(first, 0.0, a2).astype(out_ref.dtype)


def _log2_gates(z):
    soft = jnp.log(1.0 + jnp.exp2(-jnp.abs(z))) * LOG2E
    log_beta = jnp.minimum(z, 0.0) - soft
    return log_beta, log_beta - z


def _stick_step(tt, gate_in, mix_in):
    def stack_heads(b):
        half = b.shape[1] // 2
        return jnp.concatenate([b[:, :half], b[:, half:]], axis=0)

    nt_dims = (((1,), (1,)), ((), ()))
    gates = mixed = None
    if mix_in is not None:
        log_betas, log_keeps, carries, vxbs = mix_in
        sums = [jnp.dot(lk, tt, preferred_element_type=F32) for lk in log_keeps]
    if gate_in is not None:
        q2s, kxbs, brows, mask = gate_in
        zs = [lax.dot_general(q2, stack_heads(kxb), nt_dims, preferred_element_type=F32) + brow
              for q2, kxb, brow in zip(q2s, kxbs, brows)]
        if mask is not None:
            zs = [jnp.where(mask, z, NEG) for z in zs]
    if mix_in is not None:
        nk = sums[0].shape[1] // 2
        probs = [jnp.exp2(lb + s[:, :nk] + carry).astype(BF16) for lb, s, carry in zip(log_betas, sums, carries)]
    if gate_in is not None:
        gates = [(lb, lk.astype(BF16)) for lb, lk in (_log2_gates(z) for z in zs)]
    if mix_in is not None:
        outs = [jnp.dot(a, stack_heads(vxb), preferred_element_type=F32) for a, vxb in zip(probs, vxbs)]
        mixed = [(out, carry + s[:, nk:]) for out, carry, s in zip(outs, carries, sums)]
    return gates, mixed


def _proj_seq_kernel(x_ref, hist_ref, nrm_ref, win_ref, gq_ref, gk_ref, hsum_ref, wpool_ref, ps_ref,
                     q_ref, k_ref, v_ref, kx_ref, vx_ref, po_ref, tail_ref, ext_ref, *, tm, p0, full_count):
    blk = pl.program_id(1)
    u, q, k, v = _project(x_ref[...], nrm_ref[...], win_ref[...], gq_ref[...], gk_ref[...], hsum_ref[...])
    q_ref[...] = q.astype(q_ref.dtype)
    k_ref[...] = k
    v_ref[...] = v
    _split_heads(k, kx_ref)
    _split_heads(v, vx_ref)

    @pl.when(blk == 0)
    def _():
        ext_ref[0:HIST_ROWS, :] = hist_ref[...]

    ext_ref[HIST_ROWS:HIST_ROWS + tm, :] = u
    for g, w in enumerate(POOL_WINDOWS):
        e = ext_ref[:, g * POOL_GROUP:(g + 1) * POOL_GROUP]
        s = e
        sh = 1
        while sh < w:
            s = s + pltpu.roll(s, sh, axis=0)
            sh *= 2
        win = s[HIST_ROWS:, :]
        tok = e[HIST_ROWS:, :]
        if full_count:
            d = win * (1.0 / w) - tok
        else:
            pos = p0 + lax.broadcasted_iota(jnp.int32, (tm, 1), 0)
            cnt = jnp.minimum(w, pos + 1).astype(F32)
            d = win / cnt - tok
        y = jnp.dot(d.astype(BF16), wpool_ref[g], preferred_element_type=F32)
        y = y * ps_ref[:, g * POOL_GROUP:(g + 1) * POOL_GROUP]
        po_ref[:, g * POOL_GROUP:(g + 1) * POOL_GROUP] = y.astype(po_ref.dtype)
    last = ext_ref[tm:tm + HIST_ROWS, :]
    ext_ref[0:HIST_ROWS, :] = last

    @pl.when(blk == pl.num_programs(1) - 1)
    def _():
        tail_ref[...] = last


def _proj_seq(x, hist, cw, *, tm, p0, full_count):
    b, l, _ = x.shape
    nb = l // tm
    const = lambda shape: pl.BlockSpec(shape, lambda i, j: (0,) * len(shape))
    row = lambda width: pl.BlockSpec((None, tm, width), lambda i, j: (i, j, 0))
    kern = functools.partial(_proj_seq_kernel, tm=tm, p0=p0, full_count=full_count)
    return pl.pallas_call(
        kern,
        grid=(b, nb),
        in_specs=[row(D_MODEL), const((HIST_ROWS, POOL_WIDTH)), const((1, D_MODEL)),
                  const((D_MODEL, 2 * D_MODEL)), const((1, ATTN_WIDTH)), const((1, ATTN_WIDTH)),
                  const((ATTN_WIDTH, ATTN_WIDTH)), const((4, POOL_GROUP, POOL_GROUP)), const((1, POOL_WIDTH))],
        out_specs=[row(ATTN_WIDTH), row(ATTN_WIDTH), row(ATTN_WIDTH), row(2 * ATTN_WIDTH), row(2 * ATTN_WIDTH),
                   row(POOL_WIDTH), pl.BlockSpec((None, HIST_ROWS, POOL_WIDTH), lambda i, j: (i, 0, 0))],
        out_shape=[jax.ShapeDtypeStruct((b, l, ATTN_WIDTH), BF16),
                   jax.ShapeDtypeStruct((b, l, ATTN_WIDTH), F32),
                   jax.ShapeDtypeStruct((b, l, ATTN_WIDTH), F32),
                   jax.ShapeDtypeStruct((b, l, 2 * ATTN_WIDTH), BF16),
                   jax.ShapeDtypeStruct((b, l, 2 * ATTN_WIDTH), BF16),
                   jax.ShapeDtypeStruct((b, l, POOL_WIDTH), BF16),
                   jax.ShapeDtypeStruct((b, HIST_ROWS, POOL_WIDTH), F32)],
        scratch_shapes=[pltpu.VMEM((HIST_ROWS + tm, POOL_WIDTH), F32)],
        compiler_params=_cparams(("arbitrary", "arbitrary")),
        name="proj_seq",
    )(x, hist, cw["nrm"], cw["w_in"], cw["gq"], cw["gk"], cw["hsum"], cw["w_pool"], cw["pool_scale"])


def _proj_step_kernel(x_ref, hist_ref, nrm_ref, win_ref, gq_ref, gk_ref, hsum_ref, wpool_ref, ps_ref,
                      q_ref, k_ref, v_ref, po_ref, tail_ref, *, nseq, nt):
    u, q, k, v = _project(x_ref[...], nrm_ref[...], win_ref[...], gq_ref[...], gk_ref[...], hsum_ref[...])
    q_ref[...] = q.astype(q_ref.dtype)
    k_ref[...] = k
    v_ref[...] = v
    ext = [hist_ref[r] for r in range(POOL_HIST)] + [u[t * nseq:(t + 1) * nseq, :] for t in range(nt)]
    for r in range(POOL_HIST):
        tail_ref[r] = ext[nt + r]
    for g, w in enumerate(POOL_WINDOWS):
        sl = slice(g * POOL_GROUP, (g + 1) * POOL_GROUP)
        ds = []
        for t in range(nt):
            win = ext[POOL_HIST + t][:, sl]
            for i in range(1, w):
                win = win + ext[POOL_HIST + t - i][:, sl]
            ds.append(win * (1.0 / w) - ext[POOL_HIST + t][:, sl])
        d = jnp.concatenate(ds, axis=0)
        y = jnp.dot(d.astype(BF16), wpool_ref[g], preferred_element_type=F32) * ps_ref[:, sl]
        po_ref[:, sl] = y.astype(po_ref.dtype)


def _proj_step(x_tm, hist_tm, cw, *, nseq, nt):
    rows = nseq * nt
    kern = functools.partial(_proj_step_kernel, nseq=nseq, nt=nt)
    return pl.pallas_call(
        kern,
        out_shape=[jax.ShapeDtypeStruct((rows, ATTN_WIDTH), BF16),
                   jax.ShapeDtypeStruct((rows, ATTN_WIDTH), F32),
                   jax.ShapeDtypeStruct((rows, ATTN_WIDTH), F32),
                   jax.ShapeDtypeStruct((rows, POOL_WIDTH), BF16),
                   jax.ShapeDtypeStruct((POOL_HIST, nseq, POOL_WIDTH), F32)],
        compiler_params=pltpu.CompilerParams(vmem_limit_bytes=VMEM_LIMIT),
        name="proj_step",
    )(x_tm, hist_tm, cw["nrm"], cw["w_in"], cw["gq"], cw["gk"], cw["hsum"], cw["w_pool"], cw["pool_scale"])


def _attn_prompt_kernel(q_ref, kx_ref, vx_ref, kxm_ref, vxm_ref, bias_ref, tt_ref, o_ref,
                        carry_ref, acc_ref, lb_ref, lk_ref):
    qi = pl.program_id(1)
    tt = tt_ref[...]
    nq = ATT_BLOCK
    pairs = range(N_PAIRS)
    lanes = [slice(p * LANES, (p + 1) * LANES) for p in pairs]
    wide = [slice(2 * p * LANES, 2 * (p + 1) * LANES) for p in pairs]

    def rows(j):
        return pl.ds(pl.multiple_of(j * ATT_BLOCK, ATT_BLOCK), ATT_BLOCK)

    def gate_in(k_of, bias_row, mask):
        return ([q_ref[:, l] for l in lanes], [k_of(w) for w in wide],
                [bias_ref[bias_row:bias_row + 1, w] for w in wide], mask)

    def mix_in(v_of):
        return ([lb_ref[p] for p in pairs], [lk_ref[p] for p in pairs], [carry_ref[p] for p in pairs],
                [v_of(w) for w in wide])

    def step(gin, min_, last=False):
        accs = None if min_ is None else [acc_ref[:, l] for l in lanes]
        gates, mixed = _stick_step(tt, gin, min_)
        if mixed is not None:
            for p, (out, carry) in enumerate(mixed):
                if last:
                    o_ref[:, lanes[p]] = (accs[p] + out).astype(o_ref.dtype)
                else:
                    carry_ref[p] = carry
                    acc_ref[:, lanes[p]] = accs[p] + out
        if gates is not None:
            for p, (lb, lk) in enumerate(gates):
                lb_ref[p] = lb
                lk_ref[p] = lk

    carry_ref[...] = jnp.zeros_like(carry_ref)
    acc_ref[...] = jnp.zeros_like(acc_ref)
    r_i = lax.broadcasted_iota(jnp.int32, (nq, 2 * ATT_BLOCK), 0)
    c_i = lax.broadcasted_iota(jnp.int32, (nq, 2 * ATT_BLOCK), 1) & (ATT_BLOCK - 1)
    step(gate_in(lambda w: kx_ref[rows(qi), w], 0, c_i < r_i), None)

    def body(t, _):
        j = qi - 1 - t
        step(gate_in(lambda w: kx_ref[rows(j), w], 0, None), mix_in(lambda w: vx_ref[rows(j + 1), w]))
        return 0

    lax.fori_loop(0, qi, body, 0)
    step(gate_in(lambda w: kxm_ref[:, w], 1, None), mix_in(lambda w: vx_ref[0:ATT_BLOCK, w]))
    step(None, mix_in(lambda w: vxm_ref[:, w]), last=True)


def _attn_prompt(q, kx, vx, kxm, vxm, bias_rows, tt):
    b, l, _ = q.shape
    nq = l // ATT_BLOCK
    once = pl.Buffered(1)
    return pl.pallas_call(
        _attn_prompt_kernel,
        grid=(b, nq),
        in_specs=[pl.BlockSpec((None, ATT_BLOCK, ATTN_WIDTH), lambda i, j: (i, j, 0)),
                  pl.BlockSpec((None, l, 2 * ATTN_WIDTH), lambda i, j: (i, 0, 0), pipeline_mode=once),
                  pl.BlockSpec((None, l, 2 * ATTN_WIDTH), lambda i, j: (i, 0, 0), pipeline_mode=once),
                  pl.BlockSpec((ATT_BLOCK, 2 * ATTN_WIDTH), lambda i, j: (0, 0)),
                  pl.BlockSpec((ATT_BLOCK, 2 * ATTN_WIDTH), lambda i, j: (0, 0)),
                  pl.BlockSpec((2, 2 * ATTN_WIDTH), lambda i, j: (0, 0)),
                  pl.BlockSpec((2 * ATT_BLOCK, 4 * ATT_BLOCK), lambda i, j: (0, 0))],
        out_specs=pl.BlockSpec((None, ATT_BLOCK, ATTN_WIDTH), lambda i, j: (i, j, 0)),
        out_shape=jax.ShapeDtypeStruct((b, l, ATTN_WIDTH), BF16),
        scratch_shapes=[pltpu.VMEM((N_PAIRS, ATT_BLOCK, 2 * ATT_BLOCK), F32),
                        pltpu.VMEM((ATT_BLOCK, ATTN_WIDTH), F32),
                        pltpu.VMEM((N_PAIRS, ATT_BLOCK, 2 * ATT_BLOCK), F32),
                        pltpu.VMEM((N_PAIRS, ATT_BLOCK, 2 * ATT_BLOCK), BF16)],
        compiler_params=_cparams(("arbitrary", "arbitrary")),
        name="attn_prompt",
    )(q, kx, vx, kxm, vxm, bias_rows, tt)


def _attn_sample_kernel(pt_ref, qbd_ref, kn_ref, vn_ref, bias_ref, tt_ref, *refs, n_pages, nt):
    del pt_ref
    k_refs = refs[:n_pages]
    v_refs = refs[n_pages:2 * n_pages]
    o_ref = refs[2 * n_pages]
    qbd = qbd_ref[...]
    tt = tt_ref[...]
    bias = bias_ref[...]
    nc = qbd.shape[0]
    nn = (((1,), (0,)), ((), ()))
    nt_dims = (((1,), (1,)), ((), ()))

    pad = jnp.zeros((PAGE - kn_ref.shape[0], ATTN_WIDTH), F32)
    kn = jnp.concatenate([kn_ref[...], pad], axis=0).astype(BF16)
    vn = jnp.concatenate([vn_ref[...], pad], axis=0).astype(BF16)
    t_i = lax.broadcasted_iota(jnp.int32, (nc, PAGE), 0) >> 3
    k_i = lax.broadcasted_iota(jnp.int32, (nc, PAGE), 1)
    z_new = lax.dot_general(qbd, kn, nt_dims, preferred_element_type=F32) + bias
    zs = [jnp.where(k_i < t_i, z_new, NEG)]
    for j in range(n_pages - 1, -1, -1):
        zs.append(lax.dot_general(qbd, k_refs[j][...].astype(BF16), nn, preferred_element_type=F32) + bias)
    gates = [_log2_gates(z) for z in zs]
    sums = [jnp.dot(lk.astype(BF16), tt, preferred_element_type=F32) for _, lk in gates]
    carry = jnp.zeros((nc, PAGE), F32)
    acc = None
    for i, ((lb, _), s) in enumerate(zip(gates, sums)):
        a = jnp.exp2(lb + s[:, :PAGE] + carry).astype(BF16)
        carry = carry + s[:, PAGE:]
        if i == 0:
            acc = jnp.dot(a, vn, preferred_element_type=F32)
        else:
            vt = v_refs[n_pages - i][...].astype(BF16)
            acc = acc + lax.dot_general(a, vt, nt_dims, preferred_element_type=F32)
    h_i = lax.broadcasted_iota(jnp.int32, (nc, ATTN_WIDTH), 0) & (N_HEADS - 1)
    l_i = lax.broadcasted_iota(jnp.int32, (nc, ATTN_WIDTH), 1) >> 6
    acc = jnp.where(h_i == l_i, acc, 0.0)
    for t in range(nt):
        o_ref[t:t + 1, :] = jnp.sum(acc[t * N_HEADS:(t + 1) * N_HEADS, :], axis=0, keepdims=True)


def _attn_sample(page_table, qbd, kn, vn, bias_col, tt, cache_k, cache_v):
    nseq, n_pages = page_table.shape
    nt = qbd.shape[1] // N_HEADS
    pt = page_table.reshape(-1)

    def page_spec(j):
        return pl.BlockSpec((None, ATTN_WIDTH, PAGE), lambda i, pt_ref: (pt_ref[i * n_pages + j], 0, 0))

    grid_spec = pltpu.PrefetchScalarGridSpec(
        num_scalar_prefetch=1,
        grid=(nseq,),
        in_specs=[pl.BlockSpec((None, nt * N_HEADS, ATTN_WIDTH), lambda i, pt_ref: (i, 0, 0)),
                  pl.BlockSpec((None, 8, ATTN_WIDTH), lambda i, pt_ref: (i, 0, 0)),
                  pl.BlockSpec((None, 8, ATTN_WIDTH), lambda i, pt_ref: (i, 0, 0)),
                  pl.BlockSpec((nt * N_HEADS, 1), lambda i, pt_ref: (0, 0)),
                  pl.BlockSpec((PAGE, 2 * PAGE), lambda i, pt_ref: (0, 0))]
                 + [page_spec(j) for j in range(n_pages)] + [page_spec(j) for j in range(n_pages)],
        out_specs=pl.BlockSpec((None, nt, ATTN_WIDTH), lambda i, pt_ref: (i, 0, 0)),
    )
    kern = functools.partial(_attn_sample_kernel, n_pages=n_pages, nt=nt)
    return pl.pallas_call(
        kern,
        grid_spec=grid_spec,
        out_shape=jax.ShapeDtypeStruct((nseq, nt, ATTN_WIDTH), F32),
        compiler_params=_cparams(("arbitrary",)),
        name="attn_sample",
    )(pt, qbd, kn, vn, bias_col, tt, *([cache_k] * n_pages), *([cache_v] * n_pages))


def _merge_router_kernel(xp_ref, xs_ref, pop_ref, pos_ref, atp_ref, ats_ref, wop_ref, woa_ref, nrm_ref, wrh_ref,
                         wrl_ref, tri_ref, x1_ref, h_ref, meta_ref, cout_ref, carry_ref, *, nb_prompt):
    i = pl.program_id(0)

    @pl.when(i == 0)
    def _():
        carry_ref[...] = jnp.zeros_like(carry_ref)

    is_p = i < nb_prompt
    x = jnp.where(is_p, xp_ref[...], xs_ref[...])
    po = jnp.where(is_p, pop_ref[...], pos_ref[...])
    at = jnp.where(is_p, atp_ref[...], ats_ref[...])
    y = (jnp.dot(po, wop_ref[...], preferred_element_type=F32)
         + jnp.dot(at, woa_ref[...], preferred_element_type=F32))
    x1 = x + y
    x1_ref[...] = x1
    h = _rms_rows(x1, nrm_ref[...])
    h_ref[...] = h
    hi = h.astype(BF16)
    lo = (h - hi.astype(F32)).astype(BF16)
    logits = (jnp.dot(hi, wrh_ref[...], preferred_element_type=F32)
              + jnp.dot(lo, wrh_ref[...], preferred_element_type=F32)
              + jnp.dot(hi, wrl_ref[...], preferred_element_type=F32))
    tm = logits.shape[0]
    lane = lax.broadcasted_iota(jnp.int32, (tm, LANES), 1).astype(F32)
    ninf = -jnp.inf
    is_g = lane < N_GROUPS
    gl = jnp.where(is_g, logits, ninf)
    gmax = jnp.max(gl, axis=-1, keepdims=True)
    g_sel = jnp.min(jnp.where(gl == gmax, lane, float(LANES)), axis=-1, keepdims=True)
    psum = jnp.sum(jnp.where(is_g, jnp.exp(logits - gmax), 0.0), axis=-1, keepdims=True)
    g_w = 1.0 / psum
    e_lo = N_GROUPS + g_sel * EXPERTS_PER_GROUP
    in_group = (lane >= e_lo) & (lane < e_lo + EXPERTS_PER_GROUP)
    el = jnp.where(in_group, logits, ninf)
    v1 = jnp.max(el, axis=-1, keepdims=True)
    i1 = jnp.min(jnp.where(el == v1, lane, float(LANES)), axis=-1, keepdims=True)
    el2 = jnp.where(lane == i1, ninf, el)
    v2 = jnp.max(el2, axis=-1, keepdims=True)
    i2 = jnp.min(jnp.where(el2 == v2, lane, float(LANES)), axis=-1, keepdims=True)
    e21 = jnp.exp(v2 - v1)
    den = 1.0 + e21
    gate1 = g_w * (1.0 / den)
    gate2 = g_w * (e21 / den)
    e1 = i1 - N_GROUPS
    e2 = i2 - N_GROUPS
    onehot = jnp.where((lane == e1) | (lane == e2), 1.0, 0.0)
    before = jnp.dot(tri_ref[...], onehot.astype(BF16), preferred_element_type=F32) + carry_ref[...]
    rank1 = jnp.sum(jnp.where(lane == e1, before, 0.0), axis=-1, keepdims=True)
    rank2 = jnp.sum(jnp.where(lane == e2, before, 0.0), axis=-1, keepdims=True)
    carry = carry_ref[...] + jnp.sum(onehot, axis=0, keepdims=True)
    carry_ref[...] = carry
    cout_ref[...] = carry
    meta = jnp.where(lane == 0, e1, 0.0)
    meta = jnp.where(lane == 1, e2, meta)
    meta = jnp.where(lane == 2, rank1, meta)
    meta = jnp.where(lane == 3, rank2, meta)
    meta = jnp.where(lane == 4, gate1, meta)
    meta = jnp.where(lane == 5, gate2, meta)
    meta_ref[...] = meta


def _merge_router(x_p, x_s, po_p, po_s, at_p, at_s, cw):
    tm = MERGE_ROWS
    nbp = x_p.shape[0] // tm
    nbs = x_s.shape[0] // tm
    const = lambda shape: pl.BlockSpec(shape, lambda i: (0,) * len(shape))
    prow = lambda width: pl.BlockSpec((tm, width), lambda i: (jnp.minimum(i, nbp - 1), 0))
    srow = lambda width: pl.BlockSpec((tm, width), lambda i: (jnp.maximum(i - nbp, 0), 0))
    row = lambda width: pl.BlockSpec((tm, width), lambda i: (i, 0))
    t_all = (nbp + nbs) * tm
    return pl.pallas_call(
        functools.partial(_merge_router_kernel, nb_prompt=nbp),
        grid=(nbp + nbs,),
        in_specs=[prow(D_MODEL), srow(D_MODEL), prow(POOL_WIDTH), srow(POOL_WIDTH), prow(ATTN_WIDTH),
                  srow(ATTN_WIDTH), const((POOL_WIDTH, D_MODEL)), const((ATTN_WIDTH, D_MODEL)),
                  const((1, D_MODEL)), const((D_MODEL, LANES)), const((D_MODEL, LANES)), const((tm, tm))],
        out_specs=[row(D_MODEL), row(D_MODEL), row(LANES), const((1, LANES))],
        out_shape=[jax.ShapeDtypeStruct((t_all, D_MODEL), F32),
                   jax.ShapeDtypeStruct((t_all, D_MODEL), F32),
                   jax.ShapeDtypeStruct((t_all, LANES), F32),
                   jax.ShapeDtypeStruct((1, LANES), F32)],
        scratch_shapes=[pltpu.VMEM((1, LANES), F32)],
        compiler_params=_cparams(("arbitrary",)),
        name="merge_router",
    )(x_p, x_s, po_p, po_s, at_p, at_s, cw["w_out_pool"], cw["w_out_attn"], cw["nrm_ffn"], cw["w_router_hi"],
      cw["w_router_lo"], cw["tri_merge"])


def _expert_kernel(bexp_ref, nval_ref, src_ref, nsrc_ref, dst_ref, h_hbm, wg_ref, wu_ref, wd_ref,
                   ys_hbm, xbuf, obuf, wgb, wub, wdb, gsem, ssem):
    i = pl.program_id(0)
    last = pl.num_programs(0) - 1
    slot = i & 1
    other = 1 - slot
    n_cur = pl.multiple_of(nval_ref[i], 8)
    n_next = pl.multiple_of(nval_ref[jnp.minimum(i + 1, last)], 8)
    n_prev = pl.multiple_of(nval_ref[jnp.maximum(i - 1, 0)], 8)
    n_prev2 = pl.multiple_of(nval_ref[jnp.maximum(i - 2, 0)], 8)

    def each_row(n, start_one):
        def body(g, _):
            for k in range(8):
                start_one(g * 8 + k)
            return 0

        lax.fori_loop(0, lax.shift_right_logical(n, 3), body, 0)

    def start_gather(idx_ref, s, n):
        each_row(n, lambda r: pltpu.make_async_copy(h_hbm.at[pl.ds(idx_ref[0, 0, r], 1), :],
                                                    xbuf.at[s, pl.ds(r, 1), :], gsem.at[s]).start())

    def wait_gather(s, n):
        pltpu.make_async_copy(h_hbm.at[pl.ds(0, n), :], xbuf.at[s, pl.ds(0, n), :], gsem.at[s]).wait()

    def wait_scatter(s, n):
        pltpu.make_async_copy(obuf.at[s, pl.ds(0, n), :], ys_hbm.at[pl.ds(0, n), :], ssem.at[s]).wait()

    @pl.when(i == 0)
    def _():
        xbuf[...] = jnp.zeros_like(xbuf)
        spare = pltpu.make_async_copy(xbuf.at[0, pl.ds(0, 8), :], ys_hbm.at[pl.ds(ys_hbm.shape[0] - 8, 8), :],
                                      ssem.at[0])
        spare.start()
        spare.wait()

        @pl.when(n_cur > 0)
        def _():
            start_gather(src_ref, 0, n_cur)

    @pl.when((i < last) & (n_next > 0))
    def _():
        start_gather(nsrc_ref, other, n_next)

    @pl.when((i >= 2) & (n_prev2 > 0))
    def _():
        wait_scatter(slot, n_prev2)

    @pl.when(n_cur > 0)
    def _():
        changed = jnp.logical_or(i == 0, bexp_ref[i] != bexp_ref[jnp.maximum(i - 1, 0)])

        @pl.when(changed)
        def _():
            wgb[...] = wg_ref[...].astype(BF16)
            wub[...] = wu_ref[...].astype(BF16)
            wdb[...] = wd_ref[...].astype(BF16)

        wait_gather(slot, n_cur)
        xb = xbuf[slot].astype(BF16)
        g = jnp.dot(xb, wgb[...], preferred_element_type=F32)
        u = jnp.dot(xb, wub[...], preferred_element_type=F32)
        mid = (g * jax.nn.sigmoid(g)) * u
        obuf[slot] = jnp.dot(mid.astype(BF16), wdb[...], preferred_element_type=F32)
        each_row(n_cur, lambda r: pltpu.make_async_copy(obuf.at[slot, pl.ds(r, 1), :],
                                                        ys_hbm.at[pl.ds(dst_ref[0, 0, r], 1), :],
                                                        ssem.at[slot]).start())

    @pl.when(i == last)
    def _():
        @pl.when((i >= 1) & (n_prev > 0))
        def _():
            wait_scatter(other, n_prev)

        @pl.when(n_cur > 0)
        def _():
            wait_scatter(slot, n_cur)


def _experts(blk_exp, nvalid, row_src, row_dst, h_all, w_gate, w_up, w_down, ys_rows):
    nblk = blk_exp.shape[0]
    rows = EXPERT_ROWS
    idx_spec = pl.BlockSpec((1, 1, rows), lambda i, be, nv: (i, 0, 0), memory_space=pltpu.SMEM)
    nxt_spec = pl.BlockSpec((1, 1, rows), lambda i, be, nv: (jnp.minimum(i + 1, nblk - 1), 0, 0),
                            memory_space=pltpu.SMEM)
    grid_spec = pltpu.PrefetchScalarGridSpec(
        num_scalar_prefetch=2,
        grid=(nblk,),
        in_specs=[idx_spec, nxt_spec, idx_spec,
                  pl.BlockSpec(memory_space=pl.ANY),
                  pl.BlockSpec((None, D_MODEL, D_EXPERT), lambda i, be, nv: (be[i], 0, 0)),
                  pl.BlockSpec((None, D_MODEL, D_EXPERT), lambda i, be, nv: (be[i], 0, 0)),
                  pl.BlockSpec((None, D_EXPERT, D_MODEL), lambda i, be, nv: (be[i], 0, 0))],
        out_specs=pl.BlockSpec(memory_space=pl.ANY),
        scratch_shapes=[pltpu.VMEM((2, rows, D_MODEL), F32), pltpu.VMEM((2, rows, D_MODEL), F32),
                        pltpu.VMEM((D_MODEL, D_EXPERT), BF16), pltpu.VMEM((D_MODEL, D_EXPERT), BF16),
                        pltpu.VMEM((D_EXPERT, D_MODEL), BF16),
                        pltpu.SemaphoreType.DMA((2,)), pltpu.SemaphoreType.DMA((2,))],
    )
    return pl.pallas_call(
        _expert_kernel,
        grid_spec=grid_spec,
        out_shape=jax.ShapeDtypeStruct((ys_rows, D_MODEL), F32),
        compiler_params=_cparams(("arbitrary",)),
        name="experts",
    )(blk_exp, nvalid, row_src, row_src, row_dst, h_all, w_gate, w_up, w_down)


def _combine_kernel(x1_ref, y0_ref, y1_ref, meta_ref, op_ref, os_ref, *, nb_prompt):
    meta = meta_ref[...]
    y = x1_ref[...] + meta[:, 4:5] * y0_ref[...] + meta[:, 5:6] * y1_ref[...]
    i = pl.program_id(0)

    @pl.when(i < nb_prompt)
    def _():
        op_ref[...] = y

    @pl.when(i >= nb_prompt)
    def _():
        os_ref[...] = y


def _combine(x1, ys, meta, t_prompt):
    t_all = x1.shape[0]
    tm = MERGE_ROWS
    nb = t_all // tm
    nbp = t_prompt // tm
    return pl.pallas_call(
        functools.partial(_combine_kernel, nb_prompt=nbp),
        grid=(nb,),
        in_specs=[pl.BlockSpec((tm, D_MODEL), lambda i: (i, 0)),
                  pl.BlockSpec((tm, D_MODEL), lambda i: (i, 0)),
                  pl.BlockSpec((tm, D_MODEL), lambda i: (i + nb, 0)),
                  pl.BlockSpec((tm, LANES), lambda i: (i, 0))],
        out_specs=[pl.BlockSpec((tm, D_MODEL), lambda i: (jnp.minimum(i, nbp - 1), 0)),
                   pl.BlockSpec((tm, D_MODEL), lambda i: (jnp.maximum(i - nbp, 0), 0))],
        out_shape=[jax.ShapeDtypeStruct((t_prompt, D_MODEL), F32),
                   jax.ShapeDtypeStruct((t_all - t_prompt, D_MODEL), F32)],
        compiler_params=_cparams(("arbitrary",)),
        name="combine",
    )(x1, ys, ys, meta)


def _prep_weights(norm_mix, w_in, g_q, g_k, w_pool, pool_scale, w_out, norm_ffn, w_group, w_expert):
    lane = jnp.arange(ATTN_WIDTH)
    hsum = (lane[:, None] // HEAD_DIM == lane[None, :] // HEAD_DIM).astype(BF16)
    w_router = jnp.zeros((D_MODEL, LANES), F32)
    w_router = w_router.at[:, 0:N_GROUPS].set(w_group).at[:, N_GROUPS:N_GROUPS + N_EXPERTS].set(w_expert)
    w_router_hi = w_router.astype(BF16)
    w_router_lo = (w_router - w_router_hi.astype(F32)).astype(BF16)
    r = jnp.arange(MERGE_ROWS)
    return {
        "nrm": norm_mix.reshape(1, D_MODEL),
        "w_in": w_in.astype(BF16),
        "gq": (jnp.tile(g_q, N_HEADS) * (HEAD_DIM ** -0.5 * LOG2E)).reshape(1, ATTN_WIDTH),
        "gk": jnp.tile(g_k, N_HEADS).reshape(1, ATTN_WIDTH),
        "hsum": hsum,
        "w_pool": w_pool.astype(BF16),
        "pool_scale": pool_scale.reshape(1, POOL_WIDTH),
        "w_out_pool": w_out[:POOL_WIDTH].astype(BF16),
        "w_out_attn": w_out[POOL_WIDTH:].astype(BF16),
        "nrm_ffn": norm_ffn.reshape(1, D_MODEL),
        "w_router_hi": w_router_hi,
        "w_router_lo": w_router_lo,
        "tri_merge": (r[None, :] < r[:, None]).astype(BF16),
    }


def _later_sums(n_keys, n_heads):
    idx = jnp.arange(n_heads * n_keys)
    same = idx[:, None] // n_keys == idx[None, :] // n_keys
    later = same & (idx[:, None] > idx[None, :])
    return jnp.concatenate([later, same], axis=1).astype(BF16)


def kernel(x_prompt, x_sample, cache_k, cache_v, state_pool, page_table, meta_tokens, norm_mix, w_in, g_q, g_k,
           sb_bias, w_pool, pool_scale, w_out, norm_ffn, w_group, w_expert, w_gate, w_up, w_down):
    nb, seq, _ = x_prompt.shape
    nseq, nt, _ = x_sample.shape
    n_phys = cache_k.shape[1]
    t_prompt = nb * seq
    t_sample = nseq * nt
    t_all = t_prompt + t_sample

    cw = _prep_weights(norm_mix[0], w_in[0], g_q[0], g_k[0], w_pool[0], pool_scale[0], w_out[0], norm_ffn[0],
                       w_group[0], w_expert[0])
    bias = sb_bias[0].astype(F32) * LOG2E

    zero_hist = jnp.zeros((HIST_ROWS, POOL_WIDTH), F32)
    _, k_m, v_m, kx_m, vx_m, _, tail_m = _proj_seq(meta_tokens[None], zero_hist, cw, tm=N_META, p0=0,
                                                   full_count=False)
    pad_rows = ((0, ATT_BLOCK - N_META), (0, 0))
    kxm = jnp.pad(kx_m[0], pad_rows)
    vxm = jnp.pad(vx_m[0], pad_rows)

    q_p, k_p, v_p, kx_p, vx_p, po_p, tail_p = _proj_seq(x_prompt, tail_m[0], cw, tm=PROJ_ROWS, p0=N_META,
                                                        full_count=True)
    head_bias = jnp.repeat(bias, ATT_BLOCK)
    col = jnp.arange(N_HEADS * ATT_BLOCK) % ATT_BLOCK
    bias_rows = jnp.stack([head_bias, jnp.where(col < N_META, head_bias, NEG)])
    attn_p = _attn_prompt(q_p, kx_p, vx_p, kxm, vxm, bias_rows, _later_sums(ATT_BLOCK, 2))

    x_s_tm = jnp.transpose(x_sample, (1, 0, 2)).reshape(t_sample, D_MODEL)
    hist_tm = jnp.transpose(state_pool[0], (1, 0, 2))
    q_s, k_s, v_s, po_s, tail_s = _proj_step(x_s_tm, hist_tm, cw, nseq=nseq, nt=nt)
    to_seq = lambda a: jnp.transpose(a.reshape(nt, nseq, a.shape[-1]), (1, 0, 2))
    q_sb, k_sb, v_sb, po_sb = to_seq(q_s), to_seq(k_s), to_seq(v_s), to_seq(po_s)
    head_of_lane = jnp.arange(ATTN_WIDTH) // HEAD_DIM
    head_mask = head_of_lane[None, :] == jnp.arange(N_HEADS)[:, None]
    qbd = jnp.where(head_mask[None, None], q_sb[:, :, None, :], 0).reshape(nseq, nt * N_HEADS, ATTN_WIDTH)
    pad_new = ((0, 0), (0, 8 - nt), (0, 0))
    kn = jnp.pad(k_sb, pad_new)
    vn = jnp.pad(v_sb, pad_new)
    bias_col = jnp.tile(bias, nt).reshape(nt * N_HEADS, 1)
    page_t = lambda c: jnp.transpose(c[0], (0, 2, 3, 1)).reshape(n_phys, ATTN_WIDTH, PAGE)
    attn_s = _attn_sample(page_table, qbd, kn, vn, bias_col, _later_sums(PAGE, 1), page_t(cache_k), page_t(cache_v))

    x1, h_all, meta, counts_f = _merge_router(
        x_prompt.reshape(t_prompt, D_MODEL), x_sample.reshape(t_sample, D_MODEL),
        po_p.reshape(t_prompt, POOL_WIDTH), po_sb.reshape(t_sample, POOL_WIDTH),
        attn_p.reshape(t_prompt, ATTN_WIDTH), attn_s.reshape(t_sample, ATTN_WIDTH).astype(BF16), cw)

    pair_exp = meta[:, 0:2].astype(jnp.int32)
    pair_rank = meta[:, 2:4].astype(jnp.int32)
    counts = counts_f[0, :N_EXPERTS].astype(jnp.int32)
    rows = EXPERT_ROWS
    padded = (counts + rows - 1) // rows * rows
    pad_end = jnp.cumsum(padded)
    pad_start = pad_end - padded
    nblk = -(-(2 * t_all) // rows) + N_EXPERTS
    dest = (pad_start[pair_exp] + pair_rank).reshape(-1)
    pair_id = jnp.arange(2 * t_all, dtype=jnp.int32)
    row_pair = jnp.full((nblk * rows,), -1, jnp.int32).at[dest].set(pair_id)
    row_src = jnp.maximum(row_pair, 0) >> 1
    spare = 2 * t_all + (jnp.arange(nblk * rows, dtype=jnp.int32) & 7)
    row_dst = jnp.where(row_pair >= 0, (row_pair & 1) * t_all + (row_pair >> 1), spare)
    blk_row0 = jnp.arange(nblk, dtype=jnp.int32) * rows
    blk_exp = jnp.minimum(jnp.sum(blk_row0[:, None] >= pad_end[None, :], axis=1), N_EXPERTS - 1).astype(jnp.int32)
    nvalid = jnp.clip(counts[blk_exp] - (blk_row0 - pad_start[blk_exp]), 0, rows)
    nvalid = jnp.where(blk_row0 < pad_end[-1], (nvalid + 7) // 8 * 8, 0).astype(jnp.int32)

    ys = _experts(blk_exp, nvalid, row_src.reshape(nblk, 1, rows), row_dst.reshape(nblk, 1, rows), h_all,
                  w_gate[0], w_up[0], w_down[0], 2 * t_all + 8)

    y_prompt, y_sample = _combine(x1, ys, meta, t_prompt)
    y_prompt = y_prompt.reshape(nb, seq, D_MODEL)
    y_sample = y_sample.reshape(nseq, nt, D_MODEL)

    heads = lambda a: a.reshape(a.shape[:-1] + (N_HEADS, HEAD_DIM))
    k_prompt = jnp.concatenate([jnp.broadcast_to(k_m, (nb, N_META, ATTN_WIDTH)), k_p], axis=1)
    v_prompt = jnp.concatenate([jnp.broadcast_to(v_m, (nb, N_META, ATTN_WIDTH)), v_p], axis=1)
    pool_sample = jnp.transpose(tail_s, (1, 0, 2))
    return (y_prompt, y_sample, heads(k_prompt)[None], heads(v_prompt)[None], tail_p[:, 1:][None],
            heads(k_sb)[None], heads(v_sb)[None], pool_sample[None])
```

```python
import functools

import jax
import jax.numpy as jnp
from jax import lax
from jax.experimental import pallas as pl
from jax.experimental.pallas import tpu as pltpu

F32 = jnp.float32
BF16 = jnp.bfloat16

D_MODEL = 1024
POOL_WIDTH = 512
ATTN_WIDTH = 512
HEAD_DIM = 64
N_HEADS = 8
N_PAIRS = N_HEADS // 2
POOL_WINDOWS = (2, 4, 8, 16)
POOL_GROUP = 128
POOL_HIST = 15
HIST_ROWS = 16
N_META = 16
PAGE = 128
N_GROUPS = 4
EXPERTS_PER_GROUP = 8
N_EXPERTS = 32
D_EXPERT = 512
EPS = 1e-6
NEG = -1e30
LOG2E = 1.4426950408889634
LOGIT_CAP = 100.0
LANES = 128
ATT_BLOCK = 128
PROJ_ROWS = 512
MERGE_ROWS = 256
EXPERT_ROWS = 256
ROW_GROUP = 8
VMEM_LIMIT = 56 * 1024 * 1024


def _cparams(sem, vmem=VMEM_LIMIT):
    return pltpu.CompilerParams(dimension_semantics=sem, vmem_limit_bytes=vmem)


def _rms_rows(x, g):
    ms = jnp.mean(x * x, axis=-1, keepdims=True)
    return (x * lax.rsqrt(ms + EPS)) * g


def _head_norm(a, g, hsum):
    sq = a * a
    hi = sq.astype(BF16)
    lo = (sq - hi.astype(F32)).astype(BF16)
    ssum = (jnp.dot(hi, hsum, preferred_element_type=F32)
            + jnp.dot(lo, hsum, preferred_element_type=F32))
    return (a * lax.rsqrt(ssum * (1.0 / HEAD_DIM) + EPS)) * g


def _project(x, nrm, w_in, gq, gk, hsum):
    n = _rms_rows(x, nrm)
    proj = jnp.dot(n.astype(BF16), w_in, preferred_element_type=F32)
    u = proj[:, 0:POOL_WIDTH]
    q = _head_norm(proj[:, POOL_WIDTH:POOL_WIDTH + ATTN_WIDTH], gq, hsum)
    k = _head_norm(proj[:, POOL_WIDTH + ATTN_WIDTH:POOL_WIDTH + 2 * ATTN_WIDTH], gk, hsum)
    v = proj[:, POOL_WIDTH + 2 * ATTN_WIDTH:]
    return u, q, k, v


def _split_heads(a, out_ref):
    rows = a.shape[0]
    lane = lax.broadcasted_iota(jnp.int32, (rows, LANES), 1)
    first = lane < HEAD_DIM
    for p in range(N_PAIRS):
        a2 = a[:, p * LANES:(p + 1) * LANES]
        out_ref[:, 2 * p * LANES:(2 * p + 1) * LANES] = jnp.where(first, a2, 0.0).astype(out_ref.dtype)
        out_ref[:, (2 * p + 1) * LANES:(2 * p + 2) * LANES] = jnp.where(first, 0.0, a2).astype(out_ref.dtype)


def _log2_gates(z):
    z = jnp.minimum(z, LOGIT_CAP)
    drop = jnp.log(1.0 + jnp.exp2(z)) * LOG2E
    return z - drop, drop


def _stick_step(tt, gate_in, mix_in):
    def stack_heads(b):
        half = b.shape[1] // 2
        return jnp.concatenate([b[:, :half], b[:, half:]], axis=0)

    nt_dims = (((1,), (1,)), ((), ()))
    gates = mixed = None
    if mix_in is not None:
        log_betas, drops, carries, vxbs = mix_in
        sums = [jnp.dot(drop, tt, preferred_element_type=F32) for drop in drops]
    if gate_in is not None:
        q2s, kxbs, brows, mask = gate_in
        zs = [lax.dot_general(q2, stack_heads(kxb), nt_dims, preferred_element_type=F32) + brow
              for q2, kxb, brow in zip(q2s, kxbs, brows)]
        if mask is not None:
            zs = [jnp.where(mask, z, NEG) for z in zs]
    if mix_in is not None:
        nk = sums[0].shape[1] // 2
        probs = [jnp.exp2(lb - s[:, :nk] - carry).astype(BF16) for lb, s, carry in zip(log_betas, sums, carries)]
    if gate_in is not None:
        gates = [(lb, lk.astype(BF16)) for lb, lk in (_log2_gates(z) for z in zs)]
    if mix_in is not None:
        outs = [jnp.dot(a, stack_heads(vxb), preferred_element_type=F32) for a, vxb in zip(probs, vxbs)]
        mixed = [(out, carry + s[:, nk:]) for out, carry, s in zip(outs, carries, sums)]
    return gates, mixed


def _proj_seq_kernel(x_ref, hist_ref, nrm_ref, win_ref, gq_ref, gk_ref, hsum_ref, wpool_ref, ps_ref,
                     q_ref, k_ref, v_ref, kx_ref, vx_ref, po_ref, tail_ref, ext_ref, *, tm, p0, full_count):
    blk = pl.program_id(1)
    u, q, k, v = _project(x_ref[...], nrm_ref[...], win_ref[...], gq_ref[...], gk_ref[...], hsum_ref[...])
    q_ref[...] = q.astype(q_ref.dtype)
    k_ref[...] = k
    v_ref[...] = v
    _split_heads(k, kx_ref)
    _split_heads(v, vx_ref)

    @pl.when(blk == 0)
    def _():
        ext_ref[0:HIST_ROWS, :] = hist_ref[...]

    ext_ref[HIST_ROWS:HIST_ROWS + tm, :] = u
    for g, w in enumerate(POOL_WINDOWS):
        e = ext_ref[:, g * POOL_GROUP:(g + 1) * POOL_GROUP]
        s = e
        sh = 1
        while sh < w:
            s = s + pltpu.roll(s, sh, axis=0)
            sh *= 2
        win = s[HIST_ROWS:, :]
        tok = e[HIST_ROWS:, :]
        if full_count:
            d = win * (1.0 / w) - tok
        else:
            pos = p0 + lax.broadcasted_iota(jnp.int32, (tm, 1), 0)
            cnt = jnp.minimum(w, pos + 1).astype(F32)
            d = win / cnt - tok
        y = jnp.dot(d.astype(BF16), wpool_ref[g], preferred_element_type=F32)
        y = y * ps_ref[:, g * POOL_GROUP:(g + 1) * POOL_GROUP]
        po_ref[:, g * POOL_GROUP:(g + 1) * POOL_GROUP] = y.astype(po_ref.dtype)
    last = ext_ref[tm:tm + HIST_ROWS, :]
    ext_ref[0:HIST_ROWS, :] = last

    @pl.when(blk == pl.num_programs(1) - 1)
    def _():
        tail_ref[...] = last


def _proj_seq(x, hist, cw, *, tm, p0, full_count):
    b, l, _ = x.shape
    nb = l // tm
    const = lambda shape: pl.BlockSpec(shape, lambda i, j: (0,) * len(shape))
    row = lambda width: pl.BlockSpec((None, tm, width), lambda i, j: (i, j, 0))
    kern = functools.partial(_proj_seq_kernel, tm=tm, p0=p0, full_count=full_count)
    return pl.pallas_call(
        kern,
        grid=(b, nb),
        in_specs=[row(D_MODEL), const((HIST_ROWS, POOL_WIDTH)), const((1, D_MODEL)),
                  const((D_MODEL, 2 * D_MODEL)), const((1, ATTN_WIDTH)), const((1, ATTN_WIDTH)),
                  const((ATTN_WIDTH, ATTN_WIDTH)), const((4, POOL_GROUP, POOL_GROUP)), const((1, POOL_WIDTH))],
        out_specs=[row(ATTN_WIDTH), row(ATTN_WIDTH), row(ATTN_WIDTH), row(2 * ATTN_WIDTH), row(2 * ATTN_WIDTH),
                   row(POOL_WIDTH), pl.BlockSpec((None, HIST_ROWS, POOL_WIDTH), lambda i, j: (i, 0, 0))],
        out_shape=[jax.ShapeDtypeStruct((b, l, ATTN_WIDTH), BF16),
                   jax.ShapeDtypeStruct((b, l, ATTN_WIDTH), F32),
                   jax.ShapeDtypeStruct((b, l, ATTN_WIDTH), F32),
                   jax.ShapeDtypeStruct((b, l, 2 * ATTN_WIDTH), BF16),
                   jax.ShapeDtypeStruct((b, l, 2 * ATTN_WIDTH), BF16),
                   jax.ShapeDtypeStruct((b, l, POOL_WIDTH), BF16),
                   jax.ShapeDtypeStruct((b, HIST_ROWS, POOL_WIDTH), F32)],
        scratch_shapes=[pltpu.VMEM((HIST_ROWS + tm, POOL_WIDTH), F32)],
        compiler_params=_cparams(("arbitrary", "arbitrary")),
        name="proj_seq",
    )(x, hist, cw["nrm"], cw["w_in"], cw["gq"], cw["gk"], cw["hsum"], cw["w_pool"], cw["pool_scale"])


def _proj_step_kernel(x_ref, hist_ref, nrm_ref, win_ref, gq_ref, gk_ref, hsum_ref, wpool_ref, ps_ref,
                      q_ref, k_ref, v_ref, po_ref, tail_ref, *, nseq, nt):
    u, q, k, v = _project(x_ref[...], nrm_ref[...], win_ref[...], gq_ref[...], gk_ref[...], hsum_ref[...])
    q_ref[...] = q.astype(q_ref.dtype)
    k_ref[...] = k
    v_ref[...] = v
    ext = [hist_ref[r] for r in range(POOL_HIST)] + [u[t * nseq:(t + 1) * nseq, :] for t in range(nt)]
    for r in range(POOL_HIST):
        tail_ref[r] = ext[nt + r]
    for g, w in enumerate(POOL_WINDOWS):
        sl = slice(g * POOL_GROUP, (g + 1) * POOL_GROUP)
        ds = []
        for t in range(nt):
            win = ext[POOL_HIST + t][:, sl]
            for i in range(1, w):
                win = win + ext[POOL_HIST + t - i][:, sl]
            ds.append(win * (1.0 / w) - ext[POOL_HIST + t][:, sl])
        d = jnp.concatenate(ds, axis=0)
        y = jnp.dot(d.astype(BF16), wpool_ref[g], preferred_element_type=F32) * ps_ref[:, sl]
        po_ref[:, sl] = y.astype(po_ref.dtype)


def _proj_step(x_tm, hist_tm, cw, *, nseq, nt):
    rows = nseq * nt
    kern = functools.partial(_proj_step_kernel, nseq=nseq, nt=nt)
    return pl.pallas_call(
        kern,
        out_shape=[jax.ShapeDtypeStruct((rows, ATTN_WIDTH), BF16),
                   jax.ShapeDtypeStruct((rows, ATTN_WIDTH), F32),
                   jax.ShapeDtypeStruct((rows, ATTN_WIDTH), F32),
                   jax.ShapeDtypeStruct((rows, POOL_WIDTH), BF16),
                   jax.ShapeDtypeStruct((POOL_HIST, nseq, POOL_WIDTH), F32)],
        compiler_params=pltpu.CompilerParams(vmem_limit_bytes=VMEM_LIMIT),
        name="proj_step",
    )(x_tm, hist_tm, cw["nrm"], cw["w_in"], cw["gq"], cw["gk"], cw["hsum"], cw["w_pool"], cw["pool_scale"])


def _attn_prompt_kernel(q_ref, kx_ref, vx_ref, kxm_ref, vxm_ref, bias_ref, tt_ref, o_ref,
                        carry_ref, acc_ref, lb_ref, lk_ref):
    qi = pl.program_id(1)
    tt = tt_ref[...]
    nq = ATT_BLOCK
    pairs = range(N_PAIRS)
    lanes = [slice(p * LANES, (p + 1) * LANES) for p in pairs]
    wide = [slice(2 * p * LANES, 2 * (p + 1) * LANES) for p in pairs]

    def rows(j):
        return pl.ds(pl.multiple_of(j * ATT_BLOCK, ATT_BLOCK), ATT_BLOCK)

    def gate_in(k_of, bias_row, mask):
        return ([q_ref[:, l] for l in lanes], [k_of(w) for w in wide],
                [bias_ref[bias_row:bias_row + 1, w] for w in wide], mask)

    def mix_in(v_of):
        return ([lb_ref[p] for p in pairs], [lk_ref[p] for p in pairs], [carry_ref[p] for p in pairs],
                [v_of(w) for w in wide])

    def step(gin, min_, last=False):
        accs = None if min_ is None else [acc_ref[:, l] for l in lanes]
        gates, mixed = _stick_step(tt, gin, min_)
        if mixed is not None:
            for p, (out, carry) in enumerate(mixed):
                if last:
                    o_ref[:, lanes[p]] = (accs[p] + out).astype(o_ref.dtype)
                else:
                    carry_ref[p] = carry
                    acc_ref[:, lanes[p]] = accs[p] + out
        if gates is not None:
            for p, (lb, lk) in enumerate(gates):
                lb_ref[p] = lb
                lk_ref[p] = lk

    carry_ref[...] = jnp.zeros_like(carry_ref)
    acc_ref[...] = jnp.zeros_like(acc_ref)
    r_i = lax.broadcasted_iota(jnp.int32, (nq, 2 * ATT_BLOCK), 0)
    c_i = lax.broadcasted_iota(jnp.int32, (nq, 2 * ATT_BLOCK), 1) & (ATT_BLOCK - 1)
    step(gate_in(lambda w: kx_ref[rows(qi), w], 0, c_i < r_i), None)

    def body(t, _):
        j = qi - 1 - t
        step(gate_in(lambda w: kx_ref[rows(j), w], 0, None), mix_in(lambda w: vx_ref[rows(j + 1), w]))
        return 0

    lax.fori_loop(0, qi, body, 0)
    step(gate_in(lambda w: kxm_ref[:, w], 1, None), mix_in(lambda w: vx_ref[0:ATT_BLOCK, w]))
    step(None, mix_in(lambda w: vxm_ref[:, w]), last=True)


def _attn_prompt(q, kx, vx, kxm, vxm, bias_rows, tt):
    b, l, _ = q.shape
    nq = l // ATT_BLOCK
    once = pl.Buffered(1)
    return pl.pallas_call(
        _attn_prompt_kernel,
        grid=(b, nq),
        in_specs=[pl.BlockSpec((None, ATT_BLOCK, ATTN_WIDTH), lambda i, j: (i, j, 0)),
                  pl.BlockSpec((None, l, 2 * ATTN_WIDTH), lambda i, j: (i, 0, 0), pipeline_mode=once),
                  pl.BlockSpec((None, l, 2 * ATTN_WIDTH), lambda i, j: (i, 0, 0), pipeline_mode=once),
                  pl.BlockSpec((ATT_BLOCK, 2 * ATTN_WIDTH), lambda i, j: (0, 0)),
                  pl.BlockSpec((ATT_BLOCK, 2 * ATTN_WIDTH), lambda i, j: (0, 0)),
                  pl.BlockSpec((2, 2 * ATTN_WIDTH), lambda i, j: (0, 0)),
                  pl.BlockSpec((2 * ATT_BLOCK, 4 * ATT_BLOCK), lambda i, j: (0, 0))],
        out_specs=pl.BlockSpec((None, ATT_BLOCK, ATTN_WIDTH), lambda i, j: (i, j, 0)),
        out_shape=jax.ShapeDtypeStruct((b, l, ATTN_WIDTH), BF16),
        scratch_shapes=[pltpu.VMEM((N_PAIRS, ATT_BLOCK, 2 * ATT_BLOCK), F32),
                        pltpu.VMEM((ATT_BLOCK, ATTN_WIDTH), F32),
                        pltpu.VMEM((N_PAIRS, ATT_BLOCK, 2 * ATT_BLOCK), F32),
                        pltpu.VMEM((N_PAIRS, ATT_BLOCK, 2 * ATT_BLOCK), BF16)],
        compiler_params=_cparams(("arbitrary", "arbitrary")),
        name="attn_prompt",
    )(q, kx, vx, kxm, vxm, bias_rows, tt)


def _attn_sample_kernel(pt_ref, qbd_ref, kn_ref, vn_ref, bias_ref, tt_ref, *refs, n_pages, nt):
    del pt_ref
    k_refs = refs[:n_pages]
    v_refs = refs[n_pages:2 * n_pages]
    o_ref = refs[2 * n_pages]
    qbd = qbd_ref[...]
    tt = tt_ref[...]
    bias = bias_ref[...]
    nc = qbd.shape[0]
    nn = (((1,), (0,)), ((), ()))
    nt_dims = (((1,), (1,)), ((), ()))

    pad = jnp.zeros((PAGE - kn_ref.shape[0], ATTN_WIDTH), F32)
    kn = jnp.concatenate([kn_ref[...], pad], axis=0).astype(BF16)
    vn = jnp.concatenate([vn_ref[...], pad], axis=0).astype(BF16)
    t_i = lax.broadcasted_iota(jnp.int32, (nc, PAGE), 0) >> 3
    k_i = lax.broadcasted_iota(jnp.int32, (nc, PAGE), 1)
    z_new = lax.dot_general(qbd, kn, nt_dims, preferred_element_type=F32) + bias
    zs = [jnp.where(k_i < t_i, z_new, NEG)]
    for j in range(n_pages - 1, -1, -1):
        zs.append(lax.dot_general(qbd, k_refs[j][...].astype(BF16), nn, preferred_element_type=F32) + bias)
    gates = [_log2_gates(z) for z in zs]
    sums = [jnp.dot(lk.astype(BF16), tt, preferred_element_type=F32) for _, lk in gates]
    carry = jnp.zeros((nc, PAGE), F32)
    acc = None
    for i, ((lb, _), s) in enumerate(zip(gates, sums)):
        a = jnp.exp2(lb - s[:, :PAGE] - carry).astype(BF16)
        carry = carry + s[:, PAGE:]
        if i == 0:
            acc = jnp.dot(a, vn, preferred_element_type=F32)
        else:
            vt = v_refs[n_pages - i][...].astype(BF16)
            acc = acc + lax.dot_general(a, vt, nt_dims, preferred_element_type=F32)
    h_i = lax.broadcasted_iota(jnp.int32, (nc, ATTN_WIDTH), 0) & (N_HEADS - 1)
    l_i = lax.broadcasted_iota(jnp.int32, (nc, ATTN_WIDTH), 1) >> 6
    acc = jnp.where(h_i == l_i, acc, 0.0)
    for t in range(nt):
        o_ref[t:t + 1, :] = jnp.sum(acc[t * N_HEADS:(t + 1) * N_HEADS, :], axis=0, keepdims=True)


def _attn_sample(page_table, qbd, kn, vn, bias_col, tt, cache_k, cache_v):
    nseq, n_pages = page_table.shape
    nt = qbd.shape[1] // N_HEADS
    pt = page_table.reshape(-1)

    def page_spec(j):
        return pl.BlockSpec((None, ATTN_WIDTH, PAGE), lambda i, pt_ref: (pt_ref[i * n_pages + j], 0, 0))

    grid_spec = pltpu.PrefetchScalarGridSpec(
        num_scalar_prefetch=1,
        grid=(nseq,),
        in_specs=[pl.BlockSpec((None, nt * N_HEADS, ATTN_WIDTH), lambda i, pt_ref: (i, 0, 0)),
                  pl.BlockSpec((None, 8, ATTN_WIDTH), lambda i, pt_ref: (i, 0, 0)),
                  pl.BlockSpec((None, 8, ATTN_WIDTH), lambda i, pt_ref: (i, 0, 0)),
                  pl.BlockSpec((nt * N_HEADS, 1), lambda i, pt_ref: (0, 0)),
                  pl.BlockSpec((PAGE, 2 * PAGE), lambda i, pt_ref: (0, 0))]
                 + [page_spec(j) for j in range(n_pages)] + [page_spec(j) for j in range(n_pages)],
        out_specs=pl.BlockSpec((None, nt, ATTN_WIDTH), lambda i, pt_ref: (i, 0, 0)),
    )
    kern = functools.partial(_attn_sample_kernel, n_pages=n_pages, nt=nt)
    return pl.pallas_call(
        kern,
        grid_spec=grid_spec,
        out_shape=jax.ShapeDtypeStruct((nseq, nt, ATTN_WIDTH), F32),
        compiler_params=_cparams(("arbitrary",)),
        name="attn_sample",
    )(pt, qbd, kn, vn, bias_col, tt, *([cache_k] * n_pages), *([cache_v] * n_pages))


def _merge_router_kernel(xp_ref, xs_ref, pop_ref, pos_ref, atp_ref, ats_ref, wop_ref, woa_ref, nrm_ref, wrh_ref,
                         wrl_ref, tri_ref, x1_ref, h_ref, meta_ref, cout_ref, carry_ref, *, nb_prompt):
    i = pl.program_id(0)

    @pl.when(i == 0)
    def _():
        carry_ref[...] = jnp.zeros_like(carry_ref)

    is_p = i < nb_prompt
    x = jnp.where(is_p, xp_ref[...], xs_ref[...])
    po = jnp.where(is_p, pop_ref[...], pos_ref[...])
    at = jnp.where(is_p, atp_ref[...], ats_ref[...])
    y = (jnp.dot(po, wop_ref[...], preferred_element_type=F32)
         + jnp.dot(at, woa_ref[...], preferred_element_type=F32))
    x1 = x + y
    x1_ref[...] = x1
    h = _rms_rows(x1, nrm_ref[...])
    h_ref[...] = h
    hi = h.astype(BF16)
    lo = (h - hi.astype(F32)).astype(BF16)
    logits = (jnp.dot(hi, wrh_ref[...], preferred_element_type=F32)
              + jnp.dot(lo, wrh_ref[...], preferred_element_type=F32)
              + jnp.dot(hi, wrl_ref[...], preferred_element_type=F32))
    tm = logits.shape[0]
    lane = lax.broadcasted_iota(jnp.int32, (tm, LANES), 1).astype(F32)
    ninf = -jnp.inf
    is_g = lane < N_GROUPS
    gl = jnp.where(is_g, logits, ninf)
    gmax = jnp.max(gl, axis=-1, keepdims=True)
    g_sel = jnp.min(jnp.where(gl == gmax, lane, float(LANES)), axis=-1, keepdims=True)
    psum = jnp.sum(jnp.where(is_g, jnp.exp(logits - gmax), 0.0), axis=-1, keepdims=True)
    g_w = 1.0 / psum
    e_lo = N_GROUPS + g_sel * EXPERTS_PER_GROUP
    in_group = (lane >= e_lo) & (lane < e_lo + EXPERTS_PER_GROUP)
    el = jnp.where(in_group, logits, ninf)
    v1 = jnp.max(el, axis=-1, keepdims=True)
    i1 = jnp.min(jnp.where(el == v1, lane, float(LANES)), axis=-1, keepdims=True)
    el2 = jnp.where(lane == i1, ninf, el)
    v2 = jnp.max(el2, axis=-1, keepdims=True)
    i2 = jnp.min(jnp.where(el2 == v2, lane, float(LANES)), axis=-1, keepdims=True)
    e21 = jnp.exp(v2 - v1)
    den = 1.0 + e21
    gate1 = g_w * (1.0 / den)
    gate2 = g_w * (e21 / den)
    e1 = i1 - N_GROUPS
    e2 = i2 - N_GROUPS
    onehot = jnp.where((lane == e1) | (lane == e2), 1.0, 0.0)
    before = jnp.dot(tri_ref[...], onehot.astype(BF16), preferred_element_type=F32) + carry_ref[...]
    rank1 = jnp.sum(jnp.where(lane == e1, before, 0.0), axis=-1, keepdims=True)
    rank2 = jnp.sum(jnp.where(lane == e2, before, 0.0), axis=-1, keepdims=True)
    carry = carry_ref[...] + jnp.sum(onehot, axis=0, keepdims=True)
    carry_ref[...] = carry
    cout_ref[...] = carry
    meta = jnp.where(lane == 0, e1, 0.0)
    meta = jnp.where(lane == 1, e2, meta)
    meta = jnp.where(lane == 2, rank1, meta)
    meta = jnp.where(lane == 3, rank2, meta)
    meta = jnp.where(lane == 4, gate1, meta)
    meta = jnp.where(lane == 5, gate2, meta)
    meta_ref[...] = meta


def _merge_router(x_p, x_s, po_p, po_s, at_p, at_s, cw):
    tm = MERGE_ROWS
    nbp = x_p.shape[0] // tm
    nbs = x_s.shape[0] // tm
    const = lambda shape: pl.BlockSpec(shape, lambda i: (0,) * len(shape))
    prow = lambda width: pl.BlockSpec((tm, width), lambda i: (jnp.minimum(i, nbp - 1), 0))
    srow = lambda width: pl.BlockSpec((tm, width), lambda i: (jnp.maximum(i - nbp, 0), 0))
    row = lambda width: pl.BlockSpec((tm, width), lambda i: (i, 0))
    t_all = (nbp + nbs) * tm
    return pl.pallas_call(
        functools.partial(_merge_router_kernel, nb_prompt=nbp),
        grid=(nbp + nbs,),
        in_specs=[prow(D_MODEL), srow(D_MODEL), prow(POOL_WIDTH), srow(POOL_WIDTH), prow(ATTN_WIDTH),
                  srow(ATTN_WIDTH), const((POOL_WIDTH, D_MODEL)), const((ATTN_WIDTH, D_MODEL)),
                  const((1, D_MODEL)), const((D_MODEL, LANES)), const((D_MODEL, LANES)), const((tm, tm))],
        out_specs=[row(D_MODEL), row(D_MODEL), row(LANES), const((1, LANES))],
        out_shape=[jax.ShapeDtypeStruct((t_all, D_MODEL), F32),
                   jax.ShapeDtypeStruct((t_all, D_MODEL), F32),
                   jax.ShapeDtypeStruct((t_all, LANES), F32),
                   jax.ShapeDtypeStruct((1, LANES), F32)],
        scratch_shapes=[pltpu.VMEM((1, LANES), F32)],
        compiler_params=_cparams(("arbitrary",)),
        name="merge_router",
    )(x_p, x_s, po_p, po_s, at_p, at_s, cw["w_out_pool"], cw["w_out_attn"], cw["nrm_ffn"], cw["w_router_hi"],
      cw["w_router_lo"], cw["tri_merge"])


def _dispatch_kernel(fill_r0_ref, fill_nr_ref, fill_g0_ref, fill_ng_ref, dest_ref, h_ref, hs_hbm, zbuf, sem):
    i = pl.program_id(0)
    tm = h_ref.shape[0]

    def row_of(r):
        return hs_hbm.at[r >> 3, pl.ds(r & (ROW_GROUP - 1), 1), :]

    @pl.when(i == 0)
    def _():
        zbuf[...] = jnp.zeros_like(zbuf)

        def fill(wait):
            def body(blk, _):
                for k in range(ROW_GROUP - 1):
                    @pl.when(k < fill_nr_ref[blk])
                    def _():
                        cp = pltpu.make_async_copy(zbuf.at[0, pl.ds(0, 1), :], row_of(fill_r0_ref[blk] + k), sem)
                        cp.wait() if wait else cp.start()

                ng = fill_ng_ref[blk]

                @pl.when(ng > 0)
                def _():
                    cp = pltpu.make_async_copy(zbuf.at[pl.ds(0, ng)], hs_hbm.at[pl.ds(fill_g0_ref[blk], ng)], sem)
                    cp.wait() if wait else cp.start()
                return 0

            lax.fori_loop(0, fill_r0_ref.shape[0], body, 0)

        fill(False)
        fill(True)

    def rows8(g, _):
        for k in range(ROW_GROUP):
            r = g * ROW_GROUP + k
            for e in range(2):
                pltpu.make_async_copy(h_ref.at[pl.ds(r, 1), :], row_of(dest_ref[0, 0, 2 * r + e]), sem).start()
        return 0

    lax.fori_loop(0, tm // ROW_GROUP, rows8, 0)
    for _ in range(2):
        pltpu.make_async_copy(zbuf, hs_hbm.at[pl.ds(0, tm // ROW_GROUP)], sem).wait()


def _dispatch(fills, dest, h_all, n_sorted):
    t_all = h_all.shape[0]
    tm = EXPERT_ROWS
    grid_spec = pltpu.PrefetchScalarGridSpec(
        num_scalar_prefetch=4,
        grid=(t_all // tm,),
        in_specs=[pl.BlockSpec((1, 1, 2 * tm), lambda i, *_: (i, 0, 0), memory_space=pltpu.SMEM),
                  pl.BlockSpec((tm, D_MODEL), lambda i, *_: (i, 0))],
        out_specs=pl.BlockSpec(memory_space=pl.ANY),
        scratch_shapes=[pltpu.VMEM((tm // ROW_GROUP, ROW_GROUP, D_MODEL), F32), pltpu.SemaphoreType.DMA(())],
    )
    return pl.pallas_call(
        _dispatch_kernel,
        grid_spec=grid_spec,
        out_shape=jax.ShapeDtypeStruct((n_sorted // ROW_GROUP, ROW_GROUP, D_MODEL), F32),
        compiler_params=_cparams(("arbitrary",)),
        name="dispatch",
    )(*fills, dest.reshape(t_all // tm, 1, 2 * tm), h_all)


def _expert_kernel(bexp_ref, nval_ref, hs_hbm, wg_ref, wu_ref, wd_ref, ys_ref, xbuf, wgb, wub, wdb, sem):
    i = pl.program_id(0)
    last = pl.num_programs(0) - 1
    slot = i & 1
    groups = xbuf.shape[1]
    g_cur = lax.shift_right_logical(nval_ref[i], 3)
    g_next = lax.shift_right_logical(nval_ref[jnp.minimum(i + 1, last)], 3)

    def fetch(blk, s, ng):
        return pltpu.make_async_copy(hs_hbm.at[pl.ds(blk * groups, ng)], xbuf.at[s, pl.ds(0, ng)], sem.at[s])

    @pl.when(i == 0)
    def _():
        xbuf[...] = jnp.zeros_like(xbuf)

        @pl.when(g_cur > 0)
        def _():
            fetch(0, 0, g_cur).start()

    @pl.when((i < last) & (g_next > 0))
    def _():
        fetch(i + 1, 1 - slot, g_next).start()

    @pl.when(g_cur > 0)
    def _():
        changed = jnp.logical_or(i == 0, bexp_ref[i] != bexp_ref[jnp.maximum(i - 1, 0)])

        @pl.when(changed)
        def _():
            wgb[...] = wg_ref[...].astype(BF16)
            wub[...] = wu_ref[...].astype(BF16)
            wdb[...] = wd_ref[...].astype(BF16)

        fetch(i, slot, g_cur).wait()
        xb = xbuf[slot].reshape(groups * ROW_GROUP, D_MODEL).astype(BF16)
        g = jnp.dot(xb, wgb[...], preferred_element_type=F32)
        u = jnp.dot(xb, wub[...], preferred_element_type=F32)
        mid = (g * jax.nn.sigmoid(g)) * u
        ys_ref[...] = jnp.dot(mid.astype(BF16), wdb[...], preferred_element_type=F32)

    @pl.when(g_cur == 0)
    def _():
        ys_ref[...] = jnp.zeros_like(ys_ref)


def _experts(blk_exp, nvalid, hs, w_gate, w_up, w_down):
    nblk = blk_exp.shape[0]
    rows = EXPERT_ROWS
    grid_spec = pltpu.PrefetchScalarGridSpec(
        num_scalar_prefetch=2,
        grid=(nblk,),
        in_specs=[pl.BlockSpec(memory_space=pl.ANY),
                  pl.BlockSpec((None, D_MODEL, D_EXPERT), lambda i, be, nv: (be[i], 0, 0)),
                  pl.BlockSpec((None, D_MODEL, D_EXPERT), lambda i, be, nv: (be[i], 0, 0)),
                  pl.BlockSpec((None, D_EXPERT, D_MODEL), lambda i, be, nv: (be[i], 0, 0))],
        out_specs=pl.BlockSpec((rows, D_MODEL), lambda i, be, nv: (i, 0)),
        scratch_shapes=[pltpu.VMEM((2, rows // ROW_GROUP, ROW_GROUP, D_MODEL), F32),
                        pltpu.VMEM((D_MODEL, D_EXPERT), BF16), pltpu.VMEM((D_MODEL, D_EXPERT), BF16),
                        pltpu.VMEM((D_EXPERT, D_MODEL), BF16),
                        pltpu.SemaphoreType.DMA((2,))],
    )
    return pl.pallas_call(
        _expert_kernel,
        grid_spec=grid_spec,
        out_shape=jax.ShapeDtypeStruct((nblk * rows, D_MODEL), F32),
        compiler_params=_cparams(("arbitrary",)),
        name="experts",
    )(blk_exp, nvalid, hs, w_gate, w_up, w_down)


def _combine_kernel(dcur_ref, dnxt_ref, x1_ref, meta_ref, ys_hbm, op_ref, os_ref, gbuf, sem, *, nb_prompt):
    i = pl.program_id(0)
    last = pl.num_programs(0) - 1
    slot = i & 1
    tm = x1_ref.shape[0]

    def start_gather(d_ref, s):
        def rows8(g, _):
            for k in range(ROW_GROUP):
                r = g * ROW_GROUP + k
                for e in range(2):
                    pltpu.make_async_copy(ys_hbm.at[pl.ds(d_ref[0, 0, 2 * r + e], 1), :],
                                          gbuf.at[s, e, pl.ds(r, 1), :], sem.at[s]).start()
            return 0

        lax.fori_loop(0, tm // ROW_GROUP, rows8, 0)

    @pl.when(i == 0)
    def _():
        start_gather(dcur_ref, 0)

    @pl.when(i < last)
    def _():
        start_gather(dnxt_ref, 1 - slot)

    for e in range(2):
        pltpu.make_async_copy(ys_hbm.at[pl.ds(0, tm), :], gbuf.at[slot, e], sem.at[slot]).wait()
    meta = meta_ref[...]
    y = x1_ref[...] + meta[:, 4:5] * gbuf[slot, 0] + meta[:, 5:6] * gbuf[slot, 1]

    @pl.when(i < nb_prompt)
    def _():
        op_ref[...] = y

    @pl.when(i >= nb_prompt)
    def _():
        os_ref[...] = y


def _combine(x1, ys, meta, dest, t_prompt):
    t_all = x1.shape[0]
    tm = MERGE_ROWS
    nb = t_all // tm
    nbp = t_prompt // tm
    dest = dest.reshape(nb, 1, 2 * tm)
    grid_spec = pl.GridSpec(
        grid=(nb,),
        in_specs=[pl.BlockSpec((1, 1, 2 * tm), lambda i: (i, 0, 0), memory_space=pltpu.SMEM),
                  pl.BlockSpec((1, 1, 2 * tm), lambda i: (jnp.minimum(i + 1, nb - 1), 0, 0),
                               memory_space=pltpu.SMEM),
                  pl.BlockSpec((tm, D_MODEL), lambda i: (i, 0)),
                  pl.BlockSpec((tm, LANES), lambda i: (i, 0)),
                  pl.BlockSpec(memory_space=pl.ANY)],
        out_specs=[pl.BlockSpec((tm, D_MODEL), lambda i: (jnp.minimum(i, nbp - 1), 0)),
                   pl.BlockSpec((tm, D_MODEL), lambda i: (jnp.maximum(i - nbp, 0), 0))],
        scratch_shapes=[pltpu.VMEM((2, 2, tm, D_MODEL), F32), pltpu.SemaphoreType.DMA((2,))],
    )
    return pl.pallas_call(
        functools.partial(_combine_kernel, nb_prompt=nbp),
        grid_spec=grid_spec,
        out_shape=[jax.ShapeDtypeStruct((t_prompt, D_MODEL), F32),
                   jax.ShapeDtypeStruct((t_all - t_prompt, D_MODEL), F32)],
        compiler_params=_cparams(("arbitrary",)),
        name="combine",
    )(dest, dest, x1, meta, ys)


def _prep_weights(norm_mix, w_in, g_q, g_k, w_pool, pool_scale, w_out, norm_ffn, w_group, w_expert):
    lane = jnp.arange(ATTN_WIDTH)
    hsum = (lane[:, None] // HEAD_DIM == lane[None, :] // HEAD_DIM).astype(BF16)
    w_router = jnp.zeros((D_MODEL, LANES), F32)
    w_router = w_router.at[:, 0:N_GROUPS].set(w_group).at[:, N_GROUPS:N_GROUPS + N_EXPERTS].set(w_expert)
    w_router_hi = w_router.astype(BF16)
    w_router_lo = (w_router - w_router_hi.astype(F32)).astype(BF16)
    r = jnp.arange(MERGE_ROWS)
    return {
        "nrm": norm_mix.reshape(1, D_MODEL),
        "w_in": w_in.astype(BF16),
        "gq": (jnp.tile(g_q, N_HEADS) * (HEAD_DIM ** -0.5 * LOG2E)).reshape(1, ATTN_WIDTH),
        "gk": jnp.tile(g_k, N_HEADS).reshape(1, ATTN_WIDTH),
        "hsum": hsum,
        "w_pool": w_pool.astype(BF16),
        "pool_scale": pool_scale.reshape(1, POOL_WIDTH),
        "w_out_pool": w_out[:POOL_WIDTH].astype(BF16),
        "w_out_attn": w_out[POOL_WIDTH:].astype(BF16),
        "nrm_ffn": norm_ffn.reshape(1, D_MODEL),
        "w_router_hi": w_router_hi,
        "w_router_lo": w_router_lo,
        "tri_merge": (r[None, :] < r[:, None]).astype(BF16),
    }


def _later_sums(n_keys, n_heads):
    idx = jnp.arange(n_heads * n_keys)
    same = idx[:, None] // n_keys == idx[None, :] // n_keys
    later = same & (idx[:, None] > idx[None, :])
    return jnp.concatenate([later, same], axis=1).astype(BF16)


def kernel(x_prompt, x_sample, cache_k, cache_v, state_pool, page_table, meta_tokens, norm_mix, w_in, g_q, g_k,
           sb_bias, w_pool, pool_scale, w_out, norm_ffn, w_group, w_expert, w_gate, w_up, w_down):
    nb, seq, _ = x_prompt.shape
    nseq, nt, _ = x_sample.shape
    n_phys = cache_k.shape[1]
    t_prompt = nb * seq
    t_sample = nseq * nt
    t_all = t_prompt + t_sample

    cw = _prep_weights(norm_mix[0], w_in[0], g_q[0], g_k[0], w_pool[0], pool_scale[0], w_out[0], norm_ffn[0],
                       w_group[0], w_expert[0])
    bias = sb_bias[0].astype(F32) * LOG2E

    zero_hist = jnp.zeros((HIST_ROWS, POOL_WIDTH), F32)
    _, k_m, v_m, kx_m, vx_m, _, tail_m = _proj_seq(meta_tokens[None], zero_hist, cw, tm=N_META, p0=0,
                                                   full_count=False)
    pad_rows = ((0, ATT_BLOCK - N_META), (0, 0))
    kxm = jnp.pad(kx_m[0], pad_rows)
    vxm = jnp.pad(vx_m[0], pad_rows)

    q_p, k_p, v_p, kx_p, vx_p, po_p, tail_p = _proj_seq(x_prompt, tail_m[0], cw, tm=PROJ_ROWS, p0=N_META,
                                                        full_count=True)
    head_bias = jnp.repeat(bias, ATT_BLOCK)
    col = jnp.arange(N_HEADS * ATT_BLOCK) % ATT_BLOCK
    bias_rows = jnp.stack([head_bias, jnp.where(col < N_META, head_bias, NEG)])
    attn_p = _attn_prompt(q_p, kx_p, vx_p, kxm, vxm, bias_rows, _later_sums(ATT_BLOCK, 2))

    x_s_tm = jnp.transpose(x_sample, (1, 0, 2)).reshape(t_sample, D_MODEL)
    hist_tm = jnp.transpose(state_pool[0], (1, 0, 2))
    q_s, k_s, v_s, po_s, tail_s = _proj_step(x_s_tm, hist_tm, cw, nseq=nseq, nt=nt)
    to_seq = lambda a: jnp.transpose(a.reshape(nt, nseq, a.shape[-1]), (1, 0, 2))
    q_sb, k_sb, v_sb, po_sb = to_seq(q_s), to_seq(k_s), to_seq(v_s), to_seq(po_s)
    head_of_lane = jnp.arange(ATTN_WIDTH) // HEAD_DIM
    head_mask = head_of_lane[None, :] == jnp.arange(N_HEADS)[:, None]
    qbd = jnp.where(head_mask[None, None], q_sb[:, :, None, :], 0).reshape(nseq, nt * N_HEADS, ATTN_WIDTH)
    pad_new = ((0, 0), (0, 8 - nt), (0, 0))
    kn = jnp.pad(k_sb, pad_new)
    vn = jnp.pad(v_sb, pad_new)
    bias_col = jnp.tile(bias, nt).reshape(nt * N_HEADS, 1)
    page_t = lambda c: jnp.transpose(c[0], (0, 2, 3, 1)).reshape(n_phys, ATTN_WIDTH, PAGE)
    attn_s = _attn_sample(page_table, qbd, kn, vn, bias_col, _later_sums(PAGE, 1), page_t(cache_k), page_t(cache_v))

    x1, h_all, meta, counts_f = _merge_router(
        x_prompt.reshape(t_prompt, D_MODEL), x_sample.reshape(t_sample, D_MODEL),
        po_p.reshape(t_prompt, POOL_WIDTH), po_sb.reshape(t_sample, POOL_WIDTH),
        attn_p.reshape(t_prompt, ATTN_WIDTH), attn_s.reshape(t_sample, ATTN_WIDTH).astype(BF16), cw)

    pair_exp = meta[:, 0:2].astype(jnp.int32)
    pair_rank = meta[:, 2:4].astype(jnp.int32)
    counts = counts_f[0, :N_EXPERTS].astype(jnp.int32)
    rows = EXPERT_ROWS
    padded = (counts + rows - 1) // rows * rows
    pad_end = jnp.cumsum(padded)
    pad_start = pad_end - padded
    nblk = -(-(2 * t_all) // rows) + N_EXPERTS
    of_expert = pair_exp[..., None] == jnp.arange(N_EXPERTS, dtype=jnp.int32)
    dest = jnp.sum(jnp.where(of_expert, pad_start, 0), axis=-1) + pair_rank
    blk_row0 = jnp.arange(nblk, dtype=jnp.int32) * rows
    blk_exp = jnp.minimum(jnp.sum(blk_row0[:, None] >= pad_end[None, :], axis=1), N_EXPERTS - 1).astype(jnp.int32)
    n_rows = jnp.clip(counts[blk_exp] - (blk_row0 - pad_start[blk_exp]), 0, rows)
    n_rows = jnp.where(blk_row0 < pad_end[-1], n_rows, 0).astype(jnp.int32)
    n_tiled = (n_rows + ROW_GROUP - 1) // ROW_GROUP * ROW_GROUP
    fills = (blk_row0 + n_rows, n_tiled - n_rows, (blk_row0 + n_tiled) // ROW_GROUP, (rows - n_tiled) // ROW_GROUP)

    hs = _dispatch(fills, dest, h_all, nblk * rows)
    ys = _experts(blk_exp, n_tiled, hs, w_gate[0], w_up[0], w_down[0])
    y_prompt, y_sample = _combine(x1, ys, meta, dest, t_prompt)
    y_prompt = y_prompt.reshape(nb, seq, D_MODEL)
    y_sample = y_sample.reshape(nseq, nt, D_MODEL)

    heads = lambda a: a.reshape(a.shape[:-1] + (N_HEADS, HEAD_DIM))
    k_prompt = jnp.concatenate([jnp.broadcast_to(k_m, (nb, N_META, ATTN_WIDTH)), k_p], axis=1)
    v_prompt = jnp.concatenate([jnp.broadcast_to(v_m, (nb, N_META, ATTN_WIDTH)), v_p], axis=1)
    pool_sample = jnp.transpose(tail_s, (1, 0, 2))
    return (y_prompt, y_sample, heads(k_prompt)[None], heads(v_prompt)[None], tail_p[:, 1:][None],
            heads(k_sb)[None], heads(v_sb)[None], pool_sample[None])
```

```python
import functools

import jax
import jax.numpy as jnp
from jax import lax
from jax.experimental import pallas as pl
from jax.experimental.pallas import tpu as pltpu

F32 = jnp.float32
BF16 = jnp.bfloat16

D_MODEL = 1024
POOL_WIDTH = 512
ATTN_WIDTH = 512
HEAD_DIM = 64
N_HEADS = 8
N_PAIRS = N_HEADS // 2
POOL_WINDOWS = (2, 4, 8, 16)
POOL_GROUP = 128
POOL_HIST = 15
HIST_ROWS = 16
N_META = 16
PAGE = 128
N_GROUPS = 4
EXPERTS_PER_GROUP = 8
N_EXPERTS = 32
D_EXPERT = 512
EPS = 1e-6
NEG = -1e30
LOG2E = 1.4426950408889634
LOGIT_CAP = 100.0
LANES = 128
ATT_BLOCK = 128
PROJ_ROWS = 512
MERGE_ROWS = 256
EXPERT_ROWS = 256
ROW_GROUP = 8
VMEM_LIMIT = 56 * 1024 * 1024


def _cparams(sem, vmem=VMEM_LIMIT):
    return pltpu.CompilerParams(dimension_semantics=sem, vmem_limit_bytes=vmem)


def _rms_rows(x, g):
    ms = jnp.mean(x * x, axis=-1, keepdims=True)
    return (x * lax.rsqrt(ms + EPS)) * g


def _head_norm(a, g, hsum):
    sq = a * a
    hi = sq.astype(BF16)
    lo = (sq - hi.astype(F32)).astype(BF16)
    ssum = (jnp.dot(hi, hsum, preferred_element_type=F32)
            + jnp.dot(lo, hsum, preferred_element_type=F32))
    return (a * lax.rsqrt(ssum * (1.0 / HEAD_DIM) + EPS)) * g


def _project(x, nrm, w_in, gq, gk, hsum):
    n = _rms_rows(x, nrm)
    proj = jnp.dot(n.astype(BF16), w_in, preferred_element_type=F32)
    u = proj[:, 0:POOL_WIDTH]
    q = _head_norm(proj[:, POOL_WIDTH:POOL_WIDTH + ATTN_WIDTH], gq, hsum)
    k = _head_norm(proj[:, POOL_WIDTH + ATTN_WIDTH:POOL_WIDTH + 2 * ATTN_WIDTH], gk, hsum)
    v = proj[:, POOL_WIDTH + 2 * ATTN_WIDTH:]
    return u, q, k, v


def _split_heads(a, out_ref):
    rows = a.shape[0]
    lane = lax.broadcasted_iota(jnp.int32, (rows, LANES), 1)
    first = lane < HEAD_DIM
    for p in range(N_PAIRS):
        a2 = a[:, p * LANES:(p + 1) * LANES]
        out_ref[:, 2 * p * LANES:(2 * p + 1) * LANES] = jnp.where(first, a2, 0.0).astype(out_ref.dtype)
        out_ref[:, (2 * p + 1) * LANES:(2 * p + 2) * LANES] = jnp.where(first, 0.0, a2).astype(out_ref.dtype)


def _log2_gates(z):
    z = jnp.minimum(z, LOGIT_CAP)
    drop = jnp.log(1.0 + jnp.exp2(z)) * LOG2E
    return z - drop, drop


def _stick_step(tt, gate_in, mix_in):
    def stack_heads(b):
        half = b.shape[1] // 2
        return jnp.concatenate([b[:, :half], b[:, half:]], axis=0)

    nt_dims = (((1,), (1,)), ((), ()))
    gates = mixed = None
    if mix_in is not None:
        log_betas, drops, carries, vxbs = mix_in
        sums = [jnp.dot(drop, tt, preferred_element_type=F32) for drop in drops]
    if gate_in is not None:
        q2s, kxbs, brows, mask = gate_in
        zs = [lax.dot_general(q2, stack_heads(kxb), nt_dims, preferred_element_type=F32) + brow
              for q2, kxb, brow in zip(q2s, kxbs, brows)]
        if mask is not None:
            zs = [jnp.where(mask, z, NEG) for z in zs]
    if mix_in is not None:
        nk = sums[0].shape[1] // 2
        probs = [jnp.exp2(lb - s[:, :nk] - carry).astype(BF16) for lb, s, carry in zip(log_betas, sums, carries)]
    if gate_in is not None:
        gates = [(lb, lk.astype(BF16)) for lb, lk in (_log2_gates(z) for z in zs)]
    if mix_in is not None:
        outs = [jnp.dot(a, stack_heads(vxb), preferred_element_type=F32) for a, vxb in zip(probs, vxbs)]
        mixed = [(out, carry + s[:, nk:]) for out, carry, s in zip(outs, carries, sums)]
    return gates, mixed


def _proj_seq_kernel(x_ref, hist_ref, nrm_ref, win_ref, gq_ref, gk_ref, hsum_ref, wpool_ref, ps_ref,
                     q_ref, k_ref, v_ref, kx_ref, vx_ref, po_ref, tail_ref, ext_ref, *, tm, p0, full_count):
    blk = pl.program_id(1)
    u, q, k, v = _project(x_ref[...], nrm_ref[...], win_ref[...], gq_ref[...], gk_ref[...], hsum_ref[...])
    q_ref[...] = q.astype(q_ref.dtype)
    k_ref[...] = k
    v_ref[...] = v
    _split_heads(k, kx_ref)
    _split_heads(v, vx_ref)

    @pl.when(blk == 0)
    def _():
        ext_ref[0:HIST_ROWS, :] = hist_ref[...]

    ext_ref[HIST_ROWS:HIST_ROWS + tm, :] = u
    for g, w in enumerate(POOL_WINDOWS):
        e = ext_ref[:, g * POOL_GROUP:(g + 1) * POOL_GROUP]
        s = e
        sh = 1
        while sh < w:
            s = s + pltpu.roll(s, sh, axis=0)
            sh *= 2
        win = s[HIST_ROWS:, :]
        tok = e[HIST_ROWS:, :]
        if full_count:
            d = win * (1.0 / w) - tok
        else:
            pos = p0 + lax.broadcasted_iota(jnp.int32, (tm, 1), 0)
            cnt = jnp.minimum(w, pos + 1).astype(F32)
            d = win / cnt - tok
        y = jnp.dot(d.astype(BF16), wpool_ref[g], preferred_element_type=F32)
        y = y * ps_ref[:, g * POOL_GROUP:(g + 1) * POOL_GROUP]
        po_ref[:, g * POOL_GROUP:(g + 1) * POOL_GROUP] = y.astype(po_ref.dtype)
    last = ext_ref[tm:tm + HIST_ROWS, :]
    ext_ref[0:HIST_ROWS, :] = last

    @pl.when(blk == pl.num_programs(1) - 1)
    def _():
        tail_ref[...] = last


def _proj_seq(x, hist, cw, *, tm, p0, full_count):
    b, l, _ = x.shape
    nb = l // tm
    const = lambda shape: pl.BlockSpec(shape, lambda i, j: (0,) * len(shape))
    row = lambda width: pl.BlockSpec((None, tm, width), lambda i, j: (i, j, 0))
    kern = functools.partial(_proj_seq_kernel, tm=tm, p0=p0, full_count=full_count)
    return pl.pallas_call(
        kern,
        grid=(b, nb),
        in_specs=[row(D_MODEL), const((HIST_ROWS, POOL_WIDTH)), const((1, D_MODEL)),
                  const((D_MODEL, 2 * D_MODEL)), const((1, ATTN_WIDTH)), const((1, ATTN_WIDTH)),
                  const((ATTN_WIDTH, ATTN_WIDTH)), const((4, POOL_GROUP, POOL_GROUP)), const((1, POOL_WIDTH))],
        out_specs=[row(ATTN_WIDTH), row(ATTN_WIDTH), row(ATTN_WIDTH), row(2 * ATTN_WIDTH), row(2 * ATTN_WIDTH),
                   row(POOL_WIDTH), pl.BlockSpec((None, HIST_ROWS, POOL_WIDTH), lambda i, j: (i, 0, 0))],
        out_shape=[jax.ShapeDtypeStruct((b, l, ATTN_WIDTH), BF16),
                   jax.ShapeDtypeStruct((b, l, ATTN_WIDTH), F32),
                   jax.ShapeDtypeStruct((b, l, ATTN_WIDTH), F32),
                   jax.ShapeDtypeStruct((b, l, 2 * ATTN_WIDTH), BF16),
                   jax.ShapeDtypeStruct((b, l, 2 * ATTN_WIDTH), BF16),
                   jax.ShapeDtypeStruct((b, l, POOL_WIDTH), BF16),
                   jax.ShapeDtypeStruct((b, HIST_ROWS, POOL_WIDTH), F32)],
        scratch_shapes=[pltpu.VMEM((HIST_ROWS + tm, POOL_WIDTH), F32)],
        compiler_params=_cparams(("arbitrary", "arbitrary")),
        name="proj_seq",
    )(x, hist, cw["nrm"], cw["w_in"], cw["gq"], cw["gk"], cw["hsum"], cw["w_pool"], cw["pool_scale"])


def _proj_step_kernel(x_ref, hist_ref, nrm_ref, win_ref, gq_ref, gk_ref, hsum_ref, wpool_ref, ps_ref,
                      q_ref, k_ref, v_ref, po_ref, tail_ref, *, nseq, nt):
    u, q, k, v = _project(x_ref[...], nrm_ref[...], win_ref[...], gq_ref[...], gk_ref[...], hsum_ref[...])
    q_ref[...] = q.astype(q_ref.dtype)
    k_ref[...] = k
    v_ref[...] = v
    ext = [hist_ref[r] for r in range(POOL_HIST)] + [u[t * nseq:(t + 1) * nseq, :] for t in range(nt)]
    for r in range(POOL_HIST):
        tail_ref[r] = ext[nt + r]
    for g, w in enumerate(POOL_WINDOWS):
        sl = slice(g * POOL_GROUP, (g + 1) * POOL_GROUP)
        ds = []
        for t in range(nt):
            win = ext[POOL_HIST + t][:, sl]
            for i in range(1, w):
                win = win + ext[POOL_HIST + t - i][:, sl]
            ds.append(win * (1.0 / w) - ext[POOL_HIST + t][:, sl])
        d = jnp.concatenate(ds, axis=0)
        y = jnp.dot(d.astype(BF16), wpool_ref[g], preferred_element_type=F32) * ps_ref[:, sl]
        po_ref[:, sl] = y.astype(po_ref.dtype)


def _proj_step(x_tm, hist_tm, cw, *, nseq, nt):
    rows = nseq * nt
    kern = functools.partial(_proj_step_kernel, nseq=nseq, nt=nt)
    return pl.pallas_call(
        kern,
        out_shape=[jax.ShapeDtypeStruct((rows, ATTN_WIDTH), BF16),
                   jax.ShapeDtypeStruct((rows, ATTN_WIDTH), F32),
                   jax.ShapeDtypeStruct((rows, ATTN_WIDTH), F32),
                   jax.ShapeDtypeStruct((rows, POOL_WIDTH), BF16),
                   jax.ShapeDtypeStruct((POOL_HIST, nseq, POOL_WIDTH), F32)],
        compiler_params=pltpu.CompilerParams(vmem_limit_bytes=VMEM_LIMIT),
        name="proj_step",
    )(x_tm, hist_tm, cw["nrm"], cw["w_in"], cw["gq"], cw["gk"], cw["hsum"], cw["w_pool"], cw["pool_scale"])


def _attn_prompt_kernel(q_ref, kx_ref, vx_ref, kxm_ref, vxm_ref, bias_ref, tt_ref, o_ref,
                        carry_ref, acc_ref, lb_ref, lk_ref):
    qi = pl.program_id(1)
    tt = tt_ref[...]
    nq = ATT_BLOCK
    pairs = range(N_PAIRS)
    lanes = [slice(p * LANES, (p + 1) * LANES) for p in pairs]
    wide = [slice(2 * p * LANES, 2 * (p + 1) * LANES) for p in pairs]

    def rows(j):
        return pl.ds(pl.multiple_of(j * ATT_BLOCK, ATT_BLOCK), ATT_BLOCK)

    def gate_in(k_of, bias_row, mask):
        return ([q_ref[:, l] for l in lanes], [k_of(w) for w in wide],
                [bias_ref[bias_row:bias_row + 1, w] for w in wide], mask)

    def pending():
        return [lb_ref[p] for p in pairs], [lk_ref[p] for p in pairs]

    def keep(gates):
        lbs, drops = gates
        for p in pairs:
            lb_ref[p] = lbs[p]
            lk_ref[p] = drops[p]

    def step(gin, gates_prev, v_of=None, last=False):
        min_ = accs = None
        if gates_prev is not None:
            lbs, drops = gates_prev
            min_ = (lbs, drops, [carry_ref[p] for p in pairs], [v_of(w) for w in wide])
            accs = [acc_ref[:, l] for l in lanes]
        gates, mixed = _stick_step(tt, gin, min_)
        if mixed is not None:
            for p, (out, carry) in enumerate(mixed):
                if last:
                    o_ref[:, lanes[p]] = (accs[p] + out).astype(o_ref.dtype)
                else:
                    carry_ref[p] = carry
                    acc_ref[:, lanes[p]] = accs[p] + out
        return None if gates is None else ([g[0] for g in gates], [g[1] for g in gates])

    def token_tile(j):
        return gate_in(lambda w: kx_ref[rows(j), w], 0, None)

    carry_ref[...] = jnp.zeros_like(carry_ref)
    acc_ref[...] = jnp.zeros_like(acc_ref)
    r_i = lax.broadcasted_iota(jnp.int32, (nq, 2 * ATT_BLOCK), 0)
    c_i = lax.broadcasted_iota(jnp.int32, (nq, 2 * ATT_BLOCK), 1) & (ATT_BLOCK - 1)
    keep(step(gate_in(lambda w: kx_ref[rows(qi), w], 0, c_i < r_i), None))

    def two_tiles(t, _):
        j = qi - 1 - 2 * t
        g = step(token_tile(j), pending(), lambda w: vx_ref[rows(j + 1), w])
        keep(step(token_tile(j - 1), g, lambda w: vx_ref[rows(j), w]))
        return 0

    lax.fori_loop(0, lax.shift_right_logical(qi, 1), two_tiles, 0)

    @pl.when((qi & 1) == 1)
    def _():
        keep(step(token_tile(0), pending(), lambda w: vx_ref[rows(1), w]))

    g = step(gate_in(lambda w: kxm_ref[:, w], 1, None), pending(), lambda w: vx_ref[0:ATT_BLOCK, w])
    step(None, g, lambda w: vxm_ref[:, w], last=True)


def _attn_prompt(q, kx, vx, kxm, vxm, bias_rows, tt):
    b, l, _ = q.shape
    nq = l // ATT_BLOCK
    once = pl.Buffered(1)
    return pl.pallas_call(
        _attn_prompt_kernel,
        grid=(b, nq),
        in_specs=[pl.BlockSpec((None, ATT_BLOCK, ATTN_WIDTH), lambda i, j: (i, j, 0)),
                  pl.BlockSpec((None, l, 2 * ATTN_WIDTH), lambda i, j: (i, 0, 0), pipeline_mode=once),
                  pl.BlockSpec((None, l, 2 * ATTN_WIDTH), lambda i, j: (i, 0, 0), pipeline_mode=once),
                  pl.BlockSpec((ATT_BLOCK, 2 * ATTN_WIDTH), lambda i, j: (0, 0)),
                  pl.BlockSpec((ATT_BLOCK, 2 * ATTN_WIDTH), lambda i, j: (0, 0)),
                  pl.BlockSpec((2, 2 * ATTN_WIDTH), lambda i, j: (0, 0)),
                  pl.BlockSpec((2 * ATT_BLOCK, 4 * ATT_BLOCK), lambda i, j: (0, 0))],
        out_specs=pl.BlockSpec((None, ATT_BLOCK, ATTN_WIDTH), lambda i, j: (i, j, 0)),
        out_shape=jax.ShapeDtypeStruct((b, l, ATTN_WIDTH), BF16),
        scratch_shapes=[pltpu.VMEM((N_PAIRS, ATT_BLOCK, 2 * ATT_BLOCK), F32),
                        pltpu.VMEM((ATT_BLOCK, ATTN_WIDTH), F32),
                        pltpu.VMEM((N_PAIRS, ATT_BLOCK, 2 * ATT_BLOCK), F32),
                        pltpu.VMEM((N_PAIRS, ATT_BLOCK, 2 * ATT_BLOCK), BF16)],
        compiler_params=_cparams(("arbitrary", "arbitrary")),
        name="attn_prompt",
    )(q, kx, vx, kxm, vxm, bias_rows, tt)


def _attn_sample_kernel(pt_ref, qbd_ref, kn_ref, vn_ref, bias_ref, tt_ref, *refs, n_pages, nt):
    del pt_ref
    k_refs = refs[:n_pages]
    v_refs = refs[n_pages:2 * n_pages]
    o_ref = refs[2 * n_pages]
    qbd = qbd_ref[...]
    tt = tt_ref[...]
    bias = bias_ref[...]
    nc = qbd.shape[0]
    nn = (((1,), (0,)), ((), ()))
    nt_dims = (((1,), (1,)), ((), ()))

    pad = jnp.zeros((PAGE - kn_ref.shape[0], ATTN_WIDTH), F32)
    kn = jnp.concatenate([kn_ref[...], pad], axis=0).astype(BF16)
    vn = jnp.concatenate([vn_ref[...], pad], axis=0).astype(BF16)
    t_i = lax.broadcasted_iota(jnp.int32, (nc, PAGE), 0) >> 3
    k_i = lax.broadcasted_iota(jnp.int32, (nc, PAGE), 1)
    z_new = lax.dot_general(qbd, kn, nt_dims, preferred_element_type=F32) + bias
    zs = [jnp.where(k_i < t_i, z_new, NEG)]
    for j in range(n_pages - 1, -1, -1):
        zs.append(lax.dot_general(qbd, k_refs[j][...].astype(BF16), nn, preferred_element_type=F32) + bias)
    gates = [_log2_gates(z) for z in zs]
    sums = [jnp.dot(lk.astype(BF16), tt, preferred_element_type=F32) for _, lk in gates]
    carry = jnp.zeros((nc, PAGE), F32)
    acc = None
    for i, ((lb, _), s) in enumerate(zip(gates, sums)):
        a = jnp.exp2(lb - s[:, :PAGE] - carry).astype(BF16)
        carry = carry + s[:, PAGE:]
        if i == 0:
            acc = jnp.dot(a, vn, preferred_element_type=F32)
        else:
            vt = v_refs[n_pages - i][...].astype(BF16)
            acc = acc + lax.dot_general(a, vt, nt_dims, preferred_element_type=F32)
    h_i = lax.broadcasted_iota(jnp.int32, (nc, ATTN_WIDTH), 0) & (N_HEADS - 1)
    l_i = lax.broadcasted_iota(jnp.int32, (nc, ATTN_WIDTH), 1) >> 6
    acc = jnp.where(h_i == l_i, acc, 0.0)
    for t in range(nt):
        o_ref[t:t + 1, :] = jnp.sum(acc[t * N_HEADS:(t + 1) * N_HEADS, :], axis=0, keepdims=True)


def _attn_sample(page_table, qbd, kn, vn, bias_col, tt, cache_k, cache_v):
    nseq, n_pages = page_table.shape
    nt = qbd.shape[1] // N_HEADS
    pt = page_table.reshape(-1)

    def page_spec(j):
        return pl.BlockSpec((None, ATTN_WIDTH, PAGE), lambda i, pt_ref: (pt_ref[i * n_pages + j], 0, 0))

    grid_spec = pltpu.PrefetchScalarGridSpec(
        num_scalar_prefetch=1,
        grid=(nseq,),
        in_specs=[pl.BlockSpec((None, nt * N_HEADS, ATTN_WIDTH), lambda i, pt_ref: (i, 0, 0)),
                  pl.BlockSpec((None, 8, ATTN_WIDTH), lambda i, pt_ref: (i, 0, 0)),
                  pl.BlockSpec((None, 8, ATTN_WIDTH), lambda i, pt_ref: (i, 0, 0)),
                  pl.BlockSpec((nt * N_HEADS, 1), lambda i, pt_ref: (0, 0)),
                  pl.BlockSpec((PAGE, 2 * PAGE), lambda i, pt_ref: (0, 0))]
                 + [page_spec(j) for j in range(n_pages)] + [page_spec(j) for j in range(n_pages)],
        out_specs=pl.BlockSpec((None, nt, ATTN_WIDTH), lambda i, pt_ref: (i, 0, 0)),
    )
    kern = functools.partial(_attn_sample_kernel, n_pages=n_pages, nt=nt)
    return pl.pallas_call(
        kern,
        grid_spec=grid_spec,
        out_shape=jax.ShapeDtypeStruct((nseq, nt, ATTN_WIDTH), F32),
        compiler_params=_cparams(("arbitrary",)),
        name="attn_sample",
    )(pt, qbd, kn, vn, bias_col, tt, *([cache_k] * n_pages), *([cache_v] * n_pages))


def _merge_router_kernel(xp_ref, xs_ref, pop_ref, pos_ref, atp_ref, ats_ref, wop_ref, woa_ref, nrm_ref, wrh_ref,
                         wrl_ref, tri_ref, x1_ref, h_ref, meta_ref, cout_ref, carry_ref, *, nb_prompt):
    i = pl.program_id(0)

    @pl.when(i == 0)
    def _():
        carry_ref[...] = jnp.zeros_like(carry_ref)

    is_p = i < nb_prompt
    x = jnp.where(is_p, xp_ref[...], xs_ref[...])
    po = jnp.where(is_p, pop_ref[...], pos_ref[...])
    at = jnp.where(is_p, atp_ref[...], ats_ref[...])
    y = (jnp.dot(po, wop_ref[...], preferred_element_type=F32)
         + jnp.dot(at, woa_ref[...], preferred_element_type=F32))
    x1 = x + y
    x1_ref[...] = x1
    h = _rms_rows(x1, nrm_ref[...])
    h_ref[...] = h
    hi = h.astype(BF16)
    lo = (h - hi.astype(F32)).astype(BF16)
    logits = (jnp.dot(hi, wrh_ref[...], preferred_element_type=F32)
              + jnp.dot(lo, wrh_ref[...], preferred_element_type=F32)
              + jnp.dot(hi, wrl_ref[...], preferred_element_type=F32))
    tm = logits.shape[0]
    lane = lax.broadcasted_iota(jnp.int32, (tm, LANES), 1).astype(F32)
    ninf = -jnp.inf
    is_g = lane < N_GROUPS
    gl = jnp.where(is_g, logits, ninf)
    gmax = jnp.max(gl, axis=-1, keepdims=True)
    g_sel = jnp.min(jnp.where(gl == gmax, lane, float(LANES)), axis=-1, keepdims=True)
    psum = jnp.sum(jnp.where(is_g, jnp.exp(logits - gmax), 0.0), axis=-1, keepdims=True)
    g_w = 1.0 / psum
    e_lo = N_GROUPS + g_sel * EXPERTS_PER_GROUP
    in_group = (lane >= e_lo) & (lane < e_lo + EXPERTS_PER_GROUP)
    el = jnp.where(in_group, logits, ninf)
    v1 = jnp.max(el, axis=-1, keepdims=True)
    i1 = jnp.min(jnp.where(el == v1, lane, float(LANES)), axis=-1, keepdims=True)
    el2 = jnp.where(lane == i1, ninf, el)
    v2 = jnp.max(el2, axis=-1, keepdims=True)
    i2 = jnp.min(jnp.where(el2 == v2, lane, float(LANES)), axis=-1, keepdims=True)
    e21 = jnp.exp(v2 - v1)
    den = 1.0 + e21
    gate1 = g_w * (1.0 / den)
    gate2 = g_w * (e21 / den)
    e1 = i1 - N_GROUPS
    e2 = i2 - N_GROUPS
    onehot = jnp.where((lane == e1) | (lane == e2), 1.0, 0.0)
    before = jnp.dot(tri_ref[...], onehot.astype(BF16), preferred_element_type=F32) + carry_ref[...]
    rank1 = jnp.sum(jnp.where(lane == e1, before, 0.0), axis=-1, keepdims=True)
    rank2 = jnp.sum(jnp.where(lane == e2, before, 0.0), axis=-1, keepdims=True)
    carry = carry_ref[...] + jnp.sum(onehot, axis=0, keepdims=True)
    carry_ref[...] = carry
    cout_ref[...] = carry
    meta = jnp.where(lane == 0, e1, 0.0)
    meta = jnp.where(lane == 1, e2, meta)
    meta = jnp.where(lane == 2, rank1, meta)
    meta = jnp.where(lane == 3, rank2, meta)
    meta = jnp.where(lane == 4, gate1, meta)
    meta = jnp.where(lane == 5, gate2, meta)
    meta_ref[...] = meta


def _merge_router(x_p, x_s, po_p, po_s, at_p, at_s, cw):
    tm = MERGE_ROWS
    nbp = x_p.shape[0] // tm
    nbs = x_s.shape[0] // tm
    const = lambda shape: pl.BlockSpec(shape, lambda i: (0,) * len(shape))
    prow = lambda width: pl.BlockSpec((tm, width), lambda i: (jnp.minimum(i, nbp - 1), 0))
    srow = lambda width: pl.BlockSpec((tm, width), lambda i: (jnp.maximum(i - nbp, 0), 0))
    row = lambda width: pl.BlockSpec((tm, width), lambda i: (i, 0))
    t_all = (nbp + nbs) * tm
    return pl.pallas_call(
        functools.partial(_merge_router_kernel, nb_prompt=nbp),
        grid=(nbp + nbs,),
        in_specs=[prow(D_MODEL), srow(D_MODEL), prow(POOL_WIDTH), srow(POOL_WIDTH), prow(ATTN_WIDTH),
                  srow(ATTN_WIDTH), const((POOL_WIDTH, D_MODEL)), const((ATTN_WIDTH, D_MODEL)),
                  const((1, D_MODEL)), const((D_MODEL, LANES)), const((D_MODEL, LANES)), const((tm, tm))],
        out_specs=[row(D_MODEL), row(D_MODEL), row(LANES), const((1, LANES))],
        out_shape=[jax.ShapeDtypeStruct((t_all, D_MODEL), F32),
                   jax.ShapeDtypeStruct((t_all, D_MODEL), F32),
                   jax.ShapeDtypeStruct((t_all, LANES), F32),
                   jax.ShapeDtypeStruct((1, LANES), F32)],
        scratch_shapes=[pltpu.VMEM((1, LANES), F32)],
        compiler_params=_cparams(("arbitrary",)),
        name="merge_router",
    )(x_p, x_s, po_p, po_s, at_p, at_s, cw["w_out_pool"], cw["w_out_attn"], cw["nrm_ffn"], cw["w_router_hi"],
      cw["w_router_lo"], cw["tri_merge"])


def _dispatch_kernel(fill_r0_ref, fill_nr_ref, fill_g0_ref, fill_ng_ref, dest_ref, h_ref, hs_hbm, zbuf, sem):
    i = pl.program_id(0)
    tm = h_ref.shape[0]

    def row_of(r):
        return hs_hbm.at[r >> 3, pl.ds(r & (ROW_GROUP - 1), 1), :]

    @pl.when(i == 0)
    def _():
        zbuf[...] = jnp.zeros_like(zbuf)

        def fill(wait):
            def body(blk, _):
                for k in range(ROW_GROUP - 1):
                    @pl.when(k < fill_nr_ref[blk])
                    def _():
                        cp = pltpu.make_async_copy(zbuf.at[0, pl.ds(0, 1), :], row_of(fill_r0_ref[blk] + k), sem)
                        cp.wait() if wait else cp.start()

                ng = fill_ng_ref[blk]

                @pl.when(ng > 0)
                def _():
                    cp = pltpu.make_async_copy(zbuf.at[pl.ds(0, ng)], hs_hbm.at[pl.ds(fill_g0_ref[blk], ng)], sem)
                    cp.wait() if wait else cp.start()
                return 0

            lax.fori_loop(0, fill_r0_ref.shape[0], body, 0)

        fill(False)
        fill(True)

    def rows8(g, _):
        for k in range(ROW_GROUP):
            r = g * ROW_GROUP + k
            for e in range(2):
                pltpu.make_async_copy(h_ref.at[pl.ds(r, 1), :], row_of(dest_ref[0, 0, 2 * r + e]),
                                      sem).start(priority=e)
        return 0

    lax.fori_loop(0, tm // ROW_GROUP, rows8, 0)
    for _ in range(2):
        pltpu.make_async_copy(zbuf, hs_hbm.at[pl.ds(0, tm // ROW_GROUP)], sem).wait()


def _dispatch(fills, dest, h_all, n_sorted):
    t_all = h_all.shape[0]
    tm = EXPERT_ROWS
    grid_spec = pltpu.PrefetchScalarGridSpec(
        num_scalar_prefetch=4,
        grid=(t_all // tm,),
        in_specs=[pl.BlockSpec((1, 1, 2 * tm), lambda i, *_: (i, 0, 0), memory_space=pltpu.SMEM),
                  pl.BlockSpec((tm, D_MODEL), lambda i, *_: (i, 0))],
        out_specs=pl.BlockSpec(memory_space=pl.ANY),
        scratch_shapes=[pltpu.VMEM((tm // ROW_GROUP, ROW_GROUP, D_MODEL), F32), pltpu.SemaphoreType.DMA(())],
    )
    return pl.pallas_call(
        _dispatch_kernel,
        grid_spec=grid_spec,
        out_shape=jax.ShapeDtypeStruct((n_sorted // ROW_GROUP, ROW_GROUP, D_MODEL), F32),
        compiler_params=_cparams(("arbitrary",)),
        name="dispatch",
    )(*fills, dest.reshape(t_all // tm, 1, 2 * tm), h_all)


def _expert_kernel(bexp_ref, nval_ref, hs_hbm, wg_ref, wu_ref, wd_ref, ys_ref, xbuf, wgb, wub, wdb, sem):
    i = pl.program_id(0)
    last = pl.num_programs(0) - 1
    slot = i & 1
    groups = xbuf.shape[1]
    g_cur = lax.shift_right_logical(nval_ref[i], 3)
    g_next = lax.shift_right_logical(nval_ref[jnp.minimum(i + 1, last)], 3)

    def fetch(blk, s, ng):
        return pltpu.make_async_copy(hs_hbm.at[pl.ds(blk * groups, ng)], xbuf.at[s, pl.ds(0, ng)], sem.at[s])

    @pl.when(i == 0)
    def _():
        xbuf[...] = jnp.zeros_like(xbuf)

        @pl.when(g_cur > 0)
        def _():
            fetch(0, 0, g_cur).start()

    @pl.when((i < last) & (g_next > 0))
    def _():
        fetch(i + 1, 1 - slot, g_next).start()

    @pl.when(g_cur > 0)
    def _():
        changed = jnp.logical_or(i == 0, bexp_ref[i] != bexp_ref[jnp.maximum(i - 1, 0)])

        @pl.when(changed)
        def _():
            wgb[...] = wg_ref[...].astype(BF16)
            wub[...] = wu_ref[...].astype(BF16)
            wdb[...] = wd_ref[...].astype(BF16)

        fetch(i, slot, g_cur).wait()
        xb = xbuf[slot].reshape(groups * ROW_GROUP, D_MODEL).astype(BF16)
        g = jnp.dot(xb, wgb[...], preferred_element_type=F32)
        u = jnp.dot(xb, wub[...], preferred_element_type=F32)
        mid = (g * jax.nn.sigmoid(g)) * u
        ys_ref[...] = jnp.dot(mid.astype(BF16), wdb[...], preferred_element_type=F32)

    @pl.when(g_cur == 0)
    def _():
        ys_ref[...] = jnp.zeros_like(ys_ref)


def _experts(blk_exp, nvalid, hs, w_gate, w_up, w_down):
    nblk = blk_exp.shape[0]
    rows = EXPERT_ROWS
    grid_spec = pltpu.PrefetchScalarGridSpec(
        num_scalar_prefetch=2,
        grid=(nblk,),
        in_specs=[pl.BlockSpec(memory_space=pl.ANY),
                  pl.BlockSpec((None, D_MODEL, D_EXPERT), lambda i, be, nv: (be[i], 0, 0)),
                  pl.BlockSpec((None, D_MODEL, D_EXPERT), lambda i, be, nv: (be[i], 0, 0)),
                  pl.BlockSpec((None, D_EXPERT, D_MODEL), lambda i, be, nv: (be[i], 0, 0))],
        out_specs=pl.BlockSpec((rows, D_MODEL), lambda i, be, nv: (i, 0)),
        scratch_shapes=[pltpu.VMEM((2, rows // ROW_GROUP, ROW_GROUP, D_MODEL), F32),
                        pltpu.VMEM((D_MODEL, D_EXPERT), BF16), pltpu.VMEM((D_MODEL, D_EXPERT), BF16),
                        pltpu.VMEM((D_EXPERT, D_MODEL), BF16),
                        pltpu.SemaphoreType.DMA((2,))],
    )
    return pl.pallas_call(
        _expert_kernel,
        grid_spec=grid_spec,
        out_shape=jax.ShapeDtypeStruct((nblk * rows, D_MODEL), F32),
        compiler_params=_cparams(("arbitrary",)),
        name="experts",
    )(blk_exp, nvalid, hs, w_gate, w_up, w_down)


def _combine_kernel(dcur_ref, dnxt_ref, x1_ref, meta_ref, ys_hbm, op_ref, os_ref, gbuf, sem, *, nb_prompt):
    i = pl.program_id(0)
    last = pl.num_programs(0) - 1
    slot = i & 1
    tm = x1_ref.shape[0]

    def start_gather(d_ref, s):
        def rows8(g, _):
            for k in range(ROW_GROUP):
                r = g * ROW_GROUP + k
                for e in range(2):
                    pltpu.make_async_copy(ys_hbm.at[pl.ds(d_ref[0, 0, 2 * r + e], 1), :],
                                          gbuf.at[s, e, pl.ds(r, 1), :], sem.at[s]).start(priority=e)
            return 0

        lax.fori_loop(0, tm // ROW_GROUP, rows8, 0)

    @pl.when(i == 0)
    def _():
        start_gather(dcur_ref, 0)

    @pl.when(i < last)
    def _():
        start_gather(dnxt_ref, 1 - slot)

    for e in range(2):
        pltpu.make_async_copy(ys_hbm.at[pl.ds(0, tm), :], gbuf.at[slot, e], sem.at[slot]).wait()
    meta = meta_ref[...]
    y = x1_ref[...] + meta[:, 4:5] * gbuf[slot, 0] + meta[:, 5:6] * gbuf[slot, 1]

    @pl.when(i < nb_prompt)
    def _():
        op_ref[...] = y

    @pl.when(i >= nb_prompt)
    def _():
        os_ref[...] = y


def _combine(x1, ys, meta, dest, t_prompt):
    t_all = x1.shape[0]
    tm = MERGE_ROWS
    nb = t_all // tm
    nbp = t_prompt // tm
    dest = dest.reshape(nb, 1, 2 * tm)
    grid_spec = pl.GridSpec(
        grid=(nb,),
        in_specs=[pl.BlockSpec((1, 1, 2 * tm), lambda i: (i, 0, 0), memory_space=pltpu.SMEM),
                  pl.BlockSpec((1, 1, 2 * tm), lambda i: (jnp.minimum(i + 1, nb - 1), 0, 0),
                               memory_space=pltpu.SMEM),
                  pl.BlockSpec((tm, D_MODEL), lambda i: (i, 0)),
                  pl.BlockSpec((tm, LANES), lambda i: (i, 0)),
                  pl.BlockSpec(memory_space=pl.ANY)],
        out_specs=[pl.BlockSpec((tm, D_MODEL), lambda i: (jnp.minimum(i, nbp - 1), 0)),
                   pl.BlockSpec((tm, D_MODEL), lambda i: (jnp.maximum(i - nbp, 0), 0))],
        scratch_shapes=[pltpu.VMEM((2, 2, tm, D_MODEL), F32), pltpu.SemaphoreType.DMA((2,))],
    )
    return pl.pallas_call(
        functools.partial(_combine_kernel, nb_prompt=nbp),
        grid_spec=grid_spec,
        out_shape=[jax.ShapeDtypeStruct((t_prompt, D_MODEL), F32),
                   jax.ShapeDtypeStruct((t_all - t_prompt, D_MODEL), F32)],
        compiler_params=_cparams(("arbitrary",)),
        name="combine",
    )(dest, dest, x1, meta, ys)


def _prep_weights(norm_mix, w_in, g_q, g_k, w_pool, pool_scale, w_out, norm_ffn, w_group, w_expert):
    lane = jnp.arange(ATTN_WIDTH)
    hsum = (lane[:, None] // HEAD_DIM == lane[None, :] // HEAD_DIM).astype(BF16)
    w_router = jnp.zeros((D_MODEL, LANES), F32)
    w_router = w_router.at[:, 0:N_GROUPS].set(w_group).at[:, N_GROUPS:N_GROUPS + N_EXPERTS].set(w_expert)
    w_router_hi = w_router.astype(BF16)
    w_router_lo = (w_router - w_router_hi.astype(F32)).astype(BF16)
    r = jnp.arange(MERGE_ROWS)
    return {
        "nrm": norm_mix.reshape(1, D_MODEL),
        "w_in": w_in.astype(BF16),
        "gq": (jnp.tile(g_q, N_HEADS) * (HEAD_DIM ** -0.5 * LOG2E)).reshape(1, ATTN_WIDTH),
        "gk": jnp.tile(g_k, N_HEADS).reshape(1, ATTN_WIDTH),
        "hsum": hsum,
        "w_pool": w_pool.astype(BF16),
        "pool_scale": pool_scale.reshape(1, POOL_WIDTH),
        "w_out_pool": w_out[:POOL_WIDTH].astype(BF16),
        "w_out_attn": w_out[POOL_WIDTH:].astype(BF16),
        "nrm_ffn": norm_ffn.reshape(1, D_MODEL),
        "w_router_hi": w_router_hi,
        "w_router_lo": w_router_lo,
        "tri_merge": (r[None, :] < r[:, None]).astype(BF16),
    }


def _later_sums(n_keys, n_heads):
    idx = jnp.arange(n_heads * n_keys)
    same = idx[:, None] // n_keys == idx[None, :] // n_keys
    later = same & (idx[:, None] > idx[None, :])
    return jnp.concatenate([later, same], axis=1).astype(BF16)


def kernel(x_prompt, x_sample, cache_k, cache_v, state_pool, page_table, meta_tokens, norm_mix, w_in, g_q, g_k,
           sb_bias, w_pool, pool_scale, w_out, norm_ffn, w_group, w_expert, w_gate, w_up, w_down):
    nb, seq, _ = x_prompt.shape
    nseq, nt, _ = x_sample.shape
    n_phys = cache_k.shape[1]
    t_prompt = nb * seq
    t_sample = nseq * nt
    t_all = t_prompt + t_sample

    cw = _prep_weights(norm_mix[0], w_in[0], g_q[0], g_k[0], w_pool[0], pool_scale[0], w_out[0], norm_ffn[0],
                       w_group[0], w_expert[0])
    bias = sb_bias[0].astype(F32) * LOG2E

    zero_hist = jnp.zeros((HIST_ROWS, POOL_WIDTH), F32)
    _, k_m, v_m, kx_m, vx_m, _, tail_m = _proj_seq(meta_tokens[None], zero_hist, cw, tm=N_META, p0=0,
                                                   full_count=False)
    pad_rows = ((0, ATT_BLOCK - N_META), (0, 0))
    kxm = jnp.pad(kx_m[0], pad_rows)
    vxm = jnp.pad(vx_m[0], pad_rows)

    q_p, k_p, v_p, kx_p, vx_p, po_p, tail_p = _proj_seq(x_prompt, tail_m[0], cw, tm=PROJ_ROWS, p0=N_META,
                                                        full_count=True)
    head_bias = jnp.repeat(bias, ATT_BLOCK)
    col = jnp.arange(N_HEADS * ATT_BLOCK) % ATT_BLOCK
    bias_rows = jnp.stack([head_bias, jnp.where(col < N_META, head_bias, NEG)])
    attn_p = _attn_prompt(q_p, kx_p, vx_p, kxm, vxm, bias_rows, _later_sums(ATT_BLOCK, 2))

    x_s_tm = jnp.transpose(x_sample, (1, 0, 2)).reshape(t_sample, D_MODEL)
    hist_tm = jnp.transpose(state_pool[0], (1, 0, 2))
    q_s, k_s, v_s, po_s, tail_s = _proj_step(x_s_tm, hist_tm, cw, nseq=nseq, nt=nt)
    to_seq = lambda a: jnp.transpose(a.reshape(nt, nseq, a.shape[-1]), (1, 0, 2))
    q_sb, k_sb, v_sb, po_sb = to_seq(q_s), to_seq(k_s), to_seq(v_s), to_seq(po_s)
    head_of_lane = jnp.arange(ATTN_WIDTH) // HEAD_DIM
    head_mask = head_of_lane[None, :] == jnp.arange(N_HEADS)[:, None]
    qbd = jnp.where(head_mask[None, None], q_sb[:, :, None, :], 0).reshape(nseq, nt * N_HEADS, ATTN_WIDTH)
    pad_new = ((0, 0), (0, 8 - nt), (0, 0))
    kn = jnp.pad(k_sb, pad_new)
    vn = jnp.pad(v_sb, pad_new)
    bias_col = jnp.tile(bias, nt).reshape(nt * N_HEADS, 1)
    page_t = lambda c: jnp.transpose(c[0], (0, 2, 3, 1)).reshape(n_phys, ATTN_WIDTH, PAGE)
    attn_s = _attn_sample(page_table, qbd, kn, vn, bias_col, _later_sums(PAGE, 1), page_t(cache_k), page_t(cache_v))

    x1, h_all, meta, counts_f = _merge_router(
        x_prompt.reshape(t_prompt, D_MODEL), x_sample.reshape(t_sample, D_MODEL),
        po_p.reshape(t_prompt, POOL_WIDTH), po_sb.reshape(t_sample, POOL_WIDTH),
        attn_p.reshape(t_prompt, ATTN_WIDTH), attn_s.reshape(t_sample, ATTN_WIDTH).astype(BF16), cw)

    pair_exp = meta[:, 0:2].astype(jnp.int32)
    pair_rank = meta[:, 2:4].astype(jnp.int32)
    counts = counts_f[0, :N_EXPERTS].astype(jnp.int32)
    rows = EXPERT_ROWS
    padded = (counts + rows - 1) // rows * rows
    pad_end = jnp.cumsum(padded)
    pad_start = pad_end - padded
    nblk = -(-(2 * t_all) // rows) + N_EXPERTS
    of_expert = pair_exp[..., None] == jnp.arange(N_EXPERTS, dtype=jnp.int32)
    dest = jnp.sum(jnp.where(of_expert, pad_start, 0), axis=-1) + pair_rank
    blk_row0 = jnp.arange(nblk, dtype=jnp.int32) * rows
    blk_exp = jnp.minimum(jnp.sum(blk_row0[:, None] >= pad_end[None, :], axis=1), N_EXPERTS - 1).astype(jnp.int32)
    blk_of = blk_exp[:, None] == jnp.arange(N_EXPERTS, dtype=jnp.int32)
    blk_count = jnp.sum(jnp.where(blk_of, counts, 0), axis=1)
    blk_start = jnp.sum(jnp.where(blk_of, pad_start, 0), axis=1)
    n_rows = jnp.clip(blk_count - (blk_row0 - blk_start), 0, rows)
    n_rows = jnp.where(blk_row0 < pad_end[-1], n_rows, 0).astype(jnp.int32)
    n_tiled = (n_rows + ROW_GROUP - 1) // ROW_GROUP * ROW_GROUP
    fills = (blk_row0 + n_rows, n_tiled - n_rows, (blk_row0 + n_tiled) // ROW_GROUP, (rows - n_tiled) // ROW_GROUP)

    hs = _dispatch(fills, dest, h_all, nblk * rows)
    ys = _experts(blk_exp, n_tiled, hs, w_gate[0], w_up[0], w_down[0])
    y_prompt, y_sample = _combine(x1, ys, meta, dest, t_prompt)
    y_prompt = y_prompt.reshape(nb, seq, D_MODEL)
    y_sample = y_sample.reshape(nseq, nt, D_MODEL)

    heads = lambda a: a.reshape(a.shape[:-1] + (N_HEADS, HEAD_DIM))
    k_prompt = jnp.concatenate([jnp.broadcast_to(k_m, (nb, N_META, ATTN_WIDTH)), k_p], axis=1)
    v_prompt = jnp.concatenate([jnp.broadcast_to(v_m, (nb, N_META, ATTN_WIDTH)), v_p], axis=1)
    pool_sample = jnp.transpose(tail_s, (1, 0, 2))
    return (y_prompt, y_sample, heads(k_prompt)[None], heads(v_prompt)[None], tail_p[:, 1:][None],
            heads(k_sb)[None], heads(v_sb)[None], pool_sample[None])
```

```python
import functools

import jax
import jax.numpy as jnp
from jax import lax
from jax.experimental import pallas as pl
from jax.experimental.pallas import tpu as pltpu

F32 = jnp.float32
BF16 = jnp.bfloat16

D_MODEL = 1024
POOL_WIDTH = 512
ATTN_WIDTH = 512
HEAD_DIM = 64
N_HEADS = 8
N_PAIRS = N_HEADS // 2
POOL_WINDOWS = (2, 4, 8, 16)
POOL_GROUP = 128
POOL_HIST = 15
HIST_ROWS = 16
N_META = 16
PAGE = 128
N_GROUPS = 4
EXPERTS_PER_GROUP = 8
N_EXPERTS = 32
D_EXPERT = 512
EPS = 1e-6
NEG = -1e30
LOG2E = 1.4426950408889634
LOGIT_CAP = 100.0
LANES = 128
ATT_BLOCK = 128
ATT_UNROLL = 4
PROJ_ROWS = 512
MERGE_ROWS = 256
EXPERT_ROWS = 256
ROW_GROUP = 8
VMEM_LIMIT = 56 * 1024 * 1024


def _cparams(sem, vmem=VMEM_LIMIT):
    return pltpu.CompilerParams(dimension_semantics=sem, vmem_limit_bytes=vmem)


def _rms_rows(x, g):
    ms = jnp.mean(x * x, axis=-1, keepdims=True)
    return (x * lax.rsqrt(ms + EPS)) * g


def _head_norm(a, g, hsum):
    sq = a * a
    hi = sq.astype(BF16)
    lo = (sq - hi.astype(F32)).astype(BF16)
    ssum = (jnp.dot(hi, hsum, preferred_element_type=F32)
            + jnp.dot(lo, hsum, preferred_element_type=F32))
    return (a * lax.rsqrt(ssum * (1.0 / HEAD_DIM) + EPS)) * g


def _project(x, nrm, w_in, gq, gk, hsum):
    n = _rms_rows(x, nrm)
    proj = jnp.dot(n.astype(BF16), w_in, preferred_element_type=F32)
    u = proj[:, 0:POOL_WIDTH]
    q = _head_norm(proj[:, POOL_WIDTH:POOL_WIDTH + ATTN_WIDTH], gq, hsum)
    k = _head_norm(proj[:, POOL_WIDTH + ATTN_WIDTH:POOL_WIDTH + 2 * ATTN_WIDTH], gk, hsum)
    v = proj[:, POOL_WIDTH + 2 * ATTN_WIDTH:]
    return u, q, k, v


def _split_heads(a, out_ref):
    rows = a.shape[0]
    lane = lax.broadcasted_iota(jnp.int32, (rows, LANES), 1)
    first = lane < HEAD_DIM
    for p in range(N_PAIRS):
        a2 = a[:, p * LANES:(p + 1) * LANES]
        out_ref[:, 2 * p * LANES:(2 * p + 1) * LANES] = jnp.where(first, a2, 0.0).astype(out_ref.dtype)
        out_ref[:, (2 * p + 1) * LANES:(2 * p + 2) * LANES] = jnp.where(first, 0.0, a2).astype(out_ref.dtype)


def _log2_gates(z, visible=None):
    z = lax.clamp(-LOGIT_CAP, z, LOGIT_CAP)
    if visible is not None:
        z = jnp.where(visible, z, NEG)
    drop = jnp.log(1.0 + jnp.exp2(z)) * LOG2E
    return z - drop, drop


def _stick_step(tt, gate_in, mix_in):
    def stack_heads(b):
        half = b.shape[1] // 2
        return jnp.concatenate([b[:, :half], b[:, half:]], axis=0)

    nt_dims = (((1,), (1,)), ((), ()))
    gates = mixed = None
    if mix_in is not None:
        log_betas, drops, carries, vxbs = mix_in
        sums = [jnp.dot(drop, tt, preferred_element_type=F32) for drop in drops]
    if gate_in is not None:
        q2s, kxbs, brows, mask = gate_in
        zs = [lax.dot_general(q2, stack_heads(kxb), nt_dims, preferred_element_type=F32) + brow
              for q2, kxb, brow in zip(q2s, kxbs, brows)]
    if mix_in is not None:
        nk = sums[0].shape[1] // 2
        probs = [jnp.exp2(lb - s[:, :nk] - carry).astype(BF16) for lb, s, carry in zip(log_betas, sums, carries)]
    if gate_in is not None:
        gates = [(lb, lk.astype(BF16)) for lb, lk in (_log2_gates(z, mask) for z in zs)]
    if mix_in is not None:
        outs = [jnp.dot(a, stack_heads(vxb), preferred_element_type=F32) for a, vxb in zip(probs, vxbs)]
        mixed = [(out, carry + s[:, nk:]) for out, carry, s in zip(outs, carries, sums)]
    return gates, mixed


def _proj_seq_kernel(x_ref, hist_ref, nrm_ref, win_ref, gq_ref, gk_ref, hsum_ref, wpool_ref, ps_ref,
                     q_ref, k_ref, v_ref, kx_ref, vx_ref, po_ref, tail_ref, ext_ref, *, tm, p0, full_count):
    blk = pl.program_id(1)
    u, q, k, v = _project(x_ref[...], nrm_ref[...], win_ref[...], gq_ref[...], gk_ref[...], hsum_ref[...])
    q_ref[...] = q.astype(q_ref.dtype)
    k_ref[...] = k
    v_ref[...] = v
    _split_heads(k, kx_ref)
    _split_heads(v, vx_ref)

    @pl.when(blk == 0)
    def _():
        ext_ref[0:HIST_ROWS, :] = hist_ref[...]

    ext_ref[HIST_ROWS:HIST_ROWS + tm, :] = u
    for g, w in enumerate(POOL_WINDOWS):
        e = ext_ref[:, g * POOL_GROUP:(g + 1) * POOL_GROUP]
        s = e
        sh = 1
        while sh < w:
            s = s + pltpu.roll(s, sh, axis=0)
            sh *= 2
        win = s[HIST_ROWS:, :]
        tok = e[HIST_ROWS:, :]
        if full_count:
            d = win * (1.0 / w) - tok
        else:
            pos = p0 + lax.broadcasted_iota(jnp.int32, (tm, 1), 0)
            cnt = jnp.minimum(w, pos + 1).astype(F32)
            d = win / cnt - tok
        y = jnp.dot(d.astype(BF16), wpool_ref[g], preferred_element_type=F32)
        y = y * ps_ref[:, g * POOL_GROUP:(g + 1) * POOL_GROUP]
        po_ref[:, g * POOL_GROUP:(g + 1) * POOL_GROUP] = y.astype(po_ref.dtype)
    last = ext_ref[tm:tm + HIST_ROWS, :]
    ext_ref[0:HIST_ROWS, :] = last

    @pl.when(blk == pl.num_programs(1) - 1)
    def _():
        tail_ref[...] = last


def _proj_seq(x, hist, cw, *, tm, p0, full_count):
    b, l, _ = x.shape
    nb = l // tm
    const = lambda shape: pl.BlockSpec(shape, lambda i, j: (0,) * len(shape))
    row = lambda width: pl.BlockSpec((None, tm, width), lambda i, j: (i, j, 0))
    kern = functools.partial(_proj_seq_kernel, tm=tm, p0=p0, full_count=full_count)
    return pl.pallas_call(
        kern,
        grid=(b, nb),
        in_specs=[row(D_MODEL), const((HIST_ROWS, POOL_WIDTH)), const((1, D_MODEL)),
                  const((D_MODEL, 2 * D_MODEL)), const((1, ATTN_WIDTH)), const((1, ATTN_WIDTH)),
                  const((ATTN_WIDTH, ATTN_WIDTH)), const((4, POOL_GROUP, POOL_GROUP)), const((1, POOL_WIDTH))],
        out_specs=[row(ATTN_WIDTH), row(ATTN_WIDTH), row(ATTN_WIDTH), row(2 * ATTN_WIDTH), row(2 * ATTN_WIDTH),
                   row(POOL_WIDTH), pl.BlockSpec((None, HIST_ROWS, POOL_WIDTH), lambda i, j: (i, 0, 0))],
        out_shape=[jax.ShapeDtypeStruct((b, l, ATTN_WIDTH), BF16),
                   jax.ShapeDtypeStruct((b, l, ATTN_WIDTH), F32),
                   jax.ShapeDtypeStruct((b, l, ATTN_WIDTH), F32),
                   jax.ShapeDtypeStruct((b, l, 2 * ATTN_WIDTH), BF16),
                   jax.ShapeDtypeStruct((b, l, 2 * ATTN_WIDTH), BF16),
                   jax.ShapeDtypeStruct((b, l, POOL_WIDTH), BF16),
                   jax.ShapeDtypeStruct((b, HIST_ROWS, POOL_WIDTH), F32)],
        scratch_shapes=[pltpu.VMEM((HIST_ROWS + tm, POOL_WIDTH), F32)],
        compiler_params=_cparams(("arbitrary", "arbitrary")),
        name="proj_seq",
    )(x, hist, cw["nrm"], cw["w_in"], cw["gq"], cw["gk"], cw["hsum"], cw["w_pool"], cw["pool_scale"])


def _proj_step_kernel(x_ref, hist_ref, nrm_ref, win_ref, gq_ref, gk_ref, hsum_ref, wpool_ref, ps_ref,
                      q_ref, k_ref, v_ref, po_ref, tail_ref, *, nseq, nt):
    u, q, k, v = _project(x_ref[...], nrm_ref[...], win_ref[...], gq_ref[...], gk_ref[...], hsum_ref[...])
    q_ref[...] = q.astype(q_ref.dtype)
    k_ref[...] = k
    v_ref[...] = v
    ext = [hist_ref[r] for r in range(POOL_HIST)] + [u[t * nseq:(t + 1) * nseq, :] for t in range(nt)]
    for r in range(POOL_HIST):
        tail_ref[r] = ext[nt + r]
    for g, w in enumerate(POOL_WINDOWS):
        sl = slice(g * POOL_GROUP, (g + 1) * POOL_GROUP)
        ds = []
        for t in range(nt):
            win = ext[POOL_HIST + t][:, sl]
            for i in range(1, w):
                win = win + ext[POOL_HIST + t - i][:, sl]
            ds.append(win * (1.0 / w) - ext[POOL_HIST + t][:, sl])
        d = jnp.concatenate(ds, axis=0)
        y = jnp.dot(d.astype(BF16), wpool_ref[g], preferred_element_type=F32) * ps_ref[:, sl]
        po_ref[:, sl] = y.astype(po_ref.dtype)


def _proj_step(x_tm, hist_tm, cw, *, nseq, nt):
    rows = nseq * nt
    kern = functools.partial(_proj_step_kernel, nseq=nseq, nt=nt)
    return pl.pallas_call(
        kern,
        out_shape=[jax.ShapeDtypeStruct((rows, ATTN_WIDTH), BF16),
                   jax.ShapeDtypeStruct((rows, ATTN_WIDTH), F32),
                   jax.ShapeDtypeStruct((rows, ATTN_WIDTH), F32),
                   jax.ShapeDtypeStruct((rows, POOL_WIDTH), BF16),
                   jax.ShapeDtypeStruct((POOL_HIST, nseq, POOL_WIDTH), F32)],
        compiler_params=pltpu.CompilerParams(vmem_limit_bytes=VMEM_LIMIT),
        name="proj_step",
    )(x_tm, hist_tm, cw["nrm"], cw["w_in"], cw["gq"], cw["gk"], cw["hsum"], cw["w_pool"], cw["pool_scale"])


def _attn_prompt_kernel(q_ref, kx_ref, vx_ref, kxm_ref, vxm_ref, bias_ref, tt_ref, o_ref,
                        carry_ref, acc_ref, lb_ref, lk_ref):
    qi = pl.program_id(1)
    tt = tt_ref[...]
    nq = ATT_BLOCK
    pairs = range(N_PAIRS)
    lanes = [slice(p * LANES, (p + 1) * LANES) for p in pairs]
    wide = [slice(2 * p * LANES, 2 * (p + 1) * LANES) for p in pairs]

    def rows(j):
        return pl.ds(pl.multiple_of(j * ATT_BLOCK, ATT_BLOCK), ATT_BLOCK)

    def gate_in(k_of, mask):
        return [q_ref[:, l] for l in lanes], [k_of(w) for w in wide], [bias_ref[:, w] for w in wide], mask

    def pending():
        return [lb_ref[p] for p in pairs], [lk_ref[p] for p in pairs]

    def keep(gates):
        lbs, drops = gates
        for p in pairs:
            lb_ref[p] = lbs[p]
            lk_ref[p] = drops[p]

    def step(gin, gates_prev, v_of=None, last=False):
        min_ = accs = None
        if gates_prev is not None:
            lbs, drops = gates_prev
            min_ = (lbs, drops, [carry_ref[p] for p in pairs], [v_of(w) for w in wide])
            accs = [acc_ref[:, l] for l in lanes]
        gates, mixed = _stick_step(tt, gin, min_)
        if mixed is not None:
            for p, (out, carry) in enumerate(mixed):
                if last:
                    o_ref[:, lanes[p]] = (accs[p] + out).astype(o_ref.dtype)
                else:
                    carry_ref[p] = carry
                    acc_ref[:, lanes[p]] = accs[p] + out
        return None if gates is None else ([g[0] for g in gates], [g[1] for g in gates])

    def token_tile(j):
        return gate_in(lambda w: kx_ref[rows(j), w], None)

    carry_ref[...] = jnp.zeros_like(carry_ref)
    acc_ref[...] = jnp.zeros_like(acc_ref)
    r_i = lax.broadcasted_iota(jnp.int32, (nq, 2 * ATT_BLOCK), 0)
    c_i = lax.broadcasted_iota(jnp.int32, (nq, 2 * ATT_BLOCK), 1) & (ATT_BLOCK - 1)
    keep(step(gate_in(lambda w: kx_ref[rows(qi), w], c_i < r_i), None))

    def tiles_from(j, n):
        g = pending()
        for k in range(n):
            g = step(token_tile(j - k), g, lambda w, k=k: vx_ref[rows(j - k + 1), w])
        keep(g)

    def unrolled_trip(t, _):
        tiles_from(qi - 1 - ATT_UNROLL * t, ATT_UNROLL)
        return 0

    n_trips = qi // ATT_UNROLL
    lax.fori_loop(0, n_trips, unrolled_trip, 0)

    def single_trip(t, _):
        tiles_from(qi - 1 - ATT_UNROLL * n_trips - t, 1)
        return 0

    lax.fori_loop(0, qi - ATT_UNROLL * n_trips, single_trip, 0)

    g = step(gate_in(lambda w: kxm_ref[:, w], c_i < N_META), pending(), lambda w: vx_ref[0:ATT_BLOCK, w])
    step(None, g, lambda w: vxm_ref[:, w], last=True)


def _attn_prompt(q, kx, vx, kxm, vxm, bias_rows, tt):
    b, l, _ = q.shape
    nq = l // ATT_BLOCK
    once = pl.Buffered(1)
    return pl.pallas_call(
        _attn_prompt_kernel,
        grid=(b, nq),
        in_specs=[pl.BlockSpec((None, ATT_BLOCK, ATTN_WIDTH), lambda i, j: (i, j, 0)),
                  pl.BlockSpec((None, l, 2 * ATTN_WIDTH), lambda i, j: (i, 0, 0), pipeline_mode=once),
                  pl.BlockSpec((None, l, 2 * ATTN_WIDTH), lambda i, j: (i, 0, 0), pipeline_mode=once),
                  pl.BlockSpec((ATT_BLOCK, 2 * ATTN_WIDTH), lambda i, j: (0, 0)),
                  pl.BlockSpec((ATT_BLOCK, 2 * ATTN_WIDTH), lambda i, j: (0, 0)),
                  pl.BlockSpec((1, 2 * ATTN_WIDTH), lambda i, j: (0, 0)),
                  pl.BlockSpec((2 * ATT_BLOCK, 4 * ATT_BLOCK), lambda i, j: (0, 0))],
        out_specs=pl.BlockSpec((None, ATT_BLOCK, ATTN_WIDTH), lambda i, j: (i, j, 0)),
        out_shape=jax.ShapeDtypeStruct((b, l, ATTN_WIDTH), BF16),
        scratch_shapes=[pltpu.VMEM((N_PAIRS, ATT_BLOCK, 2 * ATT_BLOCK), F32),
                        pltpu.VMEM((ATT_BLOCK, ATTN_WIDTH), F32),
                        pltpu.VMEM((N_PAIRS, ATT_BLOCK, 2 * ATT_BLOCK), F32),
                        pltpu.VMEM((N_PAIRS, ATT_BLOCK, 2 * ATT_BLOCK), BF16)],
        compiler_params=_cparams(("arbitrary", "arbitrary")),
        name="attn_prompt",
    )(q, kx, vx, kxm, vxm, bias_rows, tt)


def _attn_sample_kernel(pt_ref, qbd_ref, kn_ref, vn_ref, bias_ref, tt_ref, *refs, n_pages, nt):
    del pt_ref
    k_refs = refs[:n_pages]
    v_refs = refs[n_pages:2 * n_pages]
    o_ref = refs[2 * n_pages]
    qbd = qbd_ref[...]
    tt = tt_ref[...]
    bias = bias_ref[...]
    nc = qbd.shape[0]
    nn = (((1,), (0,)), ((), ()))
    nt_dims = (((1,), (1,)), ((), ()))

    pad = jnp.zeros((PAGE - kn_ref.shape[0], ATTN_WIDTH), F32)
    kn = jnp.concatenate([kn_ref[...], pad], axis=0).astype(BF16)
    vn = jnp.concatenate([vn_ref[...], pad], axis=0).astype(BF16)
    t_i = lax.broadcasted_iota(jnp.int32, (nc, PAGE), 0) >> 3
    k_i = lax.broadcasted_iota(jnp.int32, (nc, PAGE), 1)
    z_new = lax.dot_general(qbd, kn, nt_dims, preferred_element_type=F32) + bias
    zs = [z_new]
    for j in range(n_pages - 1, -1, -1):
        zs.append(lax.dot_general(qbd, k_refs[j][...].astype(BF16), nn, preferred_element_type=F32) + bias)
    gates = [_log2_gates(z, k_i < t_i if i == 0 else None) for i, z in enumerate(zs)]
    sums = [jnp.dot(lk.astype(BF16), tt, preferred_element_type=F32) for _, lk in gates]
    carry = jnp.zeros((nc, PAGE), F32)
    acc = None
    for i, ((lb, _), s) in enumerate(zip(gates, sums)):
        a = jnp.exp2(lb - s[:, :PAGE] - carry).astype(BF16)
        carry = carry + s[:, PAGE:]
        if i == 0:
            acc = jnp.dot(a, vn, preferred_element_type=F32)
        else:
            vt = v_refs[n_pages - i][...].astype(BF16)
            acc = acc + lax.dot_general(a, vt, nt_dims, preferred_element_type=F32)
    h_i = lax.broadcasted_iota(jnp.int32, (nc, ATTN_WIDTH), 0) & (N_HEADS - 1)
    l_i = lax.broadcasted_iota(jnp.int32, (nc, ATTN_WIDTH), 1) >> 6
    acc = jnp.where(h_i == l_i, acc, 0.0)
    for t in range(nt):
        o_ref[t:t + 1, :] = jnp.sum(acc[t * N_HEADS:(t + 1) * N_HEADS, :], axis=0, keepdims=True)


def _attn_sample(page_table, qbd, kn, vn, bias_col, tt, cache_k, cache_v):
    nseq, n_pages = page_table.shape
    nt = qbd.shape[1] // N_HEADS
    pt = page_table.reshape(-1)

    def page_spec(j):
        return pl.BlockSpec((None, ATTN_WIDTH, PAGE), lambda i, pt_ref: (pt_ref[i * n_pages + j], 0, 0))

    grid_spec = pltpu.PrefetchScalarGridSpec(
        num_scalar_prefetch=1,
        grid=(nseq,),
        in_specs=[pl.BlockSpec((None, nt * N_HEADS, ATTN_WIDTH), lambda i, pt_ref: (i, 0, 0)),
                  pl.BlockSpec((None, 8, ATTN_WIDTH), lambda i, pt_ref: (i, 0, 0)),
                  pl.BlockSpec((None, 8, ATTN_WIDTH), lambda i, pt_ref: (i, 0, 0)),
                  pl.BlockSpec((nt * N_HEADS, 1), lambda i, pt_ref: (0, 0)),
                  pl.BlockSpec((PAGE, 2 * PAGE), lambda i, pt_ref: (0, 0))]
                 + [page_spec(j) for j in range(n_pages)] + [page_spec(j) for j in range(n_pages)],
        out_specs=pl.BlockSpec((None, nt, ATTN_WIDTH), lambda i, pt_ref: (i, 0, 0)),
    )
    kern = functools.partial(_attn_sample_kernel, n_pages=n_pages, nt=nt)
    return pl.pallas_call(
        kern,
        grid_spec=grid_spec,
        out_shape=jax.ShapeDtypeStruct((nseq, nt, ATTN_WIDTH), F32),
        compiler_params=_cparams(("arbitrary",)),
        name="attn_sample",
    )(pt, qbd, kn, vn, bias_col, tt, *([cache_k] * n_pages), *([cache_v] * n_pages))


def _merge_router_kernel(xp_ref, xs_ref, pop_ref, pos_ref, atp_ref, ats_ref, wop_ref, woa_ref, nrm_ref, wrh_ref,
                         wrl_ref, tri_ref, x1_ref, h_ref, meta_ref, cout_ref, carry_ref, *, nb_prompt):
    i = pl.program_id(0)

    @pl.when(i == 0)
    def _():
        carry_ref[...] = jnp.zeros_like(carry_ref)

    is_p = i < nb_prompt
    x = jnp.where(is_p, xp_ref[...], xs_ref[...])
    po = jnp.where(is_p, pop_ref[...], pos_ref[...])
    at = jnp.where(is_p, atp_ref[...], ats_ref[...])
    y = (jnp.dot(po, wop_ref[...], preferred_element_type=F32)
         + jnp.dot(at, woa_ref[...], preferred_element_type=F32))
    x1 = x + y
    x1_ref[...] = x1
    h = _rms_rows(x1, nrm_ref[...])
    h_ref[...] = h
    hi = h.astype(BF16)
    lo = (h - hi.astype(F32)).astype(BF16)
    both = jnp.dot(hi, jnp.concatenate([wrh_ref[...], wrl_ref[...]], axis=1), preferred_element_type=F32)
    logits = both[:, :LANES] + both[:, LANES:] + jnp.dot(lo, wrh_ref[...], preferred_element_type=F32)
    tm = logits.shape[0]
    lane = lax.broadcasted_iota(jnp.int32, (tm, LANES), 1).astype(F32)
    ninf = -jnp.inf
    is_g = lane < N_GROUPS
    gl = jnp.where(is_g, logits, ninf)
    gmax = jnp.max(gl, axis=-1, keepdims=True)
    g_sel = jnp.min(jnp.where(gl == gmax, lane, float(LANES)), axis=-1, keepdims=True)
    psum = jnp.sum(jnp.where(is_g, jnp.exp(logits - gmax), 0.0), axis=-1, keepdims=True)
    g_w = 1.0 / psum
    e_lo = N_GROUPS + g_sel * EXPERTS_PER_GROUP
    in_group = (lane >= e_lo) & (lane < e_lo + EXPERTS_PER_GROUP)
    el = jnp.where(in_group, logits, ninf)
    v1 = jnp.max(el, axis=-1, keepdims=True)
    i1 = jnp.min(jnp.where(el == v1, lane, float(LANES)), axis=-1, keepdims=True)
    el2 = jnp.where(lane == i1, ninf, el)
    v2 = jnp.max(el2, axis=-1, keepdims=True)
    i2 = jnp.min(jnp.where(el2 == v2, lane, float(LANES)), axis=-1, keepdims=True)
    e21 = jnp.exp(v2 - v1)
    den = 1.0 + e21
    gate1 = g_w * (1.0 / den)
    gate2 = g_w * (e21 / den)
    e1 = i1 - N_GROUPS
    e2 = i2 - N_GROUPS
    onehot = jnp.where((lane == e1) | (lane == e2), 1.0, 0.0)
    before = jnp.dot(tri_ref[...], onehot.astype(BF16), preferred_element_type=F32) + carry_ref[...]
    rank1 = jnp.sum(jnp.where(lane == e1, before, 0.0), axis=-1, keepdims=True)
    rank2 = jnp.sum(jnp.where(lane == e2, before, 0.0), axis=-1, keepdims=True)
    carry = carry_ref[...] + jnp.sum(onehot, axis=0, keepdims=True)
    carry_ref[...] = carry
    cout_ref[...] = carry
    meta = jnp.where(lane == 0, e1, 0.0)
    meta = jnp.where(lane == 1, e2, meta)
    meta = jnp.where(lane == 2, rank1, meta)
    meta = jnp.where(lane == 3, rank2, meta)
    meta = jnp.where(lane == 4, gate1, meta)
    meta = jnp.where(lane == 5, gate2, meta)
    meta_ref[...] = meta


def _merge_router(x_p, x_s, po_p, po_s, at_p, at_s, cw):
    tm = MERGE_ROWS
    nbp = x_p.shape[0] // tm
    nbs = x_s.shape[0] // tm
    const = lambda shape: pl.BlockSpec(shape, lambda i: (0,) * len(shape))
    prow = lambda width: pl.BlockSpec((tm, width), lambda i: (jnp.minimum(i, nbp - 1), 0))
    srow = lambda width: pl.BlockSpec((tm, width), lambda i: (jnp.maximum(i - nbp, 0), 0))
    row = lambda width: pl.BlockSpec((tm, width), lambda i: (i, 0))
    t_all = (nbp + nbs) * tm
    return pl.pallas_call(
        functools.partial(_merge_router_kernel, nb_prompt=nbp),
        grid=(nbp + nbs,),
        in_specs=[prow(D_MODEL), srow(D_MODEL), prow(POOL_WIDTH), srow(POOL_WIDTH), prow(ATTN_WIDTH),
                  srow(ATTN_WIDTH), const((POOL_WIDTH, D_MODEL)), const((ATTN_WIDTH, D_MODEL)),
                  const((1, D_MODEL)), const((D_MODEL, LANES)), const((D_MODEL, LANES)), const((tm, tm))],
        out_specs=[row(D_MODEL), row(D_MODEL), row(LANES), const((1, LANES))],
        out_shape=[jax.ShapeDtypeStruct((t_all, D_MODEL), F32),
                   jax.ShapeDtypeStruct((t_all, D_MODEL), F32),
                   jax.ShapeDtypeStruct((t_all, LANES), F32),
                   jax.ShapeDtypeStruct((1, LANES), F32)],
        scratch_shapes=[pltpu.VMEM((1, LANES), F32)],
        compiler_params=_cparams(("arbitrary",)),
        name="merge_router",
    )(x_p, x_s, po_p, po_s, at_p, at_s, cw["w_out_pool"], cw["w_out_attn"], cw["nrm_ffn"], cw["w_router_hi"],
      cw["w_router_lo"], cw["tri_merge"])


def _dispatch_kernel(fill_r0_ref, fill_nr_ref, fill_g0_ref, fill_ng_ref, dest_ref, h_ref, hs_hbm, zbuf, sem):
    i = pl.program_id(0)
    tm = h_ref.shape[0]

    def row_of(r):
        return hs_hbm.at[r >> 3, pl.ds(r & (ROW_GROUP - 1), 1), :]

    @pl.when(i == 0)
    def _():
        zbuf[...] = jnp.zeros_like(zbuf)

        def fill(wait):
            def body(blk, _):
                for k in range(ROW_GROUP - 1):
                    @pl.when(k < fill_nr_ref[blk])
                    def _():
                        cp = pltpu.make_async_copy(zbuf.at[0, pl.ds(0, 1), :], row_of(fill_r0_ref[blk] + k), sem)
                        cp.wait() if wait else cp.start()

                ng = fill_ng_ref[blk]

                @pl.when(ng > 0)
                def _():
                    cp = pltpu.make_async_copy(zbuf.at[pl.ds(0, ng)], hs_hbm.at[pl.ds(fill_g0_ref[blk], ng)], sem)
                    cp.wait() if wait else cp.start()
                return 0

            lax.fori_loop(0, fill_r0_ref.shape[0], body, 0)

        fill(False)
        fill(True)

    def rows8(g, _):
        for k in range(ROW_GROUP):
            r = g * ROW_GROUP + k
            for e in range(2):
                pltpu.make_async_copy(h_ref.at[pl.ds(r, 1), :], row_of(dest_ref[0, 0, 2 * r + e]),
                                      sem).start(priority=e)
        return 0

    lax.fori_loop(0, tm // ROW_GROUP, rows8, 0)
    for _ in range(2):
        pltpu.make_async_copy(zbuf, hs_hbm.at[pl.ds(0, tm // ROW_GROUP)], sem).wait()


def _dispatch(fills, dest, h_all, n_sorted):
    t_all = h_all.shape[0]
    tm = EXPERT_ROWS
    grid_spec = pltpu.PrefetchScalarGridSpec(
        num_scalar_prefetch=4,
        grid=(t_all // tm,),
        in_specs=[pl.BlockSpec((1, 1, 2 * tm), lambda i, *_: (i, 0, 0), memory_space=pltpu.SMEM),
                  pl.BlockSpec((tm, D_MODEL), lambda i, *_: (i, 0))],
        out_specs=pl.BlockSpec(memory_space=pl.ANY),
        scratch_shapes=[pltpu.VMEM((tm // ROW_GROUP, ROW_GROUP, D_MODEL), F32), pltpu.SemaphoreType.DMA(())],
    )
    return pl.pallas_call(
        _dispatch_kernel,
        grid_spec=grid_spec,
        out_shape=jax.ShapeDtypeStruct((n_sorted // ROW_GROUP, ROW_GROUP, D_MODEL), F32),
        compiler_params=_cparams(("arbitrary",)),
        name="dispatch",
    )(*fills, dest.reshape(t_all // tm, 1, 2 * tm), h_all)


def _expert_kernel(bexp_ref, nval_ref, hs_hbm, wg_ref, wu_ref, wd_ref, ys_ref, xbuf, wgb, wub, wdb, sem):
    i = pl.program_id(0)
    last = pl.num_programs(0) - 1
    slot = i & 1
    groups = xbuf.shape[1]
    g_cur = lax.shift_right_logical(nval_ref[i], 3)
    g_next = lax.shift_right_logical(nval_ref[jnp.minimum(i + 1, last)], 3)

    def fetch(blk, s, ng):
        return pltpu.make_async_copy(hs_hbm.at[pl.ds(blk * groups, ng)], xbuf.at[s, pl.ds(0, ng)], sem.at[s])

    @pl.when(i == 0)
    def _():
        xbuf[...] = jnp.zeros_like(xbuf)

        @pl.when(g_cur > 0)
        def _():
            fetch(0, 0, g_cur).start()

    @pl.when((i < last) & (g_next > 0))
    def _():
        fetch(i + 1, 1 - slot, g_next).start()

    @pl.when(g_cur > 0)
    def _():
        changed = jnp.logical_or(i == 0, bexp_ref[i] != bexp_ref[jnp.maximum(i - 1, 0)])

        @pl.when(changed)
        def _():
            wgb[...] = wg_ref[...].astype(BF16)
            wub[...] = wu_ref[...].astype(BF16)
            wdb[...] = wd_ref[...].astype(BF16)

        fetch(i, slot, g_cur).wait()
        xb = xbuf[slot].reshape(groups * ROW_GROUP, D_MODEL).astype(BF16)
        g = jnp.dot(xb, wgb[...], preferred_element_type=F32)
        u = jnp.dot(xb, wub[...], preferred_element_type=F32)
        mid = (g * jax.nn.sigmoid(g)) * u
        ys_ref[...] = jnp.dot(mid.astype(BF16), wdb[...], preferred_element_type=F32)

    @pl.when(g_cur == 0)
    def _():
        ys_ref[...] = jnp.zeros_like(ys_ref)


def _experts(blk_exp, nvalid, hs, w_gate, w_up, w_down):
    nblk = blk_exp.shape[0]
    rows = EXPERT_ROWS
    grid_spec = pltpu.PrefetchScalarGridSpec(
        num_scalar_prefetch=2,
        grid=(nblk,),
        in_specs=[pl.BlockSpec(memory_space=pl.ANY),
                  pl.BlockSpec((None, D_MODEL, D_EXPERT), lambda i, be, nv: (be[i], 0, 0)),
                  pl.BlockSpec((None, D_MODEL, D_EXPERT), lambda i, be, nv: (be[i], 0, 0)),
                  pl.BlockSpec((None, D_EXPERT, D_MODEL), lambda i, be, nv: (be[i], 0, 0))],
        out_specs=pl.BlockSpec((rows, D_MODEL), lambda i, be, nv: (i, 0)),
        scratch_shapes=[pltpu.VMEM((2, rows // ROW_GROUP, ROW_GROUP, D_MODEL), F32),
                        pltpu.VMEM((D_MODEL, D_EXPERT), BF16), pltpu.VMEM((D_MODEL, D_EXPERT), BF16),
                        pltpu.VMEM((D_EXPERT, D_MODEL), BF16),
                        pltpu.SemaphoreType.DMA((2,))],
    )
    return pl.pallas_call(
        _expert_kernel,
        grid_spec=grid_spec,
        out_shape=jax.ShapeDtypeStruct((nblk * rows, D_MODEL), F32),
        compiler_params=_cparams(("arbitrary",)),
        name="experts",
    )(blk_exp, nvalid, hs, w_gate, w_up, w_down)


def _combine_kernel(dcur_ref, dnxt_ref, x1_ref, meta_ref, ys_hbm, op_ref, os_ref, gbuf, sem, *, nb_prompt):
    i = pl.program_id(0)
    last = pl.num_programs(0) - 1
    slot = i & 1
    tm = x1_ref.shape[0]

    def start_gather(d_ref, s):
        def rows8(g, _):
            for k in range(ROW_GROUP):
                r = g * ROW_GROUP + k
                for e in range(2):
                    pltpu.make_async_copy(ys_hbm.at[pl.ds(d_ref[0, 0, 2 * r + e], 1), :],
                                          gbuf.at[s, e, pl.ds(r, 1), :], sem.at[s]).start(priority=e)
            return 0

        lax.fori_loop(0, tm // ROW_GROUP, rows8, 0)

    @pl.when(i == 0)
    def _():
        start_gather(dcur_ref, 0)

    @pl.when(i < last)
    def _():
        start_gather(dnxt_ref, 1 - slot)

    for e in range(2):
        pltpu.make_async_copy(ys_hbm.at[pl.ds(0, tm), :], gbuf.at[slot, e], sem.at[slot]).wait()
    meta = meta_ref[...]
    y = x1_ref[...] + meta[:, 4:5] * gbuf[slot, 0] + meta[:, 5:6] * gbuf[slot, 1]

    @pl.when(i < nb_prompt)
    def _():
        op_ref[...] = y

    @pl.when(i >= nb_prompt)
    def _():
        os_ref[...] = y


def _combine(x1, ys, meta, dest, t_prompt):
    t_all = x1.shape[0]
    tm = MERGE_ROWS
    nb = t_all // tm
    nbp = t_prompt // tm
    dest = dest.reshape(nb, 1, 2 * tm)
    grid_spec = pl.GridSpec(
        grid=(nb,),
        in_specs=[pl.BlockSpec((1, 1, 2 * tm), lambda i: (i, 0, 0), memory_space=pltpu.SMEM),
                  pl.BlockSpec((1, 1, 2 * tm), lambda i: (jnp.minimum(i + 1, nb - 1), 0, 0),
                               memory_space=pltpu.SMEM),
                  pl.BlockSpec((tm, D_MODEL), lambda i: (i, 0)),
                  pl.BlockSpec((tm, LANES), lambda i: (i, 0)),
                  pl.BlockSpec(memory_space=pl.ANY)],
        out_specs=[pl.BlockSpec((tm, D_MODEL), lambda i: (jnp.minimum(i, nbp - 1), 0)),
                   pl.BlockSpec((tm, D_MODEL), lambda i: (jnp.maximum(i - nbp, 0), 0))],
        scratch_shapes=[pltpu.VMEM((2, 2, tm, D_MODEL), F32), pltpu.SemaphoreType.DMA((2,))],
    )
    return pl.pallas_call(
        functools.partial(_combine_kernel, nb_prompt=nbp),
        grid_spec=grid_spec,
        out_shape=[jax.ShapeDtypeStruct((t_prompt, D_MODEL), F32),
                   jax.ShapeDtypeStruct((t_all - t_prompt, D_MODEL), F32)],
        compiler_params=_cparams(("arbitrary",)),
        name="combine",
    )(dest, dest, x1, meta, ys)


def _prep_weights(norm_mix, w_in, g_q, g_k, w_pool, pool_scale, w_out, norm_ffn, w_group, w_expert):
    lane = jnp.arange(ATTN_WIDTH)
    hsum = (lane[:, None] // HEAD_DIM == lane[None, :] // HEAD_DIM).astype(BF16)
    w_router = jnp.zeros((D_MODEL, LANES), F32)
    w_router = w_router.at[:, 0:N_GROUPS].set(w_group).at[:, N_GROUPS:N_GROUPS + N_EXPERTS].set(w_expert)
    w_router_hi = w_router.astype(BF16)
    w_router_lo = (w_router - w_router_hi.astype(F32)).astype(BF16)
    r = jnp.arange(MERGE_ROWS)
    return {
        "nrm": norm_mix.reshape(1, D_MODEL),
        "w_in": w_in.astype(BF16),
        "gq": (jnp.tile(g_q, N_HEADS) * (HEAD_DIM ** -0.5 * LOG2E)).reshape(1, ATTN_WIDTH),
        "gk": jnp.tile(g_k, N_HEADS).reshape(1, ATTN_WIDTH),
        "hsum": hsum,
        "w_pool": w_pool.astype(BF16),
        "pool_scale": pool_scale.reshape(1, POOL_WIDTH),
        "w_out_pool": w_out[:POOL_WIDTH].astype(BF16),
        "w_out_attn": w_out[POOL_WIDTH:].astype(BF16),
        "nrm_ffn": norm_ffn.reshape(1, D_MODEL),
        "w_router_hi": w_router_hi,
        "w_router_lo": w_router_lo,
        "tri_merge": (r[None, :] < r[:, None]).astype(BF16),
    }


def _later_sums(n_keys, n_heads):
    idx = jnp.arange(n_heads * n_keys)
    same = idx[:, None] // n_keys == idx[None, :] // n_keys
    later = same & (idx[:, None] > idx[None, :])
    return jnp.concatenate([later, same], axis=1).astype(BF16)


def kernel(x_prompt, x_sample, cache_k, cache_v, state_pool, page_table, meta_tokens, norm_mix, w_in, g_q, g_k,
           sb_bias, w_pool, pool_scale, w_out, norm_ffn, w_group, w_expert, w_gate, w_up, w_down):
    nb, seq, _ = x_prompt.shape
    nseq, nt, _ = x_sample.shape
    n_phys = cache_k.shape[1]
    t_prompt = nb * seq
    t_sample = nseq * nt
    t_all = t_prompt + t_sample

    cw = _prep_weights(norm_mix[0], w_in[0], g_q[0], g_k[0], w_pool[0], pool_scale[0], w_out[0], norm_ffn[0],
                       w_group[0], w_expert[0])
    bias = sb_bias[0].astype(F32) * LOG2E

    zero_hist = jnp.zeros((HIST_ROWS, POOL_WIDTH), F32)
    _, k_m, v_m, kx_m, vx_m, _, tail_m = _proj_seq(meta_tokens[None], zero_hist, cw, tm=N_META, p0=0,
                                                   full_count=False)
    pad_rows = ((0, ATT_BLOCK - N_META), (0, 0))
    kxm = jnp.pad(kx_m[0], pad_rows)
    vxm = jnp.pad(vx_m[0], pad_rows)

    q_p, k_p, v_p, kx_p, vx_p, po_p, tail_p = _proj_seq(x_prompt, tail_m[0], cw, tm=PROJ_ROWS, p0=N_META,
                                                        full_count=True)
    bias_rows = jnp.repeat(bias, ATT_BLOCK)[None]
    attn_p = _attn_prompt(q_p, kx_p, vx_p, kxm, vxm, bias_rows, _later_sums(ATT_BLOCK, 2))

    x_s_tm = jnp.transpose(x_sample, (1, 0, 2)).reshape(t_sample, D_MODEL)
    hist_tm = jnp.transpose(state_pool[0], (1, 0, 2))
    q_s, k_s, v_s, po_s, tail_s = _proj_step(x_s_tm, hist_tm, cw, nseq=nseq, nt=nt)
    to_seq = lambda a: jnp.transpose(a.reshape(nt, nseq, a.shape[-1]), (1, 0, 2))
    q_sb, k_sb, v_sb, po_sb = to_seq(q_s), to_seq(k_s), to_seq(v_s), to_seq(po_s)
    head_of_lane = jnp.arange(ATTN_WIDTH) // HEAD_DIM
    head_mask = head_of_lane[None, :] == jnp.arange(N_HEADS)[:, None]
    qbd = jnp.where(head_mask[None, None], q_sb[:, :, None, :], 0).reshape(nseq, nt * N_HEADS, ATTN_WIDTH)
    pad_new = ((0, 0), (0, 8 - nt), (0, 0))
    kn = jnp.pad(k_sb, pad_new)
    vn = jnp.pad(v_sb, pad_new)
    bias_col = jnp.tile(bias, nt).reshape(nt * N_HEADS, 1)
    page_t = lambda c: jnp.transpose(c[0], (0, 2, 3, 1)).reshape(n_phys, ATTN_WIDTH, PAGE)
    attn_s = _attn_sample(page_table, qbd, kn, vn, bias_col, _later_sums(PAGE, 1), page_t(cache_k), page_t(cache_v))

    x1, h_all, meta, counts_f = _merge_router(
        x_prompt.reshape(t_prompt, D_MODEL), x_sample.reshape(t_sample, D_MODEL),
        po_p.reshape(t_prompt, POOL_WIDTH), po_sb.reshape(t_sample, POOL_WIDTH),
        attn_p.reshape(t_prompt, ATTN_WIDTH), attn_s.reshape(t_sample, ATTN_WIDTH).astype(BF16), cw)

    pair_exp = meta[:, 0:2].astype(jnp.int32)
    pair_rank = meta[:, 2:4].astype(jnp.int32)
    counts = counts_f[0, :N_EXPERTS].astype(jnp.int32)
    rows = EXPERT_ROWS
    padded = (counts + rows - 1) // rows * rows
    pad_end = jnp.cumsum(padded)
    pad_start = pad_end - padded
    nblk = -(-(2 * t_all) // rows) + N_EXPERTS
    of_expert = pair_exp[..., None] == jnp.arange(N_EXPERTS, dtype=jnp.int32)
    dest = jnp.sum(jnp.where(of_expert, pad_start, 0), axis=-1) + pair_rank
    blk_row0 = jnp.arange(nblk, dtype=jnp.int32) * rows
    blk_exp = jnp.minimum(jnp.sum(blk_row0[:, None] >= pad_end[None, :], axis=1), N_EXPERTS - 1).astype(jnp.int32)
    blk_of = blk_exp[:, None] == jnp.arange(N_EXPERTS, dtype=jnp.int32)
    blk_count = jnp.sum(jnp.where(blk_of, counts, 0), axis=1)
    blk_start = jnp.sum(jnp.where(blk_of, pad_start, 0), axis=1)
    n_rows = jnp.clip(blk_count - (blk_row0 - blk_start), 0, rows)
    n_rows = jnp.where(blk_row0 < pad_end[-1], n_rows, 0).astype(jnp.int32)
    n_tiled = (n_rows + ROW_GROUP - 1) // ROW_GROUP * ROW_GROUP
    fills = (blk_row0 + n_rows, n_tiled - n_rows, (blk_row0 + n_tiled) // ROW_GROUP, (rows - n_tiled) // ROW_GROUP)

    hs = _dispatch(fills, dest, h_all, nblk * rows)
    ys = _experts(blk_exp, n_tiled, hs, w_gate[0], w_up[0], w_down[0])
    y_prompt, y_sample = _combine(x1, ys, meta, dest, t_prompt)
    y_prompt = y_prompt.reshape(nb, seq, D_MODEL)
    y_sample = y_sample.reshape(nseq, nt, D_MODEL)

    heads = lambda a: a.reshape(a.shape[:-1] + (N_HEADS, HEAD_DIM))
    k_prompt = jnp.concatenate([jnp.broadcast_to(k_m, (nb, N_META, ATTN_WIDTH)), k_p], axis=1)
    v_prompt = jnp.concatenate([jnp.broadcast_to(v_m, (nb, N_META, ATTN_WIDTH)), v_p], axis=1)
    pool_sample = jnp.transpose(tail_s, (1, 0, 2))
    return (y_prompt, y_sample, heads(k_prompt)[None], heads(v_prompt)[None], tail_p[:, 1:][None],
            heads(k_sb)[None], heads(v_sb)[None], pool_sample[None])
```

```python
import functools

import jax
import jax.numpy as jnp
from jax import lax
from jax.experimental import pallas as pl
from jax.experimental.pallas import tpu as pltpu

F32 = jnp.float32
BF16 = jnp.bfloat16

D_MODEL = 1024
POOL_WIDTH = 512
ATTN_WIDTH = 512
HEAD_DIM = 64
N_HEADS = 8
N_PAIRS = N_HEADS // 2
POOL_WINDOWS = (2, 4, 8, 16)
POOL_GROUP = 128
POOL_HIST = 15
HIST_ROWS = 16
N_META = 16
PAGE = 128
N_GROUPS = 4
EXPERTS_PER_GROUP = 8
N_EXPERTS = 32
D_EXPERT = 512
EPS = 1e-6
NEG = -1e30
LOG2E = 1.4426950408889634
LOGIT_CAP = 100.0
LANES = 128
ATT_BLOCK = 128
ATT_UNROLL = 4
PROJ_ROWS = 512
MERGE_ROWS = 256
EXPERT_ROWS = 512
ROW_GROUP = 8
VMEM_LIMIT = 56 * 1024 * 1024


def _cparams(sem, vmem=VMEM_LIMIT):
    return pltpu.CompilerParams(dimension_semantics=sem, vmem_limit_bytes=vmem)


def _rms_rows(x, g):
    ms = jnp.mean(x * x, axis=-1, keepdims=True)
    return (x * lax.rsqrt(ms + EPS)) * g


def _head_norm(a, g, hsum):
    ssum = jnp.dot((a * a).astype(BF16), hsum, preferred_element_type=F32)
    return (a * lax.rsqrt(ssum * (1.0 / HEAD_DIM) + EPS)) * g


def _project(x, nrm, w_in, gq, gk, hsum):
    n = _rms_rows(x, nrm)
    proj = jnp.dot(n.astype(BF16), w_in, preferred_element_type=F32)
    u = proj[:, 0:POOL_WIDTH]
    q = _head_norm(proj[:, POOL_WIDTH:POOL_WIDTH + ATTN_WIDTH], gq, hsum)
    k = _head_norm(proj[:, POOL_WIDTH + ATTN_WIDTH:POOL_WIDTH + 2 * ATTN_WIDTH], gk, hsum)
    v = proj[:, POOL_WIDTH + 2 * ATTN_WIDTH:]
    return u, q, k, v


def _split_heads(a, out_ref):
    rows = a.shape[0]
    lane = lax.broadcasted_iota(jnp.int32, (rows, LANES), 1)
    first = lane < HEAD_DIM
    for p in range(N_PAIRS):
        a2 = a[:, p * LANES:(p + 1) * LANES]
        out_ref[:, 2 * p * LANES:(2 * p + 1) * LANES] = jnp.where(first, a2, 0.0).astype(out_ref.dtype)
        out_ref[:, (2 * p + 1) * LANES:(2 * p + 2) * LANES] = jnp.where(first, 0.0, a2).astype(out_ref.dtype)


def _log2_gates(z, visible=None):
    z = lax.clamp(-LOGIT_CAP, z, LOGIT_CAP)
    if visible is not None:
        z = jnp.where(visible, z, NEG)
    drop = jnp.log(1.0 + jnp.exp2(z)) * LOG2E
    return z - drop, drop


def _stick_step(tt, gate_in, mix_in):
    def stack_heads(b):
        half = b.shape[1] // 2
        return jnp.concatenate([b[:, :half], b[:, half:]], axis=0)

    nt_dims = (((1,), (1,)), ((), ()))
    gates = mixed = None
    if mix_in is not None:
        log_betas, drops, carries, vxbs = mix_in
        sums = [jnp.dot(drop, tt, preferred_element_type=F32) for drop in drops]
    if gate_in is not None:
        q2s, kxbs, brows, mask = gate_in
        zs = [lax.dot_general(q2, stack_heads(kxb), nt_dims, preferred_element_type=F32) + brow
              for q2, kxb, brow in zip(q2s, kxbs, brows)]
    if mix_in is not None:
        nk = sums[0].shape[1] // 2
        probs = [jnp.exp2(lb - s[:, :nk] - carry).astype(BF16) for lb, s, carry in zip(log_betas, sums, carries)]
    if gate_in is not None:
        gates = [(lb, lk.astype(BF16)) for lb, lk in (_log2_gates(z, mask) for z in zs)]
    if mix_in is not None:
        outs = [jnp.dot(a, stack_heads(vxb), preferred_element_type=F32) for a, vxb in zip(probs, vxbs)]
        mixed = [(out, carry + s[:, nk:]) for out, carry, s in zip(outs, carries, sums)]
    return gates, mixed


def _proj_seq_kernel(x_ref, hist_ref, nrm_ref, win_ref, gq_ref, gk_ref, hsum_ref, wpool_ref, ps_ref,
                     q_ref, k_ref, v_ref, kx_ref, vx_ref, po_ref, tail_ref, ext_ref, *, tm, p0, full_count):
    blk = pl.program_id(1)
    u, q, k, v = _project(x_ref[...], nrm_ref[...], win_ref[...], gq_ref[...], gk_ref[...], hsum_ref[...])
    q_ref[...] = q.astype(q_ref.dtype)
    k_ref[...] = k
    v_ref[...] = v
    _split_heads(k, kx_ref)
    _split_heads(v, vx_ref)

    @pl.when(blk == 0)
    def _():
        ext_ref[0:HIST_ROWS, :] = hist_ref[...]

    ext_ref[HIST_ROWS:HIST_ROWS + tm, :] = u
    for g, w in enumerate(POOL_WINDOWS):
        e = ext_ref[:, g * POOL_GROUP:(g + 1) * POOL_GROUP]
        s = e
        sh = 1
        while sh < w:
            s = s + pltpu.roll(s, sh, axis=0)
            sh *= 2
        win = s[HIST_ROWS:, :]
        tok = e[HIST_ROWS:, :]
        if full_count:
            d = win * (1.0 / w) - tok
        else:
            pos = p0 + lax.broadcasted_iota(jnp.int32, (tm, 1), 0)
            cnt = jnp.minimum(w, pos + 1).astype(F32)
            d = win / cnt - tok
        y = jnp.dot(d.astype(BF16), wpool_ref[g], preferred_element_type=F32)
        y = y * ps_ref[:, g * POOL_GROUP:(g + 1) * POOL_GROUP]
        po_ref[:, g * POOL_GROUP:(g + 1) * POOL_GROUP] = y.astype(po_ref.dtype)
    last = ext_ref[tm:tm + HIST_ROWS, :]
    ext_ref[0:HIST_ROWS, :] = last

    @pl.when(blk == pl.num_programs(1) - 1)
    def _():
        tail_ref[...] = last


def _proj_seq(x, hist, cw, *, tm, p0, full_count):
    b, l, _ = x.shape
    nb = l // tm
    const = lambda shape: pl.BlockSpec(shape, lambda i, j: (0,) * len(shape))
    row = lambda width: pl.BlockSpec((None, tm, width), lambda i, j: (i, j, 0))
    kern = functools.partial(_proj_seq_kernel, tm=tm, p0=p0, full_count=full_count)
    return pl.pallas_call(
        kern,
        grid=(b, nb),
        in_specs=[row(D_MODEL), const((HIST_ROWS, POOL_WIDTH)), const((1, D_MODEL)),
                  const((D_MODEL, 2 * D_MODEL)), const((1, ATTN_WIDTH)), const((1, ATTN_WIDTH)),
                  const((ATTN_WIDTH, ATTN_WIDTH)), const((4, POOL_GROUP, POOL_GROUP)), const((1, POOL_WIDTH))],
        out_specs=[row(ATTN_WIDTH), row(ATTN_WIDTH), row(ATTN_WIDTH), row(2 * ATTN_WIDTH), row(2 * ATTN_WIDTH),
                   row(POOL_WIDTH), pl.BlockSpec((None, HIST_ROWS, POOL_WIDTH), lambda i, j: (i, 0, 0))],
        out_shape=[jax.ShapeDtypeStruct((b, l, ATTN_WIDTH), BF16),
                   jax.ShapeDtypeStruct((b, l, ATTN_WIDTH), F32),
                   jax.ShapeDtypeStruct((b, l, ATTN_WIDTH), F32),
                   jax.ShapeDtypeStruct((b, l, 2 * ATTN_WIDTH), BF16),
                   jax.ShapeDtypeStruct((b, l, 2 * ATTN_WIDTH), BF16),
                   jax.ShapeDtypeStruct((b, l, POOL_WIDTH), BF16),
                   jax.ShapeDtypeStruct((b, HIST_ROWS, POOL_WIDTH), F32)],
        scratch_shapes=[pltpu.VMEM((HIST_ROWS + tm, POOL_WIDTH), F32)],
        compiler_params=_cparams(("arbitrary", "arbitrary")),
        name="proj_seq",
    )(x, hist, cw["nrm"], cw["w_in"], cw["gq"], cw["gk"], cw["hsum"], cw["w_pool"], cw["pool_scale"])


def _proj_step_kernel(x_ref, hist_ref, nrm_ref, win_ref, gq_ref, gk_ref, hsum_ref, wpool_ref, ps_ref,
                      q_ref, k_ref, v_ref, po_ref, tail_ref, *, nseq, nt):
    u, q, k, v = _project(x_ref[...], nrm_ref[...], win_ref[...], gq_ref[...], gk_ref[...], hsum_ref[...])
    q_ref[...] = q.astype(q_ref.dtype)
    k_ref[...] = k
    v_ref[...] = v
    ext = [hist_ref[r] for r in range(POOL_HIST)] + [u[t * nseq:(t + 1) * nseq, :] for t in range(nt)]
    for r in range(POOL_HIST):
        tail_ref[r] = ext[nt + r]
    for g, w in enumerate(POOL_WINDOWS):
        sl = slice(g * POOL_GROUP, (g + 1) * POOL_GROUP)
        ds = []
        for t in range(nt):
            win = ext[POOL_HIST + t][:, sl]
            for i in range(1, w):
                win = win + ext[POOL_HIST + t - i][:, sl]
            ds.append(win * (1.0 / w) - ext[POOL_HIST + t][:, sl])
        d = jnp.concatenate(ds, axis=0)
        y = jnp.dot(d.astype(BF16), wpool_ref[g], preferred_element_type=F32) * ps_ref[:, sl]
        po_ref[:, sl] = y.astype(po_ref.dtype)


def _proj_step(x_tm, hist_tm, cw, *, nseq, nt):
    rows = nseq * nt
    kern = functools.partial(_proj_step_kernel, nseq=nseq, nt=nt)
    return pl.pallas_call(
        kern,
        out_shape=[jax.ShapeDtypeStruct((rows, ATTN_WIDTH), BF16),
                   jax.ShapeDtypeStruct((rows, ATTN_WIDTH), F32),
                   jax.ShapeDtypeStruct((rows, ATTN_WIDTH), F32),
                   jax.ShapeDtypeStruct((rows, POOL_WIDTH), BF16),
                   jax.ShapeDtypeStruct((POOL_HIST, nseq, POOL_WIDTH), F32)],
        compiler_params=pltpu.CompilerParams(vmem_limit_bytes=VMEM_LIMIT),
        name="proj_step",
    )(x_tm, hist_tm, cw["nrm"], cw["w_in"], cw["gq"], cw["gk"], cw["hsum"], cw["w_pool"], cw["pool_scale"])


def _attn_prompt_kernel(q_ref, kx_ref, vx_ref, kxm_ref, vxm_ref, bias_ref, tt_ref, o_ref,
                        carry_ref, acc_ref, lb_ref, lk_ref):
    qi = pl.program_id(1)
    tt = tt_ref[...]
    nq = ATT_BLOCK
    pairs = range(N_PAIRS)
    lanes = [slice(p * LANES, (p + 1) * LANES) for p in pairs]
    wide = [slice(2 * p * LANES, 2 * (p + 1) * LANES) for p in pairs]

    def rows(j):
        return pl.ds(pl.multiple_of(j * ATT_BLOCK, ATT_BLOCK), ATT_BLOCK)

    def gate_in(k_of, mask):
        return [q_ref[:, l] for l in lanes], [k_of(w) for w in wide], [bias_ref[:, w] for w in wide], mask

    def pending():
        return [lb_ref[p] for p in pairs], [lk_ref[p] for p in pairs]

    def keep(gates):
        lbs, drops = gates
        for p in pairs:
            lb_ref[p] = lbs[p]
            lk_ref[p] = drops[p]

    def step(gin, gates_prev, v_of=None, last=False):
        min_ = accs = None
        if gates_prev is not None:
            lbs, drops = gates_prev
            min_ = (lbs, drops, [carry_ref[p] for p in pairs], [v_of(w) for w in wide])
            accs = [acc_ref[:, l] for l in lanes]
        gates, mixed = _stick_step(tt, gin, min_)
        if mixed is not None:
            for p, (out, carry) in enumerate(mixed):
                if last:
                    o_ref[:, lanes[p]] = (accs[p] + out).astype(o_ref.dtype)
                else:
                    carry_ref[p] = carry
                    acc_ref[:, lanes[p]] = accs[p] + out
        return None if gates is None else ([g[0] for g in gates], [g[1] for g in gates])

    def token_tile(j):
        return gate_in(lambda w: kx_ref[rows(j), w], None)

    carry_ref[...] = jnp.zeros_like(carry_ref)
    acc_ref[...] = jnp.zeros_like(acc_ref)
    r_i = lax.broadcasted_iota(jnp.int32, (nq, 2 * ATT_BLOCK), 0)
    c_i = lax.broadcasted_iota(jnp.int32, (nq, 2 * ATT_BLOCK), 1) & (ATT_BLOCK - 1)
    keep(step(gate_in(lambda w: kx_ref[rows(qi), w], c_i < r_i), None))

    def tiles_from(j, n):
        g = pending()
        for k in range(n):
            g = step(token_tile(j - k), g, lambda w, k=k: vx_ref[rows(j - k + 1), w])
        keep(g)

    def unrolled_trip(t, _):
        tiles_from(qi - 1 - ATT_UNROLL * t, ATT_UNROLL)
        return 0

    n_trips = qi // ATT_UNROLL
    lax.fori_loop(0, n_trips, unrolled_trip, 0)

    def single_trip(t, _):
        tiles_from(qi - 1 - ATT_UNROLL * n_trips - t, 1)
        return 0

    lax.fori_loop(0, qi - ATT_UNROLL * n_trips, single_trip, 0)

    g = step(gate_in(lambda w: kxm_ref[:, w], c_i < N_META), pending(), lambda w: vx_ref[0:ATT_BLOCK, w])
    step(None, g, lambda w: vxm_ref[:, w], last=True)


def _attn_prompt(q, kx, vx, kxm, vxm, bias_rows, tt):
    b, l, _ = q.shape
    nq = l // ATT_BLOCK
    once = pl.Buffered(1)
    return pl.pallas_call(
        _attn_prompt_kernel,
        grid=(b, nq),
        in_specs=[pl.BlockSpec((None, ATT_BLOCK, ATTN_WIDTH), lambda i, j: (i, j, 0)),
                  pl.BlockSpec((None, l, 2 * ATTN_WIDTH), lambda i, j: (i, 0, 0), pipeline_mode=once),
                  pl.BlockSpec((None, l, 2 * ATTN_WIDTH), lambda i, j: (i, 0, 0), pipeline_mode=once),
                  pl.BlockSpec((ATT_BLOCK, 2 * ATTN_WIDTH), lambda i, j: (0, 0)),
                  pl.BlockSpec((ATT_BLOCK, 2 * ATTN_WIDTH), lambda i, j: (0, 0)),
                  pl.BlockSpec((1, 2 * ATTN_WIDTH), lambda i, j: (0, 0)),
                  pl.BlockSpec((2 * ATT_BLOCK, 4 * ATT_BLOCK), lambda i, j: (0, 0))],
        out_specs=pl.BlockSpec((None, ATT_BLOCK, ATTN_WIDTH), lambda i, j: (i, j, 0)),
        out_shape=jax.ShapeDtypeStruct((b, l, ATTN_WIDTH), BF16),
        scratch_shapes=[pltpu.VMEM((N_PAIRS, ATT_BLOCK, 2 * ATT_BLOCK), F32),
                        pltpu.VMEM((ATT_BLOCK, ATTN_WIDTH), F32),
                        pltpu.VMEM((N_PAIRS, ATT_BLOCK, 2 * ATT_BLOCK), F32),
                        pltpu.VMEM((N_PAIRS, ATT_BLOCK, 2 * ATT_BLOCK), BF16)],
        compiler_params=_cparams(("arbitrary", "arbitrary")),
        name="attn_prompt",
    )(q, kx, vx, kxm, vxm, bias_rows, tt)


def _attn_sample_kernel(pt_ref, qbd_ref, kn_ref, vn_ref, bias_ref, tt_ref, *refs, n_pages, nt):
    del pt_ref
    k_refs = refs[:n_pages]
    v_refs = refs[n_pages:2 * n_pages]
    o_ref = refs[2 * n_pages]
    qbd = qbd_ref[...]
    tt = tt_ref[...]
    bias = bias_ref[...]
    nc = qbd.shape[0]
    nn = (((1,), (0,)), ((), ()))
    nt_dims = (((1,), (1,)), ((), ()))

    pad = jnp.zeros((PAGE - kn_ref.shape[0], ATTN_WIDTH), F32)
    kn = jnp.concatenate([kn_ref[...], pad], axis=0).astype(BF16)
    vn = jnp.concatenate([vn_ref[...], pad], axis=0).astype(BF16)
    t_i = lax.broadcasted_iota(jnp.int32, (nc, PAGE), 0) >> 3
    k_i = lax.broadcasted_iota(jnp.int32, (nc, PAGE), 1)
    z_new = lax.dot_general(qbd, kn, nt_dims, preferred_element_type=F32) + bias
    zs = [z_new]
    for j in range(n_pages - 1, -1, -1):
        zs.append(lax.dot_general(qbd, k_refs[j][...].astype(BF16), nn, preferred_element_type=F32) + bias)
    gates = [_log2_gates(z, k_i < t_i if i == 0 else None) for i, z in enumerate(zs)]
    sums = [jnp.dot(lk.astype(BF16), tt, preferred_element_type=F32) for _, lk in gates]
    carry = jnp.zeros((nc, PAGE), F32)
    acc = None
    for i, ((lb, _), s) in enumerate(zip(gates, sums)):
        a = jnp.exp2(lb - s[:, :PAGE] - carry).astype(BF16)
        carry = carry + s[:, PAGE:]
        if i == 0:
            acc = jnp.dot(a, vn, preferred_element_type=F32)
        else:
            vt = v_refs[n_pages - i][...].astype(BF16)
            acc = acc + lax.dot_general(a, vt, nt_dims, preferred_element_type=F32)
    h_i = lax.broadcasted_iota(jnp.int32, (nc, ATTN_WIDTH), 0) & (N_HEADS - 1)
    l_i = lax.broadcasted_iota(jnp.int32, (nc, ATTN_WIDTH), 1) >> 6
    acc = jnp.where(h_i == l_i, acc, 0.0)
    for t in range(nt):
        o_ref[t:t + 1, :] = jnp.sum(acc[t * N_HEADS:(t + 1) * N_HEADS, :], axis=0, keepdims=True)


def _attn_sample(page_table, qbd, kn, vn, bias_col, tt, cache_k, cache_v):
    nseq, n_pages = page_table.shape
    nt = qbd.shape[1] // N_HEADS
    pt = page_table.reshape(-1)

    def page_spec(j):
        return pl.BlockSpec((None, ATTN_WIDTH, PAGE), lambda i, pt_ref: (pt_ref[i * n_pages + j], 0, 0))

    grid_spec = pltpu.PrefetchScalarGridSpec(
        num_scalar_prefetch=1,
        grid=(nseq,),
        in_specs=[pl.BlockSpec((None, nt * N_HEADS, ATTN_WIDTH), lambda i, pt_ref: (i, 0, 0)),
                  pl.BlockSpec((None, 8, ATTN_WIDTH), lambda i, pt_ref: (i, 0, 0)),
                  pl.BlockSpec((None, 8, ATTN_WIDTH), lambda i, pt_ref: (i, 0, 0)),
                  pl.BlockSpec((nt * N_HEADS, 1), lambda i, pt_ref: (0, 0)),
                  pl.BlockSpec((PAGE, 2 * PAGE), lambda i, pt_ref: (0, 0))]
                 + [page_spec(j) for j in range(n_pages)] + [page_spec(j) for j in range(n_pages)],
        out_specs=pl.BlockSpec((None, nt, ATTN_WIDTH), lambda i, pt_ref: (i, 0, 0)),
    )
    kern = functools.partial(_attn_sample_kernel, n_pages=n_pages, nt=nt)
    return pl.pallas_call(
        kern,
        grid_spec=grid_spec,
        out_shape=jax.ShapeDtypeStruct((nseq, nt, ATTN_WIDTH), F32),
        compiler_params=_cparams(("arbitrary",)),
        name="attn_sample",
    )(pt, qbd, kn, vn, bias_col, tt, *([cache_k] * n_pages), *([cache_v] * n_pages))


def _merge_router_kernel(xp_ref, xs_ref, pop_ref, pos_ref, atp_ref, ats_ref, wop_ref, woa_ref, nrm_ref, wrh_ref,
                         wrl_ref, tri_ref, x1_ref, h_ref, meta_ref, cout_ref, carry_ref, *, nb_prompt):
    i = pl.program_id(0)

    @pl.when(i == 0)
    def _():
        carry_ref[...] = jnp.zeros_like(carry_ref)

    is_p = i < nb_prompt
    x = jnp.where(is_p, xp_ref[...], xs_ref[...])
    po = jnp.where(is_p, pop_ref[...], pos_ref[...])
    at = jnp.where(is_p, atp_ref[...], ats_ref[...])
    y = (jnp.dot(po, wop_ref[...], preferred_element_type=F32)
         + jnp.dot(at, woa_ref[...], preferred_element_type=F32))
    x1 = x + y
    x1_ref[...] = x1
    h = _rms_rows(x1, nrm_ref[...])
    h_ref[...] = h
    hi = h.astype(BF16)
    lo = (h - hi.astype(F32)).astype(BF16)
    both = jnp.dot(hi, jnp.concatenate([wrh_ref[...], wrl_ref[...]], axis=1), preferred_element_type=F32)
    logits = both[:, :LANES] + both[:, LANES:] + jnp.dot(lo, wrh_ref[...], preferred_element_type=F32)
    tm = logits.shape[0]
    lane = lax.broadcasted_iota(jnp.int32, (tm, LANES), 1).astype(F32)
    ninf = -jnp.inf
    is_g = lane < N_GROUPS
    gl = jnp.where(is_g, logits, ninf)
    gmax = jnp.max(gl, axis=-1, keepdims=True)
    g_sel = jnp.min(jnp.where(gl == gmax, lane, float(LANES)), axis=-1, keepdims=True)
    psum = jnp.sum(jnp.where(is_g, jnp.exp(logits - gmax), 0.0), axis=-1, keepdims=True)
    g_w = 1.0 / psum
    e_lo = N_GROUPS + g_sel * EXPERTS_PER_GROUP
    in_group = (lane >= e_lo) & (lane < e_lo + EXPERTS_PER_GROUP)
    el = jnp.where(in_group, logits, ninf)
    v1 = jnp.max(el, axis=-1, keepdims=True)
    i1 = jnp.min(jnp.where(el == v1, lane, float(LANES)), axis=-1, keepdims=True)
    el2 = jnp.where(lane == i1, ninf, el)
    v2 = jnp.max(el2, axis=-1, keepdims=True)
    i2 = jnp.min(jnp.where(el2 == v2, lane, float(LANES)), axis=-1, keepdims=True)
    e21 = jnp.exp(v2 - v1)
    den = 1.0 + e21
    gate1 = g_w * (1.0 / den)
    gate2 = g_w * (e21 / den)
    e1 = i1 - N_GROUPS
    e2 = i2 - N_GROUPS
    onehot = jnp.where((lane == e1) | (lane == e2), 1.0, 0.0)
    before = jnp.dot(tri_ref[...], onehot.astype(BF16), preferred_element_type=F32) + carry_ref[...]
    rank1 = jnp.sum(jnp.where(lane == e1, before, 0.0), axis=-1, keepdims=True)
    rank2 = jnp.sum(jnp.where(lane == e2, before, 0.0), axis=-1, keepdims=True)
    carry = carry_ref[...] + jnp.sum(onehot, axis=0, keepdims=True)
    carry_ref[...] = carry
    cout_ref[...] = carry
    meta = jnp.where(lane == 0, e1, 0.0)
    meta = jnp.where(lane == 1, e2, meta)
    meta = jnp.where(lane == 2, rank1, meta)
    meta = jnp.where(lane == 3, rank2, meta)
    meta = jnp.where(lane == 4, gate1, meta)
    meta = jnp.where(lane == 5, gate2, meta)
    meta_ref[...] = meta


def _merge_router(x_p, x_s, po_p, po_s, at_p, at_s, cw):
    tm = MERGE_ROWS
    nbp = x_p.shape[0] // tm
    nbs = x_s.shape[0] // tm
    const = lambda shape: pl.BlockSpec(shape, lambda i: (0,) * len(shape))
    prow = lambda width: pl.BlockSpec((tm, width), lambda i: (jnp.minimum(i, nbp - 1), 0))
    srow = lambda width: pl.BlockSpec((tm, width), lambda i: (jnp.maximum(i - nbp, 0), 0))
    row = lambda width: pl.BlockSpec((tm, width), lambda i: (i, 0))
    t_all = (nbp + nbs) * tm
    return pl.pallas_call(
        functools.partial(_merge_router_kernel, nb_prompt=nbp),
        grid=(nbp + nbs,),
        in_specs=[prow(D_MODEL), srow(D_MODEL), prow(POOL_WIDTH), srow(POOL_WIDTH), prow(ATTN_WIDTH),
                  srow(ATTN_WIDTH), const((POOL_WIDTH, D_MODEL)), const((ATTN_WIDTH, D_MODEL)),
                  const((1, D_MODEL)), const((D_MODEL, LANES)), const((D_MODEL, LANES)), const((tm, tm))],
        out_specs=[row(D_MODEL), row(D_MODEL), row(LANES), const((1, LANES))],
        out_shape=[jax.ShapeDtypeStruct((t_all, D_MODEL), F32),
                   jax.ShapeDtypeStruct((t_all, D_MODEL), F32),
                   jax.ShapeDtypeStruct((t_all, LANES), F32),
                   jax.ShapeDtypeStruct((1, LANES), F32)],
        scratch_shapes=[pltpu.VMEM((1, LANES), F32)],
        compiler_params=_cparams(("arbitrary",)),
        name="merge_router",
    )(x_p, x_s, po_p, po_s, at_p, at_s, cw["w_out_pool"], cw["w_out_attn"], cw["nrm_ffn"], cw["w_router_hi"],
      cw["w_router_lo"], cw["tri_merge"])


def _dispatch_kernel(fill_r0_ref, fill_nr_ref, fill_g0_ref, fill_ng_ref, dest_ref, h_ref, hs_hbm, zbuf, sem):
    i = pl.program_id(0)
    tm = h_ref.shape[0]

    def row_of(r):
        return hs_hbm.at[r >> 3, pl.ds(r & (ROW_GROUP - 1), 1), :]

    @pl.when(i == 0)
    def _():
        zbuf[...] = jnp.zeros_like(zbuf)

        def fill(wait):
            def body(blk, _):
                for k in range(ROW_GROUP - 1):
                    @pl.when(k < fill_nr_ref[blk])
                    def _():
                        cp = pltpu.make_async_copy(zbuf.at[0, pl.ds(0, 1), :], row_of(fill_r0_ref[blk] + k), sem)
                        cp.wait() if wait else cp.start()

                ng = fill_ng_ref[blk]

                @pl.when(ng > 0)
                def _():
                    cp = pltpu.make_async_copy(zbuf.at[pl.ds(0, ng)], hs_hbm.at[pl.ds(fill_g0_ref[blk], ng)], sem)
                    cp.wait() if wait else cp.start()
                return 0

            lax.fori_loop(0, fill_r0_ref.shape[0], body, 0)

        fill(False)
        fill(True)

    def rows8(g, _):
        for k in range(ROW_GROUP):
            r = g * ROW_GROUP + k
            for e in range(2):
                pltpu.make_async_copy(h_ref.at[pl.ds(r, 1), :], row_of(dest_ref[0, 0, 2 * r + e]),
                                      sem).start(priority=e)
        return 0

    lax.fori_loop(0, tm // ROW_GROUP, rows8, 0)
    for _ in range(2):
        pltpu.make_async_copy(zbuf, hs_hbm.at[pl.ds(0, tm // ROW_GROUP)], sem).wait()


def _dispatch(fills, dest, h_all, n_sorted):
    t_all = h_all.shape[0]
    tm = EXPERT_ROWS
    grid_spec = pltpu.PrefetchScalarGridSpec(
        num_scalar_prefetch=4,
        grid=(t_all // tm,),
        in_specs=[pl.BlockSpec((1, 1, 2 * tm), lambda i, *_: (i, 0, 0), memory_space=pltpu.SMEM),
                  pl.BlockSpec((tm, D_MODEL), lambda i, *_: (i, 0))],
        out_specs=pl.BlockSpec(memory_space=pl.ANY),
        scratch_shapes=[pltpu.VMEM((tm // ROW_GROUP, ROW_GROUP, D_MODEL), F32), pltpu.SemaphoreType.DMA(())],
    )
    return pl.pallas_call(
        _dispatch_kernel,
        grid_spec=grid_spec,
        out_shape=jax.ShapeDtypeStruct((n_sorted // ROW_GROUP, ROW_GROUP, D_MODEL), F32),
        compiler_params=_cparams(("arbitrary",)),
        name="dispatch",
    )(*fills, dest.reshape(t_all // tm, 1, 2 * tm), h_all)


def _expert_kernel(bexp_ref, nval_ref, hs_hbm, wg_ref, wu_ref, wd_ref, ys_ref, xbuf, wgb, wub, wdb, sem):
    i = pl.program_id(0)
    last = pl.num_programs(0) - 1
    slot = i & 1
    groups = xbuf.shape[1]
    g_cur = lax.shift_right_logical(nval_ref[i], 3)
    g_next = lax.shift_right_logical(nval_ref[jnp.minimum(i + 1, last)], 3)

    def fetch(blk, s, ng):
        return pltpu.make_async_copy(hs_hbm.at[pl.ds(blk * groups, ng)], xbuf.at[s, pl.ds(0, ng)], sem.at[s])

    @pl.when(i == 0)
    def _():
        xbuf[...] = jnp.zeros_like(xbuf)

        @pl.when(g_cur > 0)
        def _():
            fetch(0, 0, g_cur).start()

    @pl.when((i < last) & (g_next > 0))
    def _():
        fetch(i + 1, 1 - slot, g_next).start()

    @pl.when(g_cur > 0)
    def _():
        changed = jnp.logical_or(i == 0, bexp_ref[i] != bexp_ref[jnp.maximum(i - 1, 0)])

        @pl.when(changed)
        def _():
            wgb[...] = wg_ref[...].astype(BF16)
            wub[...] = wu_ref[...].astype(BF16)
            wdb[...] = wd_ref[...].astype(BF16)

        fetch(i, slot, g_cur).wait()
        xb = xbuf[slot].reshape(groups * ROW_GROUP, D_MODEL).astype(BF16)
        g = jnp.dot(xb, wgb[...], preferred_element_type=F32)
        u = jnp.dot(xb, wub[...], preferred_element_type=F32)
        mid = (g * jax.nn.sigmoid(g)) * u
        ys_ref[...] = jnp.dot(mid.astype(BF16), wdb[...], preferred_element_type=F32)

    @pl.when(g_cur == 0)
    def _():
        ys_ref[...] = jnp.zeros_like(ys_ref)


def _experts(blk_exp, nvalid, hs, w_gate, w_up, w_down):
    nblk = blk_exp.shape[0]
    rows = EXPERT_ROWS
    grid_spec = pltpu.PrefetchScalarGridSpec(
        num_scalar_prefetch=2,
        grid=(nblk,),
        in_specs=[pl.BlockSpec(memory_space=pl.ANY),
                  pl.BlockSpec((None, D_MODEL, D_EXPERT), lambda i, be, nv: (be[i], 0, 0)),
                  pl.BlockSpec((None, D_MODEL, D_EXPERT), lambda i, be, nv: (be[i], 0, 0)),
                  pl.BlockSpec((None, D_EXPERT, D_MODEL), lambda i, be, nv: (be[i], 0, 0))],
        out_specs=pl.BlockSpec((rows, D_MODEL), lambda i, be, nv: (i, 0)),
        scratch_shapes=[pltpu.VMEM((2, rows // ROW_GROUP, ROW_GROUP, D_MODEL), F32),
                        pltpu.VMEM((D_MODEL, D_EXPERT), BF16), pltpu.VMEM((D_MODEL, D_EXPERT), BF16),
                        pltpu.VMEM((D_EXPERT, D_MODEL), BF16),
                        pltpu.SemaphoreType.DMA((2,))],
    )
    return pl.pallas_call(
        _expert_kernel,
        grid_spec=grid_spec,
        out_shape=jax.ShapeDtypeStruct((nblk * rows, D_MODEL), F32),
        compiler_params=_cparams(("arbitrary",)),
        name="experts",
    )(blk_exp, nvalid, hs, w_gate, w_up, w_down)


def _combine_kernel(dcur_ref, dnxt_ref, x1_ref, meta_ref, ys_hbm, op_ref, os_ref, gbuf, sem, *, nb_prompt):
    i = pl.program_id(0)
    last = pl.num_programs(0) - 1
    slot = i & 1
    tm = x1_ref.shape[0]

    def start_gather(d_ref, s):
        def rows8(g, _):
            for k in range(ROW_GROUP):
                r = g * ROW_GROUP + k
                for e in range(2):
                    pltpu.make_async_copy(ys_hbm.at[pl.ds(d_ref[0, 0, 2 * r + e], 1), :],
                                          gbuf.at[s, e, pl.ds(r, 1), :], sem.at[s]).start(priority=e)
            return 0

        lax.fori_loop(0, tm // ROW_GROUP, rows8, 0)

    @pl.when(i == 0)
    def _():
        start_gather(dcur_ref, 0)

    @pl.when(i < last)
    def _():
        start_gather(dnxt_ref, 1 - slot)

    for e in range(2):
        pltpu.make_async_copy(ys_hbm.at[pl.ds(0, tm), :], gbuf.at[slot, e], sem.at[slot]).wait()
    meta = meta_ref[...]
    y = x1_ref[...] + meta[:, 4:5] * gbuf[slot, 0] + meta[:, 5:6] * gbuf[slot, 1]

    @pl.when(i < nb_prompt)
    def _():
        op_ref[...] = y

    @pl.when(i >= nb_prompt)
    def _():
        os_ref[...] = y


def _combine(x1, ys, meta, dest, t_prompt):
    t_all = x1.shape[0]
    tm = MERGE_ROWS
    nb = t_all // tm
    nbp = t_prompt // tm
    dest = dest.reshape(nb, 1, 2 * tm)
    grid_spec = pl.GridSpec(
        grid=(nb,),
        in_specs=[pl.BlockSpec((1, 1, 2 * tm), lambda i: (i, 0, 0), memory_space=pltpu.SMEM),
                  pl.BlockSpec((1, 1, 2 * tm), lambda i: (jnp.minimum(i + 1, nb - 1), 0, 0),
                               memory_space=pltpu.SMEM),
                  pl.BlockSpec((tm, D_MODEL), lambda i: (i, 0)),
                  pl.BlockSpec((tm, LANES), lambda i: (i, 0)),
                  pl.BlockSpec(memory_space=pl.ANY)],
        out_specs=[pl.BlockSpec((tm, D_MODEL), lambda i: (jnp.minimum(i, nbp - 1), 0)),
                   pl.BlockSpec((tm, D_MODEL), lambda i: (jnp.maximum(i - nbp, 0), 0))],
        scratch_shapes=[pltpu.VMEM((2, 2, tm, D_MODEL), F32), pltpu.SemaphoreType.DMA((2,))],
    )
    return pl.pallas_call(
        functools.partial(_combine_kernel, nb_prompt=nbp),
        grid_spec=grid_spec,
        out_shape=[jax.ShapeDtypeStruct((t_prompt, D_MODEL), F32),
                   jax.ShapeDtypeStruct((t_all - t_prompt, D_MODEL), F32)],
        compiler_params=_cparams(("arbitrary",)),
        name="combine",
    )(dest, dest, x1, meta, ys)


def _prep_weights(norm_mix, w_in, g_q, g_k, w_pool, pool_scale, w_out, norm_ffn, w_group, w_expert):
    lane = jnp.arange(ATTN_WIDTH)
    hsum = (lane[:, None] // HEAD_DIM == lane[None, :] // HEAD_DIM).astype(BF16)
    w_router = jnp.zeros((D_MODEL, LANES), F32)
    w_router = w_router.at[:, 0:N_GROUPS].set(w_group).at[:, N_GROUPS:N_GROUPS + N_EXPERTS].set(w_expert)
    w_router_hi = w_router.astype(BF16)
    w_router_lo = (w_router - w_router_hi.astype(F32)).astype(BF16)
    r = jnp.arange(MERGE_ROWS)
    return {
        "nrm": norm_mix.reshape(1, D_MODEL),
        "w_in": w_in.astype(BF16),
        "gq": (jnp.tile(g_q, N_HEADS) * (HEAD_DIM ** -0.5 * LOG2E)).reshape(1, ATTN_WIDTH),
        "gk": jnp.tile(g_k, N_HEADS).reshape(1, ATTN_WIDTH),
        "hsum": hsum,
        "w_pool": w_pool.astype(BF16),
        "pool_scale": pool_scale.reshape(1, POOL_WIDTH),
        "w_out_pool": w_out[:POOL_WIDTH].astype(BF16),
        "w_out_attn": w_out[POOL_WIDTH:].astype(BF16),
        "nrm_ffn": norm_ffn.reshape(1, D_MODEL),
        "w_router_hi": w_router_hi,
        "w_router_lo": w_router_lo,
        "tri_merge": (r[None, :] < r[:, None]).astype(BF16),
    }


def _later_sums(n_keys, n_heads):
    idx = jnp.arange(n_heads * n_keys)
    same = idx[:, None] // n_keys == idx[None, :] // n_keys
    later = same & (idx[:, None] > idx[None, :])
    return jnp.concatenate([later, same], axis=1).astype(BF16)


def kernel(x_prompt, x_sample, cache_k, cache_v, state_pool, page_table, meta_tokens, norm_mix, w_in, g_q, g_k,
           sb_bias, w_pool, pool_scale, w_out, norm_ffn, w_group, w_expert, w_gate, w_up, w_down):
    nb, seq, _ = x_prompt.shape
    nseq, nt, _ = x_sample.shape
    n_phys = cache_k.shape[1]
    t_prompt = nb * seq
    t_sample = nseq * nt
    t_all = t_prompt + t_sample

    cw = _prep_weights(norm_mix[0], w_in[0], g_q[0], g_k[0], w_pool[0], pool_scale[0], w_out[0], norm_ffn[0],
                       w_group[0], w_expert[0])
    bias = sb_bias[0].astype(F32) * LOG2E

    zero_hist = jnp.zeros((HIST_ROWS, POOL_WIDTH), F32)
    _, k_m, v_m, kx_m, vx_m, _, tail_m = _proj_seq(meta_tokens[None], zero_hist, cw, tm=N_META, p0=0,
                                                   full_count=False)
    pad_rows = ((0, ATT_BLOCK - N_META), (0, 0))
    kxm = jnp.pad(kx_m[0], pad_rows)
    vxm = jnp.pad(vx_m[0], pad_rows)

    q_p, k_p, v_p, kx_p, vx_p, po_p, tail_p = _proj_seq(x_prompt, tail_m[0], cw, tm=PROJ_ROWS, p0=N_META,
                                                        full_count=True)
    bias_rows = jnp.repeat(bias, ATT_BLOCK)[None]
    attn_p = _attn_prompt(q_p, kx_p, vx_p, kxm, vxm, bias_rows, _later_sums(ATT_BLOCK, 2))

    x_s_tm = jnp.transpose(x_sample, (1, 0, 2)).reshape(t_sample, D_MODEL)
    hist_tm = jnp.transpose(state_pool[0], (1, 0, 2))
    q_s, k_s, v_s, po_s, tail_s = _proj_step(x_s_tm, hist_tm, cw, nseq=nseq, nt=nt)
    to_seq = lambda a: jnp.transpose(a.reshape(nt, nseq, a.shape[-1]), (1, 0, 2))
    q_sb, k_sb, v_sb, po_sb = to_seq(q_s), to_seq(k_s), to_seq(v_s), to_seq(po_s)
    head_of_lane = jnp.arange(ATTN_WIDTH) // HEAD_DIM
    head_mask = head_of_lane[None, :] == jnp.arange(N_HEADS)[:, None]
    qbd = jnp.where(head_mask[None, None], q_sb[:, :, None, :], 0).reshape(nseq, nt * N_HEADS, ATTN_WIDTH)
    pad_new = ((0, 0), (0, 8 - nt), (0, 0))
    kn = jnp.pad(k_sb, pad_new)
    vn = jnp.pad(v_sb, pad_new)
    bias_col = jnp.tile(bias, nt).reshape(nt * N_HEADS, 1)
    page_t = lambda c: jnp.transpose(c[0], (0, 2, 3, 1)).reshape(n_phys, ATTN_WIDTH, PAGE)
    attn_s = _attn_sample(page_table, qbd, kn, vn, bias_col, _later_sums(PAGE, 1), page_t(cache_k), page_t(cache_v))

    x1, h_all, meta, counts_f = _merge_router(
        x_prompt.reshape(t_prompt, D_MODEL), x_sample.reshape(t_sample, D_MODEL),
        po_p.reshape(t_prompt, POOL_WIDTH), po_sb.reshape(t_sample, POOL_WIDTH),
        attn_p.reshape(t_prompt, ATTN_WIDTH), attn_s.reshape(t_sample, ATTN_WIDTH).astype(BF16), cw)

    pair_exp = meta[:, 0:2].astype(jnp.int32)
    pair_rank = meta[:, 2:4].astype(jnp.int32)
    counts = counts_f[0, :N_EXPERTS].astype(jnp.int32)
    rows = EXPERT_ROWS
    padded = (counts + rows - 1) // rows * rows
    pad_end = jnp.cumsum(padded)
    pad_start = pad_end - padded
    nblk = -(-(2 * t_all) // rows) + N_EXPERTS
    of_expert = pair_exp[..., None] == jnp.arange(N_EXPERTS, dtype=jnp.int32)
    dest = jnp.sum(jnp.where(of_expert, pad_start, 0), axis=-1) + pair_rank
    blk_row0 = jnp.arange(nblk, dtype=jnp.int32) * rows
    blk_exp = jnp.minimum(jnp.sum(blk_row0[:, None] >= pad_end[None, :], axis=1), N_EXPERTS - 1).astype(jnp.int32)
    blk_of = blk_exp[:, None] == jnp.arange(N_EXPERTS, dtype=jnp.int32)
    blk_count = jnp.sum(jnp.where(blk_of, counts, 0), axis=1)
    blk_start = jnp.sum(jnp.where(blk_of, pad_start, 0), axis=1)
    n_rows = jnp.clip(blk_count - (blk_row0 - blk_start), 0, rows)
    n_rows = jnp.where(blk_row0 < pad_end[-1], n_rows, 0).astype(jnp.int32)
    n_tiled = (n_rows + ROW_GROUP - 1) // ROW_GROUP * ROW_GROUP
    fills = (blk_row0 + n_rows, n_tiled - n_rows, (blk_row0 + n_tiled) // ROW_GROUP, (rows - n_tiled) // ROW_GROUP)

    hs = _dispatch(fills, dest, h_all, nblk * rows)
    ys = _experts(blk_exp, n_tiled, hs, w_gate[0], w_up[0], w_down[0])
    y_prompt, y_sample = _combine(x1, ys, meta, dest, t_prompt)
    y_prompt = y_prompt.reshape(nb, seq, D_MODEL)
    y_sample = y_sample.reshape(nseq, nt, D_MODEL)

    heads = lambda a: a.reshape(a.shape[:-1] + (N_HEADS, HEAD_DIM))
    k_prompt = jnp.concatenate([jnp.broadcast_to(k_m, (nb, N_META, ATTN_WIDTH)), k_p], axis=1)
    v_prompt = jnp.concatenate([jnp.broadcast_to(v_m, (nb, N_META, ATTN_WIDTH)), v_p], axis=1)
    pool_sample = jnp.transpose(tail_s, (1, 0, 2))
    return (y_prompt, y_sample, heads(k_prompt)[None], heads(v_prompt)[None], tail_p[:, 1:][None],
            heads(k_sb)[None], heads(v_sb)[None], pool_sample[None])
```

```python
import functools

import jax
import jax.numpy as jnp
from jax import lax
from jax.experimental import pallas as pl
from jax.experimental.pallas import tpu as pltpu

F32 = jnp.float32
BF16 = jnp.bfloat16

D_MODEL = 1024
POOL_WIDTH = 512
ATTN_WIDTH = 512
HEAD_DIM = 64
N_HEADS = 8
N_PAIRS = N_HEADS // 2
POOL_WINDOWS = (2, 4, 8, 16)
POOL_GROUP = 128
POOL_HIST = 15
HIST_ROWS = 16
N_META = 16
PAGE = 128
N_GROUPS = 4
EXPERTS_PER_GROUP = 8
N_EXPERTS = 32
D_EXPERT = 512
EPS = 1e-6
NEG = -1e30
LOG2E = 1.4426950408889634
LOGIT_CAP = 100.0
LANES = 128
ATT_BLOCK = 128
ATT_UNROLL = 4
PROJ_ROWS = 512
MERGE_ROWS = 256
EXPERT_ROWS = 512
ROW_GROUP = 8
VMEM_LIMIT = 56 * 1024 * 1024


def _cparams(sem, vmem=VMEM_LIMIT):
    return pltpu.CompilerParams(dimension_semantics=sem, vmem_limit_bytes=vmem)


def _rms_rows(x, g):
    ms = jnp.mean(x * x, axis=-1, keepdims=True)
    return (x * lax.rsqrt(ms + EPS)) * g


def _head_norm(a, g, hsum):
    ssum = jnp.dot((a * a).astype(BF16), hsum, preferred_element_type=F32)
    return (a * lax.rsqrt(ssum * (1.0 / HEAD_DIM) + EPS)) * g


def _project(x, nrm, w_in, gq, gk, hsum):
    n = _rms_rows(x, nrm)
    proj = jnp.dot(n.astype(BF16), w_in, preferred_element_type=F32)
    u = proj[:, 0:POOL_WIDTH]
    q = _head_norm(proj[:, POOL_WIDTH:POOL_WIDTH + ATTN_WIDTH], gq, hsum)
    k = _head_norm(proj[:, POOL_WIDTH + ATTN_WIDTH:POOL_WIDTH + 2 * ATTN_WIDTH], gk, hsum)
    v = proj[:, POOL_WIDTH + 2 * ATTN_WIDTH:]
    return u, q, k, v


def _split_heads(a, out_ref):
    rows = a.shape[0]
    lane = lax.broadcasted_iota(jnp.int32, (rows, LANES), 1)
    first = lane < HEAD_DIM
    for p in range(N_PAIRS):
        a2 = a[:, p * LANES:(p + 1) * LANES]
        out_ref[:, 2 * p * LANES:(2 * p + 1) * LANES] = jnp.where(first, a2, 0.0).astype(out_ref.dtype)
        out_ref[:, (2 * p + 1) * LANES:(2 * p + 2) * LANES] = jnp.where(first, 0.0, a2).astype(out_ref.dtype)


def _log2_gates(z, visible=None):
    z = lax.clamp(-LOGIT_CAP, z, LOGIT_CAP)
    if visible is not None:
        z = jnp.where(visible, z, NEG)
    drop = jnp.log(1.0 + jnp.exp2(z)) * LOG2E
    return z - drop, drop


def _stick_step(tt, gate_in, mix_in):
    def stack_heads(b):
        half = b.shape[1] // 2
        return jnp.concatenate([b[:, :half], b[:, half:]], axis=0)

    nt_dims = (((1,), (1,)), ((), ()))
    gates = mixed = None
    if mix_in is not None:
        log_betas, drops, carries, vxbs = mix_in
        sums = [jnp.dot(drop, tt, preferred_element_type=F32) for drop in drops]
    if gate_in is not None:
        q2s, kxbs, brows, mask = gate_in
        zs = [lax.dot_general(q2, stack_heads(kxb), nt_dims, preferred_element_type=F32) + brow
              for q2, kxb, brow in zip(q2s, kxbs, brows)]
    if mix_in is not None:
        nk = sums[0].shape[1] // 2
        probs = [jnp.exp2(lb - s[:, :nk] - carry).astype(BF16) for lb, s, carry in zip(log_betas, sums, carries)]
    if gate_in is not None:
        gates = [(lb, lk.astype(BF16)) for lb, lk in (_log2_gates(z, mask) for z in zs)]
    if mix_in is not None:
        outs = [jnp.dot(a, stack_heads(vxb), preferred_element_type=F32) for a, vxb in zip(probs, vxbs)]
        mixed = [(out, carry + s[:, nk:]) for out, carry, s in zip(outs, carries, sums)]
    return gates, mixed


def _proj_seq_kernel(x_ref, hist_ref, nrm_ref, win_ref, gq_ref, gk_ref, hsum_ref, wpool_ref, ps_ref, *refs,
                     tm, p0, full_count, lead):
    if lead:
        klead_ref, vlead_ref, q_ref, k_ref, v_ref, kx_ref, vx_ref, po_ref, tail_ref, ext_ref, kv_buf, kv_sem = refs
    else:
        q_ref, k_ref, v_ref, kx_ref, vx_ref, po_ref, tail_ref, ext_ref = refs
    blk = pl.program_id(1)
    u, q, k, v = _project(x_ref[...], nrm_ref[...], win_ref[...], gq_ref[...], gk_ref[...], hsum_ref[...])
    q_ref[...] = q.astype(q_ref.dtype)
    if lead:
        seq = pl.program_id(0)
        step = seq * pl.num_programs(1) + blk
        slot = step & 1
        n_steps = pl.num_programs(0) * pl.num_programs(1)

        def block_copy(which, hbm_ref, s, b, j):
            return pltpu.make_async_copy(kv_buf.at[which, s], hbm_ref.at[b, pl.ds(lead + j * tm, tm), :],
                                         kv_sem.at[which, s])

        @pl.when(step >= 2)
        def _():
            block_copy(0, k_ref, slot, 0, 0).wait()
            block_copy(1, v_ref, slot, 0, 0).wait()

        kv_buf[0, slot] = k
        kv_buf[1, slot] = v
        block_copy(0, k_ref, slot, seq, blk).start()
        block_copy(1, v_ref, slot, seq, blk).start()

        @pl.when(blk == 0)
        def _():
            for which, (src, dst) in enumerate(((klead_ref, k_ref), (vlead_ref, v_ref))):
                cp = pltpu.make_async_copy(src, dst.at[seq, pl.ds(0, lead), :], kv_sem.at[which, 2])
                cp.start()
                cp.wait()

        @pl.when(step == n_steps - 1)
        def _():
            @pl.when(step >= 1)
            def _():
                block_copy(0, k_ref, 1 - slot, 0, 0).wait()
                block_copy(1, v_ref, 1 - slot, 0, 0).wait()

            block_copy(0, k_ref, slot, 0, 0).wait()
            block_copy(1, v_ref, slot, 0, 0).wait()
    else:
        k_ref[...] = k
        v_ref[...] = v
    _split_heads(k, kx_ref)
    _split_heads(v, vx_ref)

    @pl.when(blk == 0)
    def _():
        ext_ref[0:HIST_ROWS, :] = hist_ref[...]

    ext_ref[HIST_ROWS:HIST_ROWS + tm, :] = u
    for g, w in enumerate(POOL_WINDOWS):
        e = ext_ref[:, g * POOL_GROUP:(g + 1) * POOL_GROUP]
        s = e
        sh = 1
        while sh < w:
            s = s + pltpu.roll(s, sh, axis=0)
            sh *= 2
        win = s[HIST_ROWS:, :]
        tok = e[HIST_ROWS:, :]
        if full_count:
            d = win * (1.0 / w) - tok
        else:
            pos = p0 + lax.broadcasted_iota(jnp.int32, (tm, 1), 0)
            cnt = jnp.minimum(w, pos + 1).astype(F32)
            d = win / cnt - tok
        y = jnp.dot(d.astype(BF16), wpool_ref[g], preferred_element_type=F32)
        y = y * ps_ref[:, g * POOL_GROUP:(g + 1) * POOL_GROUP]
        po_ref[:, g * POOL_GROUP:(g + 1) * POOL_GROUP] = y.astype(po_ref.dtype)
    last = ext_ref[tm:tm + HIST_ROWS, :]
    ext_ref[0:HIST_ROWS, :] = last

    @pl.when(blk == pl.num_programs(1) - 1)
    def _():
        tail_ref[...] = last


def _proj_seq(x, hist, cw, *, tm, p0, full_count, kv_lead=None):
    b, l, _ = x.shape
    nb = l // tm
    lead = 0 if kv_lead is None else p0
    const = lambda shape: pl.BlockSpec(shape, lambda i, j: (0,) * len(shape))
    row = lambda width: pl.BlockSpec((None, tm, width), lambda i, j: (i, j, 0))
    kv_out = pl.BlockSpec(memory_space=pl.ANY) if lead else row(ATTN_WIDTH)
    kern = functools.partial(_proj_seq_kernel, tm=tm, p0=p0, full_count=full_count, lead=lead)
    scratch = [pltpu.VMEM((HIST_ROWS + tm, POOL_WIDTH), F32)]
    if lead:
        scratch += [pltpu.VMEM((2, 2, tm, ATTN_WIDTH), F32), pltpu.SemaphoreType.DMA((2, 3))]
    return pl.pallas_call(
        kern,
        grid=(b, nb),
        in_specs=[row(D_MODEL), const((HIST_ROWS, POOL_WIDTH)), const((1, D_MODEL)),
                  const((D_MODEL, 2 * D_MODEL)), const((1, ATTN_WIDTH)), const((1, ATTN_WIDTH)),
                  const((ATTN_WIDTH, ATTN_WIDTH)), const((4, POOL_GROUP, POOL_GROUP)), const((1, POOL_WIDTH))]
                 + ([const((lead, ATTN_WIDTH))] * 2 if lead else []),
        out_specs=[row(ATTN_WIDTH), kv_out, kv_out, row(2 * ATTN_WIDTH), row(2 * ATTN_WIDTH),
                   row(POOL_WIDTH), pl.BlockSpec((None, HIST_ROWS, POOL_WIDTH), lambda i, j: (i, 0, 0))],
        out_shape=[jax.ShapeDtypeStruct((b, l, ATTN_WIDTH), BF16),
                   jax.ShapeDtypeStruct((b, lead + l, ATTN_WIDTH), F32),
                   jax.ShapeDtypeStruct((b, lead + l, ATTN_WIDTH), F32),
                   jax.ShapeDtypeStruct((b, l, 2 * ATTN_WIDTH), BF16),
                   jax.ShapeDtypeStruct((b, l, 2 * ATTN_WIDTH), BF16),
                   jax.ShapeDtypeStruct((b, l, POOL_WIDTH), BF16),
                   jax.ShapeDtypeStruct((b, HIST_ROWS, POOL_WIDTH), F32)],
        scratch_shapes=scratch,
        compiler_params=_cparams(("arbitrary", "arbitrary")),
        name="proj_seq",
    )(x, hist, cw["nrm"], cw["w_in"], cw["gq"], cw["gk"], cw["hsum"], cw["w_pool"], cw["pool_scale"],
      *(kv_lead or ()))


def _proj_step_kernel(x_ref, hist_ref, nrm_ref, win_ref, gq_ref, gk_ref, hsum_ref, wpool_ref, ps_ref,
                      q_ref, k_ref, v_ref, po_ref, tail_ref, *, nseq, nt):
    u, q, k, v = _project(x_ref[...], nrm_ref[...], win_ref[...], gq_ref[...], gk_ref[...], hsum_ref[...])
    q_ref[...] = q.astype(q_ref.dtype)
    k_ref[...] = k
    v_ref[...] = v
    ext = [hist_ref[r] for r in range(POOL_HIST)] + [u[t * nseq:(t + 1) * nseq, :] for t in range(nt)]
    for r in range(POOL_HIST):
        tail_ref[r] = ext[nt + r]
    for g, w in enumerate(POOL_WINDOWS):
        sl = slice(g * POOL_GROUP, (g + 1) * POOL_GROUP)
        ds = []
        for t in range(nt):
            win = ext[POOL_HIST + t][:, sl]
            for i in range(1, w):
                win = win + ext[POOL_HIST + t - i][:, sl]
            ds.append(win * (1.0 / w) - ext[POOL_HIST + t][:, sl])
        d = jnp.concatenate(ds, axis=0)
        y = jnp.dot(d.astype(BF16), wpool_ref[g], preferred_element_type=F32) * ps_ref[:, sl]
        po_ref[:, sl] = y.astype(po_ref.dtype)


def _proj_step(x_tm, hist_tm, cw, *, nseq, nt):
    rows = nseq * nt
    kern = functools.partial(_proj_step_kernel, nseq=nseq, nt=nt)
    return pl.pallas_call(
        kern,
        out_shape=[jax.ShapeDtypeStruct((rows, ATTN_WIDTH), BF16),
                   jax.ShapeDtypeStruct((rows, ATTN_WIDTH), F32),
                   jax.ShapeDtypeStruct((rows, ATTN_WIDTH), F32),
                   jax.ShapeDtypeStruct((rows, POOL_WIDTH), BF16),
                   jax.ShapeDtypeStruct((POOL_HIST, nseq, POOL_WIDTH), F32)],
        compiler_params=pltpu.CompilerParams(vmem_limit_bytes=VMEM_LIMIT),
        name="proj_step",
    )(x_tm, hist_tm, cw["nrm"], cw["w_in"], cw["gq"], cw["gk"], cw["hsum"], cw["w_pool"], cw["pool_scale"])


def _attn_prompt_kernel(q_ref, kx_ref, vx_ref, kxm_ref, vxm_ref, bias_ref, tt_ref, o_ref,
                        carry_ref, acc_ref, lb_ref, lk_ref):
    qi = pl.program_id(1)
    tt = tt_ref[...]
    nq = ATT_BLOCK
    pairs = range(N_PAIRS)
    lanes = [slice(p * LANES, (p + 1) * LANES) for p in pairs]
    wide = [slice(2 * p * LANES, 2 * (p + 1) * LANES) for p in pairs]

    def rows(j):
        return pl.ds(pl.multiple_of(j * ATT_BLOCK, ATT_BLOCK), ATT_BLOCK)

    def gate_in(k_of, mask):
        return [q_ref[:, l] for l in lanes], [k_of(w) for w in wide], [bias_ref[:, w] for w in wide], mask

    def pending():
        return [lb_ref[p] for p in pairs], [lk_ref[p] for p in pairs]

    def keep(gates):
        lbs, drops = gates
        for p in pairs:
            lb_ref[p] = lbs[p]
            lk_ref[p] = drops[p]

    def step(gin, gates_prev, v_of=None, last=False):
        min_ = accs = None
        if gates_prev is not None:
            lbs, drops = gates_prev
            min_ = (lbs, drops, [carry_ref[p] for p in pairs], [v_of(w) for w in wide])
            accs = [acc_ref[:, l] for l in lanes]
        gates, mixed = _stick_step(tt, gin, min_)
        if mixed is not None:
            for p, (out, carry) in enumerate(mixed):
                if last:
                    o_ref[:, lanes[p]] = (accs[p] + out).astype(o_ref.dtype)
                else:
                    carry_ref[p] = carry
                    acc_ref[:, lanes[p]] = accs[p] + out
        return None if gates is None else ([g[0] for g in gates], [g[1] for g in gates])

    def token_tile(j):
        return gate_in(lambda w: kx_ref[rows(j), w], None)

    carry_ref[...] = jnp.zeros_like(carry_ref)
    acc_ref[...] = jnp.zeros_like(acc_ref)
    r_i = lax.broadcasted_iota(jnp.int32, (nq, 2 * ATT_BLOCK), 0)
    c_i = lax.broadcasted_iota(jnp.int32, (nq, 2 * ATT_BLOCK), 1) & (ATT_BLOCK - 1)
    keep(step(gate_in(lambda w: kx_ref[rows(qi), w], c_i < r_i), None))

    def tiles_from(j, n):
        g = pending()
        for k in range(n):
            g = step(token_tile(j - k), g, lambda w, k=k: vx_ref[rows(j - k + 1), w])
        keep(g)

    def unrolled_trip(t, _):
        tiles_from(qi - 1 - ATT_UNROLL * t, ATT_UNROLL)
        return 0

    n_trips = qi // ATT_UNROLL
    lax.fori_loop(0, n_trips, unrolled_trip, 0)

    def single_trip(t, _):
        tiles_from(qi - 1 - ATT_UNROLL * n_trips - t, 1)
        return 0

    lax.fori_loop(0, qi - ATT_UNROLL * n_trips, single_trip, 0)

    g = step(gate_in(lambda w: kxm_ref[:, w], c_i < N_META), pending(), lambda w: vx_ref[0:ATT_BLOCK, w])
    step(None, g, lambda w: vxm_ref[:, w], last=True)


def _attn_prompt(q, kx, vx, kxm, vxm, bias_rows, tt):
    b, l, _ = q.shape
    nq = l // ATT_BLOCK
    once = pl.Buffered(1)
    return pl.pallas_call(
        _attn_prompt_kernel,
        grid=(b, nq),
        in_specs=[pl.BlockSpec((None, ATT_BLOCK, ATTN_WIDTH), lambda i, j: (i, j, 0)),
                  pl.BlockSpec((None, l, 2 * ATTN_WIDTH), lambda i, j: (i, 0, 0), pipeline_mode=once),
                  pl.BlockSpec((None, l, 2 * ATTN_WIDTH), lambda i, j: (i, 0, 0), pipeline_mode=once),
                  pl.BlockSpec((ATT_BLOCK, 2 * ATTN_WIDTH), lambda i, j: (0, 0)),
                  pl.BlockSpec((ATT_BLOCK, 2 * ATTN_WIDTH), lambda i, j: (0, 0)),
                  pl.BlockSpec((1, 2 * ATTN_WIDTH), lambda i, j: (0, 0)),
                  pl.BlockSpec((2 * ATT_BLOCK, 4 * ATT_BLOCK), lambda i, j: (0, 0))],
        out_specs=pl.BlockSpec((None, ATT_BLOCK, ATTN_WIDTH), lambda i, j: (i, j, 0)),
        out_shape=jax.ShapeDtypeStruct((b, l, ATTN_WIDTH), BF16),
        scratch_shapes=[pltpu.VMEM((N_PAIRS, ATT_BLOCK, 2 * ATT_BLOCK), F32),
                        pltpu.VMEM((ATT_BLOCK, ATTN_WIDTH), F32),
                        pltpu.VMEM((N_PAIRS, ATT_BLOCK, 2 * ATT_BLOCK), F32),
                        pltpu.VMEM((N_PAIRS, ATT_BLOCK, 2 * ATT_BLOCK), BF16)],
        compiler_params=_cparams(("arbitrary", "arbitrary")),
        name="attn_prompt",
    )(q, kx, vx, kxm, vxm, bias_rows, tt)


def _attn_sample_kernel(pt_ref, qbd_ref, kn_ref, vn_ref, bias_ref, tt_ref, *refs, n_pages, nt):
    del pt_ref
    k_refs = refs[:n_pages]
    v_refs = refs[n_pages:2 * n_pages]
    o_ref = refs[2 * n_pages]
    qbd = qbd_ref[...]
    tt = tt_ref[...]
    bias = bias_ref[...]
    nc = qbd.shape[0]
    nn = (((1,), (0,)), ((), ()))
    nt_dims = (((1,), (1,)), ((), ()))

    pad = jnp.zeros((PAGE - kn_ref.shape[0], ATTN_WIDTH), F32)
    kn = jnp.concatenate([kn_ref[...], pad], axis=0).astype(BF16)
    vn = jnp.concatenate([vn_ref[...], pad], axis=0).astype(BF16)
    t_i = lax.broadcasted_iota(jnp.int32, (nc, PAGE), 0) >> 3
    k_i = lax.broadcasted_iota(jnp.int32, (nc, PAGE), 1)
    z_new = lax.dot_general(qbd, kn, nt_dims, preferred_element_type=F32) + bias
    zs = [z_new]
    for j in range(n_pages - 1, -1, -1):
        zs.append(lax.dot_general(qbd, k_refs[j][...].astype(BF16), nn, preferred_element_type=F32) + bias)
    gates = [_log2_gates(z, k_i < t_i if i == 0 else None) for i, z in enumerate(zs)]
    sums = [jnp.dot(lk.astype(BF16), tt, preferred_element_type=F32) for _, lk in gates]
    carry = jnp.zeros((nc, PAGE), F32)
    acc = None
    for i, ((lb, _), s) in enumerate(zip(gates, sums)):
        a = jnp.exp2(lb - s[:, :PAGE] - carry).astype(BF16)
        carry = carry + s[:, PAGE:]
        if i == 0:
            acc = jnp.dot(a, vn, preferred_element_type=F32)
        else:
            vt = v_refs[n_pages - i][...].astype(BF16)
            acc = acc + lax.dot_general(a, vt, nt_dims, preferred_element_type=F32)
    h_i = lax.broadcasted_iota(jnp.int32, (nc, ATTN_WIDTH), 0) & (N_HEADS - 1)
    l_i = lax.broadcasted_iota(jnp.int32, (nc, ATTN_WIDTH), 1) >> 6
    acc = jnp.where(h_i == l_i, acc, 0.0)
    for t in range(nt):
        o_ref[t:t + 1, :] = jnp.sum(acc[t * N_HEADS:(t + 1) * N_HEADS, :], axis=0, keepdims=True)


def _attn_sample(page_table, qbd, kn, vn, bias_col, tt, cache_k, cache_v):
    nseq, n_pages = page_table.shape
    nt = qbd.shape[1] // N_HEADS
    pt = page_table.reshape(-1)

    def page_spec(j):
        return pl.BlockSpec((None, ATTN_WIDTH, PAGE), lambda i, pt_ref: (pt_ref[i * n_pages + j], 0, 0))

    grid_spec = pltpu.PrefetchScalarGridSpec(
        num_scalar_prefetch=1,
        grid=(nseq,),
        in_specs=[pl.BlockSpec((None, nt * N_HEADS, ATTN_WIDTH), lambda i, pt_ref: (i, 0, 0)),
                  pl.BlockSpec((None, 8, ATTN_WIDTH), lambda i, pt_ref: (i, 0, 0)),
                  pl.BlockSpec((None, 8, ATTN_WIDTH), lambda i, pt_ref: (i, 0, 0)),
                  pl.BlockSpec((nt * N_HEADS, 1), lambda i, pt_ref: (0, 0)),
                  pl.BlockSpec((PAGE, 2 * PAGE), lambda i, pt_ref: (0, 0))]
                 + [page_spec(j) for j in range(n_pages)] + [page_spec(j) for j in range(n_pages)],
        out_specs=pl.BlockSpec((None, nt, ATTN_WIDTH), lambda i, pt_ref: (i, 0, 0)),
    )
    kern = functools.partial(_attn_sample_kernel, n_pages=n_pages, nt=nt)
    return pl.pallas_call(
        kern,
        grid_spec=grid_spec,
        out_shape=jax.ShapeDtypeStruct((nseq, nt, ATTN_WIDTH), F32),
        compiler_params=_cparams(("arbitrary",)),
        name="attn_sample",
    )(pt, qbd, kn, vn, bias_col, tt, *([cache_k] * n_pages), *([cache_v] * n_pages))


def _merge_router_kernel(xp_ref, xs_ref, pop_ref, pos_ref, atp_ref, ats_ref, wop_ref, woa_ref, nrm_ref, wrh_ref,
                         wrl_ref, tri_ref, x1_ref, h_ref, meta_ref, cout_ref, carry_ref, *, nb_prompt):
    i = pl.program_id(0)

    @pl.when(i == 0)
    def _():
        carry_ref[...] = jnp.zeros_like(carry_ref)

    is_p = i < nb_prompt
    x = jnp.where(is_p, xp_ref[...], xs_ref[...])
    po = jnp.where(is_p, pop_ref[...], pos_ref[...])
    at = jnp.where(is_p, atp_ref[...], ats_ref[...])
    y = (jnp.dot(po, wop_ref[...], preferred_element_type=F32)
         + jnp.dot(at, woa_ref[...], preferred_element_type=F32))
    x1 = x + y
    x1_ref[...] = x1
    h = _rms_rows(x1, nrm_ref[...])
    h_ref[...] = h
    hi = h.astype(BF16)
    lo = (h - hi.astype(F32)).astype(BF16)
    both = jnp.dot(hi, jnp.concatenate([wrh_ref[...], wrl_ref[...]], axis=1), preferred_element_type=F32)
    logits = both[:, :LANES] + both[:, LANES:] + jnp.dot(lo, wrh_ref[...], preferred_element_type=F32)
    tm = logits.shape[0]
    lane = lax.broadcasted_iota(jnp.int32, (tm, LANES), 1).astype(F32)
    ninf = -jnp.inf
    is_g = lane < N_GROUPS
    gl = jnp.where(is_g, logits, ninf)
    gmax = jnp.max(gl, axis=-1, keepdims=True)
    g_sel = jnp.min(jnp.where(gl == gmax, lane, float(LANES)), axis=-1, keepdims=True)
    psum = jnp.sum(jnp.where(is_g, jnp.exp(logits - gmax), 0.0), axis=-1, keepdims=True)
    g_w = 1.0 / psum
    e_lo = N_GROUPS + g_sel * EXPERTS_PER_GROUP
    in_group = (lane >= e_lo) & (lane < e_lo + EXPERTS_PER_GROUP)
    el = jnp.where(in_group, logits, ninf)
    v1 = jnp.max(el, axis=-1, keepdims=True)
    i1 = jnp.min(jnp.where(el == v1, lane, float(LANES)), axis=-1, keepdims=True)
    el2 = jnp.where(lane == i1, ninf, el)
    v2 = jnp.max(el2, axis=-1, keepdims=True)
    i2 = jnp.min(jnp.where(el2 == v2, lane, float(LANES)), axis=-1, keepdims=True)
    e21 = jnp.exp(v2 - v1)
    den = 1.0 + e21
    gate1 = g_w * (1.0 / den)
    gate2 = g_w * (e21 / den)
    e1 = i1 - N_GROUPS
    e2 = i2 - N_GROUPS
    onehot = jnp.where((lane == e1) | (lane == e2), 1.0, 0.0)
    before = jnp.dot(tri_ref[...], onehot.astype(BF16), preferred_element_type=F32) + carry_ref[...]
    rank1 = jnp.sum(jnp.where(lane == e1, before, 0.0), axis=-1, keepdims=True)
    rank2 = jnp.sum(jnp.where(lane == e2, before, 0.0), axis=-1, keepdims=True)
    carry = carry_ref[...] + jnp.sum(onehot, axis=0, keepdims=True)
    carry_ref[...] = carry
    cout_ref[...] = carry
    meta = jnp.where(lane == 0, e1, 0.0)
    meta = jnp.where(lane == 1, e2, meta)
    meta = jnp.where(lane == 2, rank1, meta)
    meta = jnp.where(lane == 3, rank2, meta)
    meta = jnp.where(lane == 4, gate1, meta)
    meta = jnp.where(lane == 5, gate2, meta)
    meta_ref[...] = meta


def _merge_router(x_p, x_s, po_p, po_s, at_p, at_s, cw):
    tm = MERGE_ROWS
    nbp = x_p.shape[0] // tm
    nbs = x_s.shape[0] // tm
    const = lambda shape: pl.BlockSpec(shape, lambda i: (0,) * len(shape))
    prow = lambda width: pl.BlockSpec((tm, width), lambda i: (jnp.minimum(i, nbp - 1), 0))
    srow = lambda width: pl.BlockSpec((tm, width), lambda i: (jnp.maximum(i - nbp, 0), 0))
    row = lambda width: pl.BlockSpec((tm, width), lambda i: (i, 0))
    t_all = (nbp + nbs) * tm
    return pl.pallas_call(
        functools.partial(_merge_router_kernel, nb_prompt=nbp),
        grid=(nbp + nbs,),
        in_specs=[prow(D_MODEL), srow(D_MODEL), prow(POOL_WIDTH), srow(POOL_WIDTH), prow(ATTN_WIDTH),
                  srow(ATTN_WIDTH), const((POOL_WIDTH, D_MODEL)), const((ATTN_WIDTH, D_MODEL)),
                  const((1, D_MODEL)), const((D_MODEL, LANES)), const((D_MODEL, LANES)), const((tm, tm))],
        out_specs=[row(D_MODEL), row(D_MODEL), row(LANES), const((1, LANES))],
        out_shape=[jax.ShapeDtypeStruct((t_all, D_MODEL), F32),
                   jax.ShapeDtypeStruct((t_all, D_MODEL), F32),
                   jax.ShapeDtypeStruct((t_all, LANES), F32),
                   jax.ShapeDtypeStruct((1, LANES), F32)],
        scratch_shapes=[pltpu.VMEM((1, LANES), F32)],
        compiler_params=_cparams(("arbitrary",)),
        name="merge_router",
    )(x_p, x_s, po_p, po_s, at_p, at_s, cw["w_out_pool"], cw["w_out_attn"], cw["nrm_ffn"], cw["w_router_hi"],
      cw["w_router_lo"], cw["tri_merge"])


def _dispatch_kernel(fill_r0_ref, fill_nr_ref, fill_g0_ref, fill_ng_ref, dest_ref, h_hbm, hs_hbm,
                     zbuf, hbuf, sem, in_sem, out_sem):
    i = pl.program_id(0)
    n = pl.num_programs(0)
    tm = hbuf.shape[1]
    slot = lax.rem(i, 3)
    prev = lax.rem(i + 2, 3)

    def row_of(r):
        return hs_hbm.at[r >> 3, pl.ds(r & (ROW_GROUP - 1), 1), :]

    def fetch(blk, s):
        return pltpu.make_async_copy(h_hbm.at[pl.ds(blk * tm, tm), :], hbuf.at[s], in_sem.at[s])

    def drain(s):
        for _ in range(2):
            pltpu.make_async_copy(zbuf, hs_hbm.at[pl.ds(0, tm // ROW_GROUP)], out_sem.at[s]).wait()

    @pl.when(i == 0)
    def _():
        fetch(0, 0).start()

        @pl.when(n > 1)
        def _():
            fetch(1, 1).start()

        zbuf[...] = jnp.zeros_like(zbuf)

        def fill(wait):
            def body(blk, _):
                for k in range(ROW_GROUP - 1):
                    @pl.when(k < fill_nr_ref[blk])
                    def _():
                        cp = pltpu.make_async_copy(zbuf.at[0, pl.ds(0, 1), :], row_of(fill_r0_ref[blk] + k), sem)
                        cp.wait() if wait else cp.start()

                ng = fill_ng_ref[blk]

                @pl.when(ng > 0)
                def _():
                    cp = pltpu.make_async_copy(zbuf.at[pl.ds(0, ng)], hs_hbm.at[pl.ds(fill_g0_ref[blk], ng)], sem)
                    cp.wait() if wait else cp.start()
                return 0

            lax.fori_loop(0, fill_r0_ref.shape[0], body, 0)

        fill(False)
        fill(True)

    fetch(i, slot).wait()

    def rows8(g, _):
        for k in range(ROW_GROUP):
            r = g * ROW_GROUP + k
            for e in range(2):
                pltpu.make_async_copy(hbuf.at[slot, pl.ds(r, 1), :], row_of(dest_ref[0, 0, 2 * r + e]),
                                      out_sem.at[slot]).start(priority=e)
        return 0

    lax.fori_loop(0, tm // ROW_GROUP, rows8, 0)

    @pl.when(i >= 1)
    def _():
        drain(prev)

    @pl.when(i + 2 < n)
    def _():
        fetch(i + 2, prev).start()

    @pl.when(i == n - 1)
    def _():
        drain(slot)


def _dispatch(fills, dest, h_all, n_sorted):
    t_all = h_all.shape[0]
    tm = EXPERT_ROWS
    grid_spec = pltpu.PrefetchScalarGridSpec(
        num_scalar_prefetch=4,
        grid=(t_all // tm,),
        in_specs=[pl.BlockSpec((1, 1, 2 * tm), lambda i, *_: (i, 0, 0), memory_space=pltpu.SMEM),
                  pl.BlockSpec(memory_space=pl.ANY)],
        out_specs=pl.BlockSpec(memory_space=pl.ANY),
        scratch_shapes=[pltpu.VMEM((tm // ROW_GROUP, ROW_GROUP, D_MODEL), F32),
                        pltpu.VMEM((3, tm, D_MODEL), F32),
                        pltpu.SemaphoreType.DMA(()), pltpu.SemaphoreType.DMA((3,)), pltpu.SemaphoreType.DMA((3,))],
    )
    return pl.pallas_call(
        _dispatch_kernel,
        grid_spec=grid_spec,
        out_shape=jax.ShapeDtypeStruct((n_sorted // ROW_GROUP, ROW_GROUP, D_MODEL), F32),
        compiler_params=_cparams(("arbitrary",)),
        name="dispatch",
    )(*fills, dest.reshape(t_all // tm, 1, 2 * tm), h_all)


def _expert_kernel(bexp_ref, nval_ref, hs_hbm, wg_ref, wu_ref, wd_ref, ys_ref, xbuf, wgb, wub, wdb, sem):
    i = pl.program_id(0)
    last = pl.num_programs(0) - 1
    slot = i & 1
    groups = xbuf.shape[1]
    g_cur = lax.shift_right_logical(nval_ref[i], 3)
    g_next = lax.shift_right_logical(nval_ref[jnp.minimum(i + 1, last)], 3)

    def fetch(blk, s, ng):
        return pltpu.make_async_copy(hs_hbm.at[pl.ds(blk * groups, ng)], xbuf.at[s, pl.ds(0, ng)], sem.at[s])

    @pl.when(i == 0)
    def _():
        xbuf[...] = jnp.zeros_like(xbuf)

        @pl.when(g_cur > 0)
        def _():
            fetch(0, 0, g_cur).start()

    @pl.when((i < last) & (g_next > 0))
    def _():
        fetch(i + 1, 1 - slot, g_next).start()

    @pl.when(g_cur > 0)
    def _():
        changed = jnp.logical_or(i == 0, bexp_ref[i] != bexp_ref[jnp.maximum(i - 1, 0)])

        @pl.when(changed)
        def _():
            wgb[...] = wg_ref[...].astype(BF16)
            wub[...] = wu_ref[...].astype(BF16)
            wdb[...] = wd_ref[...].astype(BF16)

        fetch(i, slot, g_cur).wait()
        xb = xbuf[slot].reshape(groups * ROW_GROUP, D_MODEL).astype(BF16)
        g = jnp.dot(xb, wgb[...], preferred_element_type=F32)
        u = jnp.dot(xb, wub[...], preferred_element_type=F32)
        mid = (g * jax.nn.sigmoid(g)) * u
        ys_ref[...] = jnp.dot(mid.astype(BF16), wdb[...], preferred_element_type=F32)

    @pl.when(g_cur == 0)
    def _():
        ys_ref[...] = jnp.zeros_like(ys_ref)


def _experts(blk_exp, nvalid, hs, w_gate, w_up, w_down):
    nblk = blk_exp.shape[0]
    rows = EXPERT_ROWS
    grid_spec = pltpu.PrefetchScalarGridSpec(
        num_scalar_prefetch=2,
        grid=(nblk,),
        in_specs=[pl.BlockSpec(memory_space=pl.ANY),
                  pl.BlockSpec((None, D_MODEL, D_EXPERT), lambda i, be, nv: (be[i], 0, 0)),
                  pl.BlockSpec((None, D_MODEL, D_EXPERT), lambda i, be, nv: (be[i], 0, 0)),
                  pl.BlockSpec((None, D_EXPERT, D_MODEL), lambda i, be, nv: (be[i], 0, 0))],
        out_specs=pl.BlockSpec((rows, D_MODEL), lambda i, be, nv: (i, 0)),
        scratch_shapes=[pltpu.VMEM((2, rows // ROW_GROUP, ROW_GROUP, D_MODEL), F32),
                        pltpu.VMEM((D_MODEL, D_EXPERT), BF16), pltpu.VMEM((D_MODEL, D_EXPERT), BF16),
                        pltpu.VMEM((D_EXPERT, D_MODEL), BF16),
                        pltpu.SemaphoreType.DMA((2,))],
    )
    return pl.pallas_call(
        _expert_kernel,
        grid_spec=grid_spec,
        out_shape=jax.ShapeDtypeStruct((nblk * rows, D_MODEL), F32),
        compiler_params=_cparams(("arbitrary",)),
        name="experts",
    )(blk_exp, nvalid, hs, w_gate, w_up, w_down)


def _combine_kernel(dcur_ref, dnxt_ref, x1_ref, meta_ref, ys_hbm, op_ref, os_ref, gbuf, sem, *, nb_prompt):
    i = pl.program_id(0)
    last = pl.num_programs(0) - 1
    slot = i & 1
    tm = x1_ref.shape[0]

    def start_gather(d_ref, s):
        def rows8(g, _):
            for k in range(ROW_GROUP):
                r = g * ROW_GROUP + k
                for e in range(2):
                    pltpu.make_async_copy(ys_hbm.at[pl.ds(d_ref[0, 0, 2 * r + e], 1), :],
                                          gbuf.at[s, e, pl.ds(r, 1), :], sem.at[s]).start(priority=e)
            return 0

        lax.fori_loop(0, tm // ROW_GROUP, rows8, 0)

    @pl.when(i == 0)
    def _():
        start_gather(dcur_ref, 0)

    @pl.when(i < last)
    def _():
        start_gather(dnxt_ref, 1 - slot)

    for e in range(2):
        pltpu.make_async_copy(ys_hbm.at[pl.ds(0, tm), :], gbuf.at[slot, e], sem.at[slot]).wait()
    meta = meta_ref[...]
    y = x1_ref[...] + meta[:, 4:5] * gbuf[slot, 0] + meta[:, 5:6] * gbuf[slot, 1]

    @pl.when(i < nb_prompt)
    def _():
        op_ref[...] = y

    @pl.when(i >= nb_prompt)
    def _():
        os_ref[...] = y


def _combine(x1, ys, meta, dest, t_prompt):
    t_all = x1.shape[0]
    tm = MERGE_ROWS
    nb = t_all // tm
    nbp = t_prompt // tm
    dest = dest.reshape(nb, 1, 2 * tm)
    grid_spec = pl.GridSpec(
        grid=(nb,),
        in_specs=[pl.BlockSpec((1, 1, 2 * tm), lambda i: (i, 0, 0), memory_space=pltpu.SMEM),
                  pl.BlockSpec((1, 1, 2 * tm), lambda i: (jnp.minimum(i + 1, nb - 1), 0, 0),
                               memory_space=pltpu.SMEM),
                  pl.BlockSpec((tm, D_MODEL), lambda i: (i, 0)),
                  pl.BlockSpec((tm, LANES), lambda i: (i, 0)),
                  pl.BlockSpec(memory_space=pl.ANY)],
        out_specs=[pl.BlockSpec((tm, D_MODEL), lambda i: (jnp.minimum(i, nbp - 1), 0)),
                   pl.BlockSpec((tm, D_MODEL), lambda i: (jnp.maximum(i - nbp, 0), 0))],
        scratch_shapes=[pltpu.VMEM((2, 2, tm, D_MODEL), F32), pltpu.SemaphoreType.DMA((2,))],
    )
    return pl.pallas_call(
        functools.partial(_combine_kernel, nb_prompt=nbp),
        grid_spec=grid_spec,
        out_shape=[jax.ShapeDtypeStruct((t_prompt, D_MODEL), F32),
                   jax.ShapeDtypeStruct((t_all - t_prompt, D_MODEL), F32)],
        compiler_params=_cparams(("arbitrary",)),
        name="combine",
    )(dest, dest, x1, meta, ys)


def _prep_weights(norm_mix, w_in, g_q, g_k, w_pool, pool_scale, w_out, norm_ffn, w_group, w_expert):
    lane = jnp.arange(ATTN_WIDTH)
    hsum = (lane[:, None] // HEAD_DIM == lane[None, :] // HEAD_DIM).astype(BF16)
    w_router = jnp.zeros((D_MODEL, LANES), F32)
    w_router = w_router.at[:, 0:N_GROUPS].set(w_group).at[:, N_GROUPS:N_GROUPS + N_EXPERTS].set(w_expert)
    w_router_hi = w_router.astype(BF16)
    w_router_lo = (w_router - w_router_hi.astype(F32)).astype(BF16)
    r = jnp.arange(MERGE_ROWS)
    return {
        "nrm": norm_mix.reshape(1, D_MODEL),
        "w_in": w_in.astype(BF16),
        "gq": (jnp.tile(g_q, N_HEADS) * (HEAD_DIM ** -0.5 * LOG2E)).reshape(1, ATTN_WIDTH),
        "gk": jnp.tile(g_k, N_HEADS).reshape(1, ATTN_WIDTH),
        "hsum": hsum,
        "w_pool": w_pool.astype(BF16),
        "pool_scale": pool_scale.reshape(1, POOL_WIDTH),
        "w_out_pool": w_out[:POOL_WIDTH].astype(BF16),
        "w_out_attn": w_out[POOL_WIDTH:].astype(BF16),
        "nrm_ffn": norm_ffn.reshape(1, D_MODEL),
        "w_router_hi": w_router_hi,
        "w_router_lo": w_router_lo,
        "tri_merge": (r[None, :] < r[:, None]).astype(BF16),
    }


def _later_sums(n_keys, n_heads):
    idx = jnp.arange(n_heads * n_keys)
    same = idx[:, None] // n_keys == idx[None, :] // n_keys
    later = same & (idx[:, None] > idx[None, :])
    return jnp.concatenate([later, same], axis=1).astype(BF16)


def kernel(x_prompt, x_sample, cache_k, cache_v, state_pool, page_table, meta_tokens, norm_mix, w_in, g_q, g_k,
           sb_bias, w_pool, pool_scale, w_out, norm_ffn, w_group, w_expert, w_gate, w_up, w_down):
    nb, seq, _ = x_prompt.shape
    nseq, nt, _ = x_sample.shape
    n_phys = cache_k.shape[1]
    t_prompt = nb * seq
    t_sample = nseq * nt
    t_all = t_prompt + t_sample

    cw = _prep_weights(norm_mix[0], w_in[0], g_q[0], g_k[0], w_pool[0], pool_scale[0], w_out[0], norm_ffn[0],
                       w_group[0], w_expert[0])
    bias = sb_bias[0].astype(F32) * LOG2E

    zero_hist = jnp.zeros((HIST_ROWS, POOL_WIDTH), F32)
    _, k_m, v_m, kx_m, vx_m, _, tail_m = _proj_seq(meta_tokens[None], zero_hist, cw, tm=N_META, p0=0,
                                                   full_count=False)
    pad_rows = ((0, ATT_BLOCK - N_META), (0, 0))
    kxm = jnp.pad(kx_m[0], pad_rows)
    vxm = jnp.pad(vx_m[0], pad_rows)

    q_p, k_p, v_p, kx_p, vx_p, po_p, tail_p = _proj_seq(x_prompt, tail_m[0], cw, tm=PROJ_ROWS, p0=N_META,
                                                        full_count=True, kv_lead=(k_m[0], v_m[0]))
    bias_rows = jnp.repeat(bias, ATT_BLOCK)[None]
    attn_p = _attn_prompt(q_p, kx_p, vx_p, kxm, vxm, bias_rows, _later_sums(ATT_BLOCK, 2))

    x_s_tm = jnp.transpose(x_sample, (1, 0, 2)).reshape(t_sample, D_MODEL)
    hist_tm = jnp.transpose(state_pool[0], (1, 0, 2))
    q_s, k_s, v_s, po_s, tail_s = _proj_step(x_s_tm, hist_tm, cw, nseq=nseq, nt=nt)
    to_seq = lambda a: jnp.transpose(a.reshape(nt, nseq, a.shape[-1]), (1, 0, 2))
    q_sb, k_sb, v_sb, po_sb = to_seq(q_s), to_seq(k_s), to_seq(v_s), to_seq(po_s)
    head_of_lane = jnp.arange(ATTN_WIDTH) // HEAD_DIM
    head_mask = head_of_lane[None, :] == jnp.arange(N_HEADS)[:, None]
    qbd = jnp.where(head_mask[None, None], q_sb[:, :, None, :], 0).reshape(nseq, nt * N_HEADS, ATTN_WIDTH)
    pad_new = ((0, 0), (0, 8 - nt), (0, 0))
    kn = jnp.pad(k_sb, pad_new)
    vn = jnp.pad(v_sb, pad_new)
    bias_col = jnp.tile(bias, nt).reshape(nt * N_HEADS, 1)
    page_t = lambda c: jnp.transpose(c[0], (0, 2, 3, 1)).reshape(n_phys, ATTN_WIDTH, PAGE)
    attn_s = _attn_sample(page_table, qbd, kn, vn, bias_col, _later_sums(PAGE, 1), page_t(cache_k), page_t(cache_v))

    x1, h_all, meta, counts_f = _merge_router(
        x_prompt.reshape(t_prompt, D_MODEL), x_sample.reshape(t_sample, D_MODEL),
        po_p.reshape(t_prompt, POOL_WIDTH), po_sb.reshape(t_sample, POOL_WIDTH),
        attn_p.reshape(t_prompt, ATTN_WIDTH), attn_s.reshape(t_sample, ATTN_WIDTH).astype(BF16), cw)

    pair_exp = meta[:, 0:2].astype(jnp.int32)
    pair_rank = meta[:, 2:4].astype(jnp.int32)
    counts = counts_f[0, :N_EXPERTS].astype(jnp.int32)
    rows = EXPERT_ROWS
    padded = (counts + rows - 1) // rows * rows
    pad_end = jnp.cumsum(padded)
    pad_start = pad_end - padded
    nblk = -(-(2 * t_all) // rows) + N_EXPERTS
    of_expert = pair_exp[..., None] == jnp.arange(N_EXPERTS, dtype=jnp.int32)
    dest = jnp.sum(jnp.where(of_expert, pad_start, 0), axis=-1) + pair_rank
    blk_row0 = jnp.arange(nblk, dtype=jnp.int32) * rows
    blk_exp = jnp.minimum(jnp.sum(blk_row0[:, None] >= pad_end[None, :], axis=1), N_EXPERTS - 1).astype(jnp.int32)
    blk_of = blk_exp[:, None] == jnp.arange(N_EXPERTS, dtype=jnp.int32)
    blk_count = jnp.sum(jnp.where(blk_of, counts, 0), axis=1)
    blk_start = jnp.sum(jnp.where(blk_of, pad_start, 0), axis=1)
    n_rows = jnp.clip(blk_count - (blk_row0 - blk_start), 0, rows)
    n_rows = jnp.where(blk_row0 < pad_end[-1], n_rows, 0).astype(jnp.int32)
    n_tiled = (n_rows + ROW_GROUP - 1) // ROW_GROUP * ROW_GROUP
    fills = (blk_row0 + n_rows, n_tiled - n_rows, (blk_row0 + n_tiled) // ROW_GROUP, (rows - n_tiled) // ROW_GROUP)

    hs = _dispatch(fills, dest, h_all, nblk * rows)
    ys = _experts(blk_exp, n_tiled, hs, w_gate[0], w_up[0], w_down[0])
    y_prompt, y_sample = _combine(x1, ys, meta, dest, t_prompt)
    y_prompt = y_prompt.reshape(nb, seq, D_MODEL)
    y_sample = y_sample.reshape(nseq, nt, D_MODEL)

    heads = lambda a: a.reshape(a.shape[:-1] + (N_HEADS, HEAD_DIM))
    pool_sample = jnp.transpose(tail_s, (1, 0, 2))
    return (y_prompt, y_sample, heads(k_p)[None], heads(v_p)[None], tail_p[:, 1:][None],
            heads(k_sb)[None], heads(v_sb)[None], pool_sample[None])
```

```python
import functools

import jax
import jax.numpy as jnp
from jax import lax
from jax.experimental import pallas as pl
from jax.experimental.pallas import tpu as pltpu

F32 = jnp.float32
BF16 = jnp.bfloat16

D_MODEL = 1024
POOL_WIDTH = 512
ATTN_WIDTH = 512
HEAD_DIM = 64
N_HEADS = 8
N_PAIRS = N_HEADS // 2
POOL_WINDOWS = (2, 4, 8, 16)
POOL_GROUP = 128
POOL_HIST = 15
HIST_ROWS = 16
N_META = 16
PAGE = 128
N_GROUPS = 4
EXPERTS_PER_GROUP = 8
N_EXPERTS = 32
D_EXPERT = 512
EPS = 1e-6
NEG = -1e30
LOG2E = 1.4426950408889634
LOGIT_CAP = 100.0
LANES = 128
ATT_BLOCK = 128
ATT_UNROLL = 4
PROJ_ROWS = 512
MERGE_ROWS = 256
COMBINE_ROWS = 512
EXPERT_ROWS = 512
ROW_GROUP = 8
VMEM_LIMIT = 56 * 1024 * 1024


def _cparams(sem, vmem=VMEM_LIMIT):
    return pltpu.CompilerParams(dimension_semantics=sem, vmem_limit_bytes=vmem)


def _rms_rows(x, g):
    ms = jnp.mean(x * x, axis=-1, keepdims=True)
    return (x * lax.rsqrt(ms + EPS)) * g


def _head_norm(a, g, hsum):
    ssum = jnp.dot((a * a).astype(BF16), hsum, preferred_element_type=F32)
    return (a * lax.rsqrt(ssum * (1.0 / HEAD_DIM) + EPS)) * g


def _project(x, nrm, w_in, gq, gk, hsum):
    n = _rms_rows(x, nrm)
    proj = jnp.dot(n.astype(BF16), w_in, preferred_element_type=F32)
    u = proj[:, 0:POOL_WIDTH]
    q = _head_norm(proj[:, POOL_WIDTH:POOL_WIDTH + ATTN_WIDTH], gq, hsum)
    k = _head_norm(proj[:, POOL_WIDTH + ATTN_WIDTH:POOL_WIDTH + 2 * ATTN_WIDTH], gk, hsum)
    v = proj[:, POOL_WIDTH + 2 * ATTN_WIDTH:]
    return u, q, k, v


def _split_heads(a, out_ref):
    rows = a.shape[0]
    lane = lax.broadcasted_iota(jnp.int32, (rows, LANES), 1)
    first = lane < HEAD_DIM
    for p in range(N_PAIRS):
        a2 = a[:, p * LANES:(p + 1) * LANES]
        out_ref[:, 2 * p * LANES:(2 * p + 1) * LANES] = jnp.where(first, a2, 0.0).astype(out_ref.dtype)
        out_ref[:, (2 * p + 1) * LANES:(2 * p + 2) * LANES] = jnp.where(first, 0.0, a2).astype(out_ref.dtype)


def _log2_gates(z, visible=None):
    z = lax.clamp(-LOGIT_CAP, z, LOGIT_CAP)
    if visible is not None:
        z = jnp.where(visible, z, NEG)
    drop = jnp.log(1.0 + jnp.exp2(z)) * LOG2E
    return z - drop, drop


def _stick_step(tt, gate_in, mix_in):
    def stack_heads(b):
        half = b.shape[1] // 2
        return jnp.concatenate([b[:, :half], b[:, half:]], axis=0)

    nt_dims = (((1,), (1,)), ((), ()))
    gates = mixed = None
    if mix_in is not None:
        log_betas, drops, carries, vxbs = mix_in
        sums = [jnp.dot(drop, tt, preferred_element_type=F32) for drop in drops]
    if gate_in is not None:
        q2s, kxbs, brows, mask = gate_in
        zs = [lax.dot_general(q2, stack_heads(kxb), nt_dims, preferred_element_type=F32) + brow
              for q2, kxb, brow in zip(q2s, kxbs, brows)]
    if mix_in is not None:
        nk = sums[0].shape[1] // 2
        probs = [jnp.exp2(lb - s[:, :nk] - carry).astype(BF16) for lb, s, carry in zip(log_betas, sums, carries)]
    if gate_in is not None:
        gates = [(lb, lk.astype(BF16)) for lb, lk in (_log2_gates(z, mask) for z in zs)]
    if mix_in is not None:
        outs = [jnp.dot(a, stack_heads(vxb), preferred_element_type=F32) for a, vxb in zip(probs, vxbs)]
        mixed = [(out, carry + s[:, nk:]) for out, carry, s in zip(outs, carries, sums)]
    return gates, mixed


def _proj_seq_kernel(x_ref, hist_ref, nrm_ref, win_ref, gq_ref, gk_ref, hsum_ref, wpool_ref, ps_ref, *refs,
                     tm, p0, full_count, lead):
    if lead:
        klead_ref, vlead_ref, q_ref, k_ref, v_ref, kx_ref, vx_ref, po_ref, tail_ref, ext_ref, kv_buf, kv_sem = refs
    else:
        q_ref, k_ref, v_ref, kx_ref, vx_ref, po_ref, tail_ref, ext_ref = refs
    blk = pl.program_id(1)
    if lead:
        seq = pl.program_id(0)
        step = seq * pl.num_programs(1) + blk
        slot = step & 1
        n_steps = pl.num_programs(0) * pl.num_programs(1)

        def block_copy(which, hbm_ref, s, b, j):
            return pltpu.make_async_copy(kv_buf.at[which, s], hbm_ref.at[b, pl.ds(lead + j * tm, tm), :],
                                         kv_sem.at[which, s])

        @pl.when(step >= 2)
        def _():
            block_copy(0, k_ref, slot, 0, 0).wait()
            block_copy(1, v_ref, slot, 0, 0).wait()

    u, q, k, v = _project(x_ref[...], nrm_ref[...], win_ref[...], gq_ref[...], gk_ref[...], hsum_ref[...])
    q_ref[...] = q.astype(q_ref.dtype)
    if lead:
        kv_buf[0, slot] = k
        kv_buf[1, slot] = v
    else:
        k_ref[...] = k
        v_ref[...] = v
    _split_heads(k, kx_ref)
    _split_heads(v, vx_ref)

    @pl.when(blk == 0)
    def _():
        ext_ref[0:HIST_ROWS, :] = hist_ref[...]

    ext_ref[HIST_ROWS:HIST_ROWS + tm, :] = u
    for g, w in enumerate(POOL_WINDOWS):
        e = ext_ref[:, g * POOL_GROUP:(g + 1) * POOL_GROUP]
        s = e
        sh = 1
        while sh < w:
            s = s + pltpu.roll(s, sh, axis=0)
            sh *= 2
        win = s[HIST_ROWS:, :]
        tok = e[HIST_ROWS:, :]
        if full_count:
            d = win * (1.0 / w) - tok
        else:
            pos = p0 + lax.broadcasted_iota(jnp.int32, (tm, 1), 0)
            cnt = jnp.minimum(w, pos + 1).astype(F32)
            d = win / cnt - tok
        y = jnp.dot(d.astype(BF16), wpool_ref[g], preferred_element_type=F32)
        y = y * ps_ref[:, g * POOL_GROUP:(g + 1) * POOL_GROUP]
        po_ref[:, g * POOL_GROUP:(g + 1) * POOL_GROUP] = y.astype(po_ref.dtype)
    last = ext_ref[tm:tm + HIST_ROWS, :]
    ext_ref[0:HIST_ROWS, :] = last

    @pl.when(blk == pl.num_programs(1) - 1)
    def _():
        tail_ref[...] = last

    if lead:
        block_copy(0, k_ref, slot, seq, blk).start()
        block_copy(1, v_ref, slot, seq, blk).start()

        @pl.when(blk == 0)
        def _():
            for which, (src, dst) in enumerate(((klead_ref, k_ref), (vlead_ref, v_ref))):
                cp = pltpu.make_async_copy(src, dst.at[seq, pl.ds(0, lead), :], kv_sem.at[which, 2])
                cp.start()
                cp.wait()

        @pl.when(step == n_steps - 1)
        def _():
            @pl.when(step >= 1)
            def _():
                block_copy(0, k_ref, 1 - slot, 0, 0).wait()
                block_copy(1, v_ref, 1 - slot, 0, 0).wait()

            block_copy(0, k_ref, slot, 0, 0).wait()
            block_copy(1, v_ref, slot, 0, 0).wait()


def _proj_seq(x, hist, cw, *, tm, p0, full_count, kv_lead=None):
    b, l, _ = x.shape
    nb = l // tm
    lead = 0 if kv_lead is None else p0
    const = lambda shape: pl.BlockSpec(shape, lambda i, j: (0,) * len(shape))
    row = lambda width: pl.BlockSpec((None, tm, width), lambda i, j: (i, j, 0))
    kv_out = pl.BlockSpec(memory_space=pl.ANY) if lead else row(ATTN_WIDTH)
    kern = functools.partial(_proj_seq_kernel, tm=tm, p0=p0, full_count=full_count, lead=lead)
    scratch = [pltpu.VMEM((HIST_ROWS + tm, POOL_WIDTH), F32)]
    if lead:
        scratch += [pltpu.VMEM((2, 2, tm, ATTN_WIDTH), F32), pltpu.SemaphoreType.DMA((2, 3))]
    return pl.pallas_call(
        kern,
        grid=(b, nb),
        in_specs=[row(D_MODEL), const((HIST_ROWS, POOL_WIDTH)), const((1, D_MODEL)),
                  const((D_MODEL, 2 * D_MODEL)), const((1, ATTN_WIDTH)), const((1, ATTN_WIDTH)),
                  const((ATTN_WIDTH, ATTN_WIDTH)), const((4, POOL_GROUP, POOL_GROUP)), const((1, POOL_WIDTH))]
                 + ([const((lead, ATTN_WIDTH))] * 2 if lead else []),
        out_specs=[row(ATTN_WIDTH), kv_out, kv_out, row(2 * ATTN_WIDTH), row(2 * ATTN_WIDTH),
                   row(POOL_WIDTH), pl.BlockSpec((None, HIST_ROWS, POOL_WIDTH), lambda i, j: (i, 0, 0))],
        out_shape=[jax.ShapeDtypeStruct((b, l, ATTN_WIDTH), BF16),
                   jax.ShapeDtypeStruct((b, lead + l, ATTN_WIDTH), F32),
                   jax.ShapeDtypeStruct((b, lead + l, ATTN_WIDTH), F32),
                   jax.ShapeDtypeStruct((b, l, 2 * ATTN_WIDTH), BF16),
                   jax.ShapeDtypeStruct((b, l, 2 * ATTN_WIDTH), BF16),
                   jax.ShapeDtypeStruct((b, l, POOL_WIDTH), BF16),
                   jax.ShapeDtypeStruct((b, HIST_ROWS, POOL_WIDTH), F32)],
        scratch_shapes=scratch,
        compiler_params=_cparams(("arbitrary", "arbitrary")),
        name="proj_seq",
    )(x, hist, cw["nrm"], cw["w_in"], cw["gq"], cw["gk"], cw["hsum"], cw["w_pool"], cw["pool_scale"],
      *(kv_lead or ()))


def _proj_step_kernel(x_ref, hist_ref, nrm_ref, win_ref, gq_ref, gk_ref, hsum_ref, wpool_ref, ps_ref,
                      q_ref, k_ref, v_ref, po_ref, tail_ref, *, nseq, nt):
    u, q, k, v = _project(x_ref[...], nrm_ref[...], win_ref[...], gq_ref[...], gk_ref[...], hsum_ref[...])
    q_ref[...] = q.astype(q_ref.dtype)
    k_ref[...] = k
    v_ref[...] = v
    ext = [hist_ref[r] for r in range(POOL_HIST)] + [u[t * nseq:(t + 1) * nseq, :] for t in range(nt)]
    for r in range(POOL_HIST):
        tail_ref[r] = ext[nt + r]
    for g, w in enumerate(POOL_WINDOWS):
        sl = slice(g * POOL_GROUP, (g + 1) * POOL_GROUP)
        ds = []
        for t in range(nt):
            win = ext[POOL_HIST + t][:, sl]
            for i in range(1, w):
                win = win + ext[POOL_HIST + t - i][:, sl]
            ds.append(win * (1.0 / w) - ext[POOL_HIST + t][:, sl])
        d = jnp.concatenate(ds, axis=0)
        y = jnp.dot(d.astype(BF16), wpool_ref[g], preferred_element_type=F32) * ps_ref[:, sl]
        po_ref[:, sl] = y.astype(po_ref.dtype)


def _proj_step(x_tm, hist_tm, cw, *, nseq, nt):
    rows = nseq * nt
    kern = functools.partial(_proj_step_kernel, nseq=nseq, nt=nt)
    return pl.pallas_call(
        kern,
        out_shape=[jax.ShapeDtypeStruct((rows, ATTN_WIDTH), BF16),
                   jax.ShapeDtypeStruct((rows, ATTN_WIDTH), F32),
                   jax.ShapeDtypeStruct((rows, ATTN_WIDTH), F32),
                   jax.ShapeDtypeStruct((rows, POOL_WIDTH), BF16),
                   jax.ShapeDtypeStruct((POOL_HIST, nseq, POOL_WIDTH), F32)],
        compiler_params=pltpu.CompilerParams(vmem_limit_bytes=VMEM_LIMIT),
        name="proj_step",
    )(x_tm, hist_tm, cw["nrm"], cw["w_in"], cw["gq"], cw["gk"], cw["hsum"], cw["w_pool"], cw["pool_scale"])


def _attn_prompt_kernel(q_ref, kx_ref, vx_ref, kxm_ref, vxm_ref, bias_ref, tt_ref, o_ref,
                        carry_ref, acc_ref, lb_ref, lk_ref):
    qi = pl.program_id(1)
    tt = tt_ref[...]
    nq = ATT_BLOCK
    pairs = range(N_PAIRS)
    lanes = [slice(p * LANES, (p + 1) * LANES) for p in pairs]
    wide = [slice(2 * p * LANES, 2 * (p + 1) * LANES) for p in pairs]

    def rows(j):
        return pl.ds(pl.multiple_of(j * ATT_BLOCK, ATT_BLOCK), ATT_BLOCK)

    def gate_in(k_of, mask):
        return [q_ref[:, l] for l in lanes], [k_of(w) for w in wide], [bias_ref[:, w] for w in wide], mask

    def pending():
        return [lb_ref[p] for p in pairs], [lk_ref[p] for p in pairs]

    def keep(gates):
        lbs, drops = gates
        for p in pairs:
            lb_ref[p] = lbs[p]
            lk_ref[p] = drops[p]

    def step(gin, gates_prev, v_of=None, last=False):
        min_ = accs = None
        if gates_prev is not None:
            lbs, drops = gates_prev
            min_ = (lbs, drops, [carry_ref[p] for p in pairs], [v_of(w) for w in wide])
            accs = [acc_ref[:, l] for l in lanes]
        gates, mixed = _stick_step(tt, gin, min_)
        if mixed is not None:
            for p, (out, carry) in enumerate(mixed):
                if last:
                    o_ref[:, lanes[p]] = (accs[p] + out).astype(o_ref.dtype)
                else:
                    carry_ref[p] = carry
                    acc_ref[:, lanes[p]] = accs[p] + out
        return None if gates is None else ([g[0] for g in gates], [g[1] for g in gates])

    def token_tile(j):
        return gate_in(lambda w: kx_ref[rows(j), w], None)

    carry_ref[...] = jnp.zeros_like(carry_ref)
    acc_ref[...] = jnp.zeros_like(acc_ref)
    r_i = lax.broadcasted_iota(jnp.int32, (nq, 2 * ATT_BLOCK), 0)
    c_i = lax.broadcasted_iota(jnp.int32, (nq, 2 * ATT_BLOCK), 1) & (ATT_BLOCK - 1)
    keep(step(gate_in(lambda w: kx_ref[rows(qi), w], c_i < r_i), None))

    def tiles_from(j, n):
        g = pending()
        for k in range(n):
            g = step(token_tile(j - k), g, lambda w, k=k: vx_ref[rows(j - k + 1), w])
        keep(g)

    def unrolled_trip(t, _):
        tiles_from(qi - 1 - ATT_UNROLL * t, ATT_UNROLL)
        return 0

    n_trips = qi // ATT_UNROLL
    lax.fori_loop(0, n_trips, unrolled_trip, 0)

    def single_trip(t, _):
        tiles_from(qi - 1 - ATT_UNROLL * n_trips - t, 1)
        return 0

    lax.fori_loop(0, qi - ATT_UNROLL * n_trips, single_trip, 0)

    g = step(gate_in(lambda w: kxm_ref[:, w], c_i < N_META), pending(), lambda w: vx_ref[0:ATT_BLOCK, w])
    step(None, g, lambda w: vxm_ref[:, w], last=True)


def _attn_prompt(q, kx, vx, kxm, vxm, bias_rows, tt):
    b, l, _ = q.shape
    nq = l // ATT_BLOCK
    once = pl.Buffered(1)
    return pl.pallas_call(
        _attn_prompt_kernel,
        grid=(b, nq),
        in_specs=[pl.BlockSpec((None, ATT_BLOCK, ATTN_WIDTH), lambda i, j: (i, j, 0)),
                  pl.BlockSpec((None, l, 2 * ATTN_WIDTH), lambda i, j: (i, 0, 0), pipeline_mode=once),
                  pl.BlockSpec((None, l, 2 * ATTN_WIDTH), lambda i, j: (i, 0, 0), pipeline_mode=once),
                  pl.BlockSpec((ATT_BLOCK, 2 * ATTN_WIDTH), lambda i, j: (0, 0)),
                  pl.BlockSpec((ATT_BLOCK, 2 * ATTN_WIDTH), lambda i, j: (0, 0)),
                  pl.BlockSpec((1, 2 * ATTN_WIDTH), lambda i, j: (0, 0)),
                  pl.BlockSpec((2 * ATT_BLOCK, 4 * ATT_BLOCK), lambda i, j: (0, 0))],
        out_specs=pl.BlockSpec((None, ATT_BLOCK, ATTN_WIDTH), lambda i, j: (i, j, 0)),
        out_shape=jax.ShapeDtypeStruct((b, l, ATTN_WIDTH), BF16),
        scratch_shapes=[pltpu.VMEM((N_PAIRS, ATT_BLOCK, 2 * ATT_BLOCK), F32),
                        pltpu.VMEM((ATT_BLOCK, ATTN_WIDTH), F32),
                        pltpu.VMEM((N_PAIRS, ATT_BLOCK, 2 * ATT_BLOCK), F32),
                        pltpu.VMEM((N_PAIRS, ATT_BLOCK, 2 * ATT_BLOCK), BF16)],
        compiler_params=_cparams(("arbitrary", "arbitrary")),
        name="attn_prompt",
    )(q, kx, vx, kxm, vxm, bias_rows, tt)


def _attn_sample_kernel(pt_ref, qbd_ref, kn_ref, vn_ref, bias_ref, tt_ref, *refs, n_pages, nt):
    del pt_ref
    k_refs = refs[:n_pages]
    v_refs = refs[n_pages:2 * n_pages]
    o_ref = refs[2 * n_pages]
    qbd = qbd_ref[...]
    tt = tt_ref[...]
    bias = bias_ref[...]
    nc = qbd.shape[0]
    nn = (((1,), (0,)), ((), ()))
    nt_dims = (((1,), (1,)), ((), ()))

    pad = jnp.zeros((PAGE - kn_ref.shape[0], ATTN_WIDTH), F32)
    kn = jnp.concatenate([kn_ref[...], pad], axis=0).astype(BF16)
    vn = jnp.concatenate([vn_ref[...], pad], axis=0).astype(BF16)
    t_i = lax.broadcasted_iota(jnp.int32, (nc, PAGE), 0) >> 3
    k_i = lax.broadcasted_iota(jnp.int32, (nc, PAGE), 1)
    z_new = lax.dot_general(qbd, kn, nt_dims, preferred_element_type=F32) + bias
    zs = [z_new]
    for j in range(n_pages - 1, -1, -1):
        zs.append(lax.dot_general(qbd, k_refs[j][...].astype(BF16), nn, preferred_element_type=F32) + bias)
    gates = [_log2_gates(z, k_i < t_i if i == 0 else None) for i, z in enumerate(zs)]
    sums = [jnp.dot(lk.astype(BF16), tt, preferred_element_type=F32) for _, lk in gates]
    carry = jnp.zeros((nc, PAGE), F32)
    acc = None
    for i, ((lb, _), s) in enumerate(zip(gates, sums)):
        a = jnp.exp2(lb - s[:, :PAGE] - carry).astype(BF16)
        carry = carry + s[:, PAGE:]
        if i == 0:
            acc = jnp.dot(a, vn, preferred_element_type=F32)
        else:
            vt = v_refs[n_pages - i][...].astype(BF16)
            acc = acc + lax.dot_general(a, vt, nt_dims, preferred_element_type=F32)
    h_i = lax.broadcasted_iota(jnp.int32, (nc, ATTN_WIDTH), 0) & (N_HEADS - 1)
    l_i = lax.broadcasted_iota(jnp.int32, (nc, ATTN_WIDTH), 1) >> 6
    acc = jnp.where(h_i == l_i, acc, 0.0)
    for t in range(nt):
        o_ref[t:t + 1, :] = jnp.sum(acc[t * N_HEADS:(t + 1) * N_HEADS, :], axis=0, keepdims=True)


def _attn_sample(page_table, qbd, kn, vn, bias_col, tt, cache_k, cache_v):
    nseq, n_pages = page_table.shape
    nt = qbd.shape[1] // N_HEADS
    pt = page_table.reshape(-1)

    def page_spec(j):
        return pl.BlockSpec((None, ATTN_WIDTH, PAGE), lambda i, pt_ref: (pt_ref[i * n_pages + j], 0, 0))

    grid_spec = pltpu.PrefetchScalarGridSpec(
        num_scalar_prefetch=1,
        grid=(nseq,),
        in_specs=[pl.BlockSpec((None, nt * N_HEADS, ATTN_WIDTH), lambda i, pt_ref: (i, 0, 0)),
                  pl.BlockSpec((None, 8, ATTN_WIDTH), lambda i, pt_ref: (i, 0, 0)),
                  pl.BlockSpec((None, 8, ATTN_WIDTH), lambda i, pt_ref: (i, 0, 0)),
                  pl.BlockSpec((nt * N_HEADS, 1), lambda i, pt_ref: (0, 0)),
                  pl.BlockSpec((PAGE, 2 * PAGE), lambda i, pt_ref: (0, 0))]
                 + [page_spec(j) for j in range(n_pages)] + [page_spec(j) for j in range(n_pages)],
        out_specs=pl.BlockSpec((None, nt, ATTN_WIDTH), lambda i, pt_ref: (i, 0, 0)),
    )
    kern = functools.partial(_attn_sample_kernel, n_pages=n_pages, nt=nt)
    return pl.pallas_call(
        kern,
        grid_spec=grid_spec,
        out_shape=jax.ShapeDtypeStruct((nseq, nt, ATTN_WIDTH), F32),
        compiler_params=_cparams(("arbitrary",)),
        name="attn_sample",
    )(pt, qbd, kn, vn, bias_col, tt, *([cache_k] * n_pages), *([cache_v] * n_pages))


def _merge_router_kernel(xp_ref, xs_ref, pop_ref, pos_ref, atp_ref, ats_ref, wop_ref, woa_ref, nrm_ref, wrh_ref,
                         wrl_ref, tri_ref, x1_ref, h_ref, meta_ref, cout_ref, carry_ref, *, nb_prompt):
    i = pl.program_id(0)

    @pl.when(i == 0)
    def _():
        carry_ref[...] = jnp.zeros_like(carry_ref)

    is_p = i < nb_prompt
    x = jnp.where(is_p, xp_ref[...], xs_ref[...])
    po = jnp.where(is_p, pop_ref[...], pos_ref[...])
    at = jnp.where(is_p, atp_ref[...], ats_ref[...])
    y = (jnp.dot(po, wop_ref[...], preferred_element_type=F32)
         + jnp.dot(at, woa_ref[...], preferred_element_type=F32))
    x1 = x + y
    x1_ref[...] = x1
    h = _rms_rows(x1, nrm_ref[...])
    h_ref[...] = h
    hi = h.astype(BF16)
    lo = (h - hi.astype(F32)).astype(BF16)
    both = jnp.dot(hi, jnp.concatenate([wrh_ref[...], wrl_ref[...]], axis=1), preferred_element_type=F32)
    logits = both[:, :LANES] + both[:, LANES:] + jnp.dot(lo, wrh_ref[...], preferred_element_type=F32)
    tm = logits.shape[0]
    lane = lax.broadcasted_iota(jnp.int32, (tm, LANES), 1).astype(F32)
    ninf = -jnp.inf
    is_g = lane < N_GROUPS
    gl = jnp.where(is_g, logits, ninf)
    gmax = jnp.max(gl, axis=-1, keepdims=True)
    g_sel = jnp.min(jnp.where(gl == gmax, lane, float(LANES)), axis=-1, keepdims=True)
    psum = jnp.sum(jnp.where(is_g, jnp.exp(logits - gmax), 0.0), axis=-1, keepdims=True)
    g_w = 1.0 / psum
    e_lo = N_GROUPS + g_sel * EXPERTS_PER_GROUP
    in_group = (lane >= e_lo) & (lane < e_lo + EXPERTS_PER_GROUP)
    el = jnp.where(in_group, logits, ninf)
    v1 = jnp.max(el, axis=-1, keepdims=True)
    i1 = jnp.min(jnp.where(el == v1, lane, float(LANES)), axis=-1, keepdims=True)
    el2 = jnp.where(lane == i1, ninf, el)
    v2 = jnp.max(el2, axis=-1, keepdims=True)
    i2 = jnp.min(jnp.where(el2 == v2, lane, float(LANES)), axis=-1, keepdims=True)
    e21 = jnp.exp(v2 - v1)
    den = 1.0 + e21
    gate1 = g_w * (1.0 / den)
    gate2 = g_w * (e21 / den)
    e1 = i1 - N_GROUPS
    e2 = i2 - N_GROUPS
    onehot = jnp.where((lane == e1) | (lane == e2), 1.0, 0.0)
    before = jnp.dot(tri_ref[...], onehot.astype(BF16), preferred_element_type=F32) + carry_ref[...]
    rank1 = jnp.sum(jnp.where(lane == e1, before, 0.0), axis=-1, keepdims=True)
    rank2 = jnp.sum(jnp.where(lane == e2, before, 0.0), axis=-1, keepdims=True)
    carry = carry_ref[...] + jnp.sum(onehot, axis=0, keepdims=True)
    carry_ref[...] = carry
    cout_ref[...] = carry
    meta = jnp.where(lane == 0, e1, 0.0)
    meta = jnp.where(lane == 1, e2, meta)
    meta = jnp.where(lane == 2, rank1, meta)
    meta = jnp.where(lane == 3, rank2, meta)
    meta = jnp.where(lane == 4, gate1, meta)
    meta = jnp.where(lane == 5, gate2, meta)
    meta_ref[...] = meta


def _merge_router(x_p, x_s, po_p, po_s, at_p, at_s, cw):
    tm = MERGE_ROWS
    nbp = x_p.shape[0] // tm
    nbs = x_s.shape[0] // tm
    const = lambda shape: pl.BlockSpec(shape, lambda i: (0,) * len(shape))
    prow = lambda width: pl.BlockSpec((tm, width), lambda i: (jnp.minimum(i, nbp - 1), 0))
    srow = lambda width: pl.BlockSpec((tm, width), lambda i: (jnp.maximum(i - nbp, 0), 0))
    row = lambda width: pl.BlockSpec((tm, width), lambda i: (i, 0))
    t_all = (nbp + nbs) * tm
    return pl.pallas_call(
        functools.partial(_merge_router_kernel, nb_prompt=nbp),
        grid=(nbp + nbs,),
        in_specs=[prow(D_MODEL), srow(D_MODEL), prow(POOL_WIDTH), srow(POOL_WIDTH), prow(ATTN_WIDTH),
                  srow(ATTN_WIDTH), const((POOL_WIDTH, D_MODEL)), const((ATTN_WIDTH, D_MODEL)),
                  const((1, D_MODEL)), const((D_MODEL, LANES)), const((D_MODEL, LANES)), const((tm, tm))],
        out_specs=[row(D_MODEL), row(D_MODEL), row(LANES), const((1, LANES))],
        out_shape=[jax.ShapeDtypeStruct((t_all, D_MODEL), F32),
                   jax.ShapeDtypeStruct((t_all, D_MODEL), F32),
                   jax.ShapeDtypeStruct((t_all, LANES), F32),
                   jax.ShapeDtypeStruct((1, LANES), F32)],
        scratch_shapes=[pltpu.VMEM((1, LANES), F32)],
        compiler_params=_cparams(("arbitrary",)),
        name="merge_router",
    )(x_p, x_s, po_p, po_s, at_p, at_s, cw["w_out_pool"], cw["w_out_attn"], cw["nrm_ffn"], cw["w_router_hi"],
      cw["w_router_lo"], cw["tri_merge"])


def _dispatch_kernel(fill_r0_ref, fill_nr_ref, fill_g0_ref, fill_ng_ref, dest_ref, h_hbm, hs_hbm,
                     zbuf, hbuf, sem, in_sem, out_sem):
    i = pl.program_id(0)
    n = pl.num_programs(0)
    tm = hbuf.shape[1]
    slot = lax.rem(i, 3)
    prev = lax.rem(i + 2, 3)

    def row_of(r):
        return hs_hbm.at[r >> 3, pl.ds(r & (ROW_GROUP - 1), 1), :]

    def fetch(blk, s):
        return pltpu.make_async_copy(h_hbm.at[pl.ds(blk * tm, tm), :], hbuf.at[s], in_sem.at[s])

    def drain(s):
        for _ in range(2):
            pltpu.make_async_copy(zbuf, hs_hbm.at[pl.ds(0, tm // ROW_GROUP)], out_sem.at[s]).wait()

    @pl.when(i == 0)
    def _():
        fetch(0, 0).start()

        @pl.when(n > 1)
        def _():
            fetch(1, 1).start()

        zbuf[...] = jnp.zeros_like(zbuf)

        def fill(wait):
            def body(blk, _):
                for k in range(ROW_GROUP - 1):
                    @pl.when(k < fill_nr_ref[blk])
                    def _():
                        cp = pltpu.make_async_copy(zbuf.at[0, pl.ds(0, 1), :], row_of(fill_r0_ref[blk] + k), sem)
                        cp.wait() if wait else cp.start()

                ng = fill_ng_ref[blk]

                @pl.when(ng > 0)
                def _():
                    cp = pltpu.make_async_copy(zbuf.at[pl.ds(0, ng)], hs_hbm.at[pl.ds(fill_g0_ref[blk], ng)], sem)
                    cp.wait() if wait else cp.start()
                return 0

            lax.fori_loop(0, fill_r0_ref.shape[0], body, 0)

        fill(False)
        fill(True)

    fetch(i, slot).wait()

    def rows8(g, _):
        for k in range(ROW_GROUP):
            r = g * ROW_GROUP + k
            for e in range(2):
                pltpu.make_async_copy(hbuf.at[slot, pl.ds(r, 1), :], row_of(dest_ref[0, 0, 2 * r + e]),
                                      out_sem.at[slot]).start(priority=e)
        return 0

    lax.fori_loop(0, tm // ROW_GROUP, rows8, 0)

    @pl.when(i >= 1)
    def _():
        drain(prev)

    @pl.when(i + 2 < n)
    def _():
        fetch(i + 2, prev).start()

    @pl.when(i == n - 1)
    def _():
        drain(slot)


def _dispatch(fills, dest, h_all, n_sorted):
    t_all = h_all.shape[0]
    tm = EXPERT_ROWS
    grid_spec = pltpu.PrefetchScalarGridSpec(
        num_scalar_prefetch=4,
        grid=(t_all // tm,),
        in_specs=[pl.BlockSpec((1, 1, 2 * tm), lambda i, *_: (i, 0, 0), memory_space=pltpu.SMEM),
                  pl.BlockSpec(memory_space=pl.ANY)],
        out_specs=pl.BlockSpec(memory_space=pl.ANY),
        scratch_shapes=[pltpu.VMEM((tm // ROW_GROUP, ROW_GROUP, D_MODEL), F32),
                        pltpu.VMEM((3, tm, D_MODEL), F32),
                        pltpu.SemaphoreType.DMA(()), pltpu.SemaphoreType.DMA((3,)), pltpu.SemaphoreType.DMA((3,))],
    )
    return pl.pallas_call(
        _dispatch_kernel,
        grid_spec=grid_spec,
        out_shape=jax.ShapeDtypeStruct((n_sorted // ROW_GROUP, ROW_GROUP, D_MODEL), F32),
        compiler_params=_cparams(("arbitrary",)),
        name="dispatch",
    )(*fills, dest.reshape(t_all // tm, 1, 2 * tm), h_all)


def _expert_kernel(bexp_ref, nval_ref, hs_hbm, wg_ref, wu_ref, wd_ref, ys_ref, xbuf, wgb, wub, wdb, sem):
    i = pl.program_id(0)
    last = pl.num_programs(0) - 1
    slot = i & 1
    groups = xbuf.shape[1]
    g_cur = lax.shift_right_logical(nval_ref[i], 3)
    g_next = lax.shift_right_logical(nval_ref[jnp.minimum(i + 1, last)], 3)

    def fetch(blk, s, ng):
        return pltpu.make_async_copy(hs_hbm.at[pl.ds(blk * groups, ng)], xbuf.at[s, pl.ds(0, ng)], sem.at[s])

    @pl.when(i == 0)
    def _():
        xbuf[...] = jnp.zeros_like(xbuf)

        @pl.when(g_cur > 0)
        def _():
            fetch(0, 0, g_cur).start()

    @pl.when((i < last) & (g_next > 0))
    def _():
        fetch(i + 1, 1 - slot, g_next).start()

    @pl.when(g_cur > 0)
    def _():
        changed = jnp.logical_or(i == 0, bexp_ref[i] != bexp_ref[jnp.maximum(i - 1, 0)])

        @pl.when(changed)
        def _():
            wgb[...] = wg_ref[...].astype(BF16)
            wub[...] = wu_ref[...].astype(BF16)
            wdb[...] = wd_ref[...].astype(BF16)

        fetch(i, slot, g_cur).wait()
        xb = xbuf[slot].reshape(groups * ROW_GROUP, D_MODEL).astype(BF16)
        g = jnp.dot(xb, wgb[...], preferred_element_type=F32)
        u = jnp.dot(xb, wub[...], preferred_element_type=F32)
        mid = (g * jax.nn.sigmoid(g)) * u
        ys_ref[...] = jnp.dot(mid.astype(BF16), wdb[...], preferred_element_type=F32)

    @pl.when(g_cur == 0)
    def _():
        ys_ref[...] = jnp.zeros_like(ys_ref)


def _experts(blk_exp, nvalid, hs, w_gate, w_up, w_down):
    nblk = blk_exp.shape[0]
    rows = EXPERT_ROWS
    grid_spec = pltpu.PrefetchScalarGridSpec(
        num_scalar_prefetch=2,
        grid=(nblk,),
        in_specs=[pl.BlockSpec(memory_space=pl.ANY),
                  pl.BlockSpec((None, D_MODEL, D_EXPERT), lambda i, be, nv: (be[i], 0, 0)),
                  pl.BlockSpec((None, D_MODEL, D_EXPERT), lambda i, be, nv: (be[i], 0, 0)),
                  pl.BlockSpec((None, D_EXPERT, D_MODEL), lambda i, be, nv: (be[i], 0, 0))],
        out_specs=pl.BlockSpec((rows, D_MODEL), lambda i, be, nv: (i, 0)),
        scratch_shapes=[pltpu.VMEM((2, rows // ROW_GROUP, ROW_GROUP, D_MODEL), F32),
                        pltpu.VMEM((D_MODEL, D_EXPERT), BF16), pltpu.VMEM((D_MODEL, D_EXPERT), BF16),
                        pltpu.VMEM((D_EXPERT, D_MODEL), BF16),
                        pltpu.SemaphoreType.DMA((2,))],
    )
    return pl.pallas_call(
        _expert_kernel,
        grid_spec=grid_spec,
        out_shape=jax.ShapeDtypeStruct((nblk * rows, D_MODEL), F32),
        compiler_params=_cparams(("arbitrary",)),
        name="experts",
    )(blk_exp, nvalid, hs, w_gate, w_up, w_down)


def _combine_kernel(dcur_ref, dnxt_ref, x1_ref, meta_ref, ys_hbm, op_ref, os_ref, gbuf, sem, *, nb_prompt):
    i = pl.program_id(0)
    last = pl.num_programs(0) - 1
    slot = i & 1
    tm = x1_ref.shape[0]

    def start_gather(d_ref, s):
        def rows8(g, _):
            for k in range(ROW_GROUP):
                r = g * ROW_GROUP + k
                for e in range(2):
                    pltpu.make_async_copy(ys_hbm.at[pl.ds(d_ref[0, 0, 2 * r + e], 1), :],
                                          gbuf.at[s, e, pl.ds(r, 1), :], sem.at[s]).start(priority=e)
            return 0

        lax.fori_loop(0, tm // ROW_GROUP, rows8, 0)

    @pl.when(i == 0)
    def _():
        start_gather(dcur_ref, 0)

    @pl.when(i < last)
    def _():
        start_gather(dnxt_ref, 1 - slot)

    for e in range(2):
        pltpu.make_async_copy(ys_hbm.at[pl.ds(0, tm), :], gbuf.at[slot, e], sem.at[slot]).wait()
    meta = meta_ref[...]
    y = x1_ref[...] + meta[:, 4:5] * gbuf[slot, 0] + meta[:, 5:6] * gbuf[slot, 1]

    @pl.when(i < nb_prompt)
    def _():
        op_ref[...] = y

    @pl.when(i >= nb_prompt)
    def _():
        os_ref[...] = y


def _combine(x1, ys, meta, dest, t_prompt):
    t_all = x1.shape[0]
    tm = COMBINE_ROWS
    nb = t_all // tm
    nbp = t_prompt // tm
    dest = dest.reshape(nb, 1, 2 * tm)
    grid_spec = pl.GridSpec(
        grid=(nb,),
        in_specs=[pl.BlockSpec((1, 1, 2 * tm), lambda i: (i, 0, 0), memory_space=pltpu.SMEM),
                  pl.BlockSpec((1, 1, 2 * tm), lambda i: (jnp.minimum(i + 1, nb - 1), 0, 0),
                               memory_space=pltpu.SMEM),
                  pl.BlockSpec((tm, D_MODEL), lambda i: (i, 0)),
                  pl.BlockSpec((tm, LANES), lambda i: (i, 0)),
                  pl.BlockSpec(memory_space=pl.ANY)],
        out_specs=[pl.BlockSpec((tm, D_MODEL), lambda i: (jnp.minimum(i, nbp - 1), 0)),
                   pl.BlockSpec((tm, D_MODEL), lambda i: (jnp.maximum(i - nbp, 0), 0))],
        scratch_shapes=[pltpu.VMEM((2, 2, tm, D_MODEL), F32), pltpu.SemaphoreType.DMA((2,))],
    )
    return pl.pallas_call(
        functools.partial(_combine_kernel, nb_prompt=nbp),
        grid_spec=grid_spec,
        out_shape=[jax.ShapeDtypeStruct((t_prompt, D_MODEL), F32),
                   jax.ShapeDtypeStruct((t_all - t_prompt, D_MODEL), F32)],
        compiler_params=_cparams(("arbitrary",)),
        name="combine",
    )(dest, dest, x1, meta, ys)


def _prep_weights(norm_mix, w_in, g_q, g_k, w_pool, pool_scale, w_out, norm_ffn, w_group, w_expert):
    lane = jnp.arange(ATTN_WIDTH)
    hsum = (lane[:, None] // HEAD_DIM == lane[None, :] // HEAD_DIM).astype(BF16)
    w_router = jnp.zeros((D_MODEL, LANES), F32)
    w_router = w_router.at[:, 0:N_GROUPS].set(w_group).at[:, N_GROUPS:N_GROUPS + N_EXPERTS].set(w_expert)
    w_router_hi = w_router.astype(BF16)
    w_router_lo = (w_router - w_router_hi.astype(F32)).astype(BF16)
    r = jnp.arange(MERGE_ROWS)
    return {
        "nrm": norm_mix.reshape(1, D_MODEL),
        "w_in": w_in.astype(BF16),
        "gq": (jnp.tile(g_q, N_HEADS) * (HEAD_DIM ** -0.5 * LOG2E)).reshape(1, ATTN_WIDTH),
        "gk": jnp.tile(g_k, N_HEADS).reshape(1, ATTN_WIDTH),
        "hsum": hsum,
        "w_pool": w_pool.astype(BF16),
        "pool_scale": pool_scale.reshape(1, POOL_WIDTH),
        "w_out_pool": w_out[:POOL_WIDTH].astype(BF16),
        "w_out_attn": w_out[POOL_WIDTH:].astype(BF16),
        "nrm_ffn": norm_ffn.reshape(1, D_MODEL),
        "w_router_hi": w_router_hi,
        "w_router_lo": w_router_lo,
        "tri_merge": (r[None, :] < r[:, None]).astype(BF16),
    }


def _later_sums(n_keys, n_heads):
    idx = jnp.arange(n_heads * n_keys)
    same = idx[:, None] // n_keys == idx[None, :] // n_keys
    later = same & (idx[:, None] > idx[None, :])
    return jnp.concatenate([later, same], axis=1).astype(BF16)


def kernel(x_prompt, x_sample, cache_k, cache_v, state_pool, page_table, meta_tokens, norm_mix, w_in, g_q, g_k,
           sb_bias, w_pool, pool_scale, w_out, norm_ffn, w_group, w_expert, w_gate, w_up, w_down):
    nb, seq, _ = x_prompt.shape
    nseq, nt, _ = x_sample.shape
    n_phys = cache_k.shape[1]
    t_prompt = nb * seq
    t_sample = nseq * nt
    t_all = t_prompt + t_sample

    cw = _prep_weights(norm_mix[0], w_in[0], g_q[0], g_k[0], w_pool[0], pool_scale[0], w_out[0], norm_ffn[0],
                       w_group[0], w_expert[0])
    bias = sb_bias[0].astype(F32) * LOG2E

    zero_hist = jnp.zeros((HIST_ROWS, POOL_WIDTH), F32)
    _, k_m, v_m, kx_m, vx_m, _, tail_m = _proj_seq(meta_tokens[None], zero_hist, cw, tm=N_META, p0=0,
                                                   full_count=False)
    pad_rows = ((0, ATT_BLOCK - N_META), (0, 0))
    kxm = jnp.pad(kx_m[0], pad_rows)
    vxm = jnp.pad(vx_m[0], pad_rows)

    q_p, k_p, v_p, kx_p, vx_p, po_p, tail_p = _proj_seq(x_prompt, tail_m[0], cw, tm=PROJ_ROWS, p0=N_META,
                                                        full_count=True, kv_lead=(k_m[0], v_m[0]))
    bias_rows = jnp.repeat(bias, ATT_BLOCK)[None]
    attn_p = _attn_prompt(q_p, kx_p, vx_p, kxm, vxm, bias_rows, _later_sums(ATT_BLOCK, 2))

    x_s_tm = jnp.transpose(x_sample, (1, 0, 2)).reshape(t_sample, D_MODEL)
    hist_tm = jnp.transpose(state_pool[0], (1, 0, 2))
    q_s, k_s, v_s, po_s, tail_s = _proj_step(x_s_tm, hist_tm, cw, nseq=nseq, nt=nt)
    to_seq = lambda a: jnp.transpose(a.reshape(nt, nseq, a.shape[-1]), (1, 0, 2))
    q_sb, k_sb, v_sb, po_sb = to_seq(q_s), to_seq(k_s), to_seq(v_s), to_seq(po_s)
    head_of_lane = jnp.arange(ATTN_WIDTH) // HEAD_DIM
    head_mask = head_of_lane[None, :] == jnp.arange(N_HEADS)[:, None]
    qbd = jnp.where(head_mask[None, None], q_sb[:, :, None, :], 0).reshape(nseq, nt * N_HEADS, ATTN_WIDTH)
    pad_new = ((0, 0), (0, 8 - nt), (0, 0))
    kn = jnp.pad(k_sb, pad_new)
    vn = jnp.pad(v_sb, pad_new)
    bias_col = jnp.tile(bias, nt).reshape(nt * N_HEADS, 1)
    page_t = lambda c: jnp.transpose(c[0], (0, 2, 3, 1)).reshape(n_phys, ATTN_WIDTH, PAGE)
    attn_s = _attn_sample(page_table, qbd, kn, vn, bias_col, _later_sums(PAGE, 1), page_t(cache_k), page_t(cache_v))

    x1, h_all, meta, counts_f = _merge_router(
        x_prompt.reshape(t_prompt, D_MODEL), x_sample.reshape(t_sample, D_MODEL),
        po_p.reshape(t_prompt, POOL_WIDTH), po_sb.reshape(t_sample, POOL_WIDTH),
        attn_p.reshape(t_prompt, ATTN_WIDTH), attn_s.reshape(t_sample, ATTN_WIDTH).astype(BF16), cw)

    pair_exp = meta[:, 0:2].astype(jnp.int32)
    pair_rank = meta[:, 2:4].astype(jnp.int32)
    counts = counts_f[0, :N_EXPERTS].astype(jnp.int32)
    rows = EXPERT_ROWS
    padded = (counts + rows - 1) // rows * rows
    pad_end = jnp.cumsum(padded)
    pad_start = pad_end - padded
    nblk = -(-(2 * t_all) // rows) + N_EXPERTS
    of_expert = pair_exp[..., None] == jnp.arange(N_EXPERTS, dtype=jnp.int32)
    dest = jnp.sum(jnp.where(of_expert, pad_start, 0), axis=-1) + pair_rank
    blk_row0 = jnp.arange(nblk, dtype=jnp.int32) * rows
    blk_exp = jnp.minimum(jnp.sum(blk_row0[:, None] >= pad_end[None, :], axis=1), N_EXPERTS - 1).astype(jnp.int32)
    blk_of = blk_exp[:, None] == jnp.arange(N_EXPERTS, dtype=jnp.int32)
    blk_count = jnp.sum(jnp.where(blk_of, counts, 0), axis=1)
    blk_start = jnp.sum(jnp.where(blk_of, pad_start, 0), axis=1)
    n_rows = jnp.clip(blk_count - (blk_row0 - blk_start), 0, rows)
    n_rows = jnp.where(blk_row0 < pad_end[-1], n_rows, 0).astype(jnp.int32)
    n_tiled = (n_rows + ROW_GROUP - 1) // ROW_GROUP * ROW_GROUP
    fills = (blk_row0 + n_rows, n_tiled - n_rows, (blk_row0 + n_tiled) // ROW_GROUP, (rows - n_tiled) // ROW_GROUP)

    hs = _dispatch(fills, dest, h_all, nblk * rows)
    ys = _experts(blk_exp, n_tiled, hs, w_gate[0], w_up[0], w_down[0])
    y_prompt, y_sample = _combine(x1, ys, meta, dest, t_prompt)
    y_prompt = y_prompt.reshape(nb, seq, D_MODEL)
    y_sample = y_sample.reshape(nseq, nt, D_MODEL)

    heads = lambda a: a.reshape(a.shape[:-1] + (N_HEADS, HEAD_DIM))
    pool_sample = jnp.transpose(tail_s, (1, 0, 2))
    return (y_prompt, y_sample, heads(k_p)[None], heads(v_p)[None], tail_p[:, 1:][None],
            heads(k_sb)[None], heads(v_sb)[None], pool_sample[None])
```

```python
import functools

import jax
import jax.numpy as jnp
from jax import lax
from jax.experimental import pallas as pl
from jax.experimental.pallas import tpu as pltpu

F32 = jnp.float32
BF16 = jnp.bfloat16

D_MODEL = 1024
POOL_WIDTH = 512
ATTN_WIDTH = 512
HEAD_DIM = 64
N_HEADS = 8
N_PAIRS = N_HEADS // 2
POOL_WINDOWS = (2, 4, 8, 16)
POOL_GROUP = 128
POOL_HIST = 15
HIST_ROWS = 16
N_META = 16
PAGE = 128
N_GROUPS = 4
EXPERTS_PER_GROUP = 8
N_EXPERTS = 32
D_EXPERT = 512
EPS = 1e-6
NEG = -1e30
LOG2E = 1.4426950408889634
LOGIT_CAP = 100.0
LANES = 128
ATT_BLOCK = 128
ATT_UNROLL = 4
PROJ_ROWS = 512
MERGE_ROWS = 256
COMBINE_ROWS = 512
EXPERT_ROWS = 512
ROW_GROUP = 8
VMEM_LIMIT = 56 * 1024 * 1024


def _cparams(sem, vmem=VMEM_LIMIT):
    return pltpu.CompilerParams(dimension_semantics=sem, vmem_limit_bytes=vmem)


def _rms_rows(x, g):
    ms = jnp.mean(x * x, axis=-1, keepdims=True)
    return (x * lax.rsqrt(ms + EPS)) * g


def _head_norm(a, g, hsum):
    ssum = jnp.dot((a * a).astype(BF16), hsum, preferred_element_type=F32)
    return (a * lax.rsqrt(ssum * (1.0 / HEAD_DIM) + EPS)) * g


def _project(x, nrm, w_in, gq, gk, hsum):
    n = _rms_rows(x, nrm)
    proj = jnp.dot(n.astype(BF16), w_in, preferred_element_type=F32)
    u = proj[:, 0:POOL_WIDTH]
    q = _head_norm(proj[:, POOL_WIDTH:POOL_WIDTH + ATTN_WIDTH], gq, hsum)
    k = _head_norm(proj[:, POOL_WIDTH + ATTN_WIDTH:POOL_WIDTH + 2 * ATTN_WIDTH], gk, hsum)
    v = proj[:, POOL_WIDTH + 2 * ATTN_WIDTH:]
    return u, q, k, v


def _split_heads(a, out_ref):
    rows = a.shape[0]
    lane = lax.broadcasted_iota(jnp.int32, (rows, LANES), 1)
    first = lane < HEAD_DIM
    for p in range(N_PAIRS):
        a2 = a[:, p * LANES:(p + 1) * LANES]
        out_ref[:, 2 * p * LANES:(2 * p + 1) * LANES] = jnp.where(first, a2, 0.0).astype(out_ref.dtype)
        out_ref[:, (2 * p + 1) * LANES:(2 * p + 2) * LANES] = jnp.where(first, 0.0, a2).astype(out_ref.dtype)


def _log2_gates(z, visible=None):
    z = lax.clamp(-LOGIT_CAP, z, LOGIT_CAP)
    if visible is not None:
        z = jnp.where(visible, z, NEG)
    drop = jnp.log(1.0 + jnp.exp2(z)) * LOG2E
    return z - drop, drop


def _stick_step(tt, gate_in, mix_in):
    def stack_heads(b):
        half = b.shape[1] // 2
        return jnp.concatenate([b[:, :half], b[:, half:]], axis=0)

    nt_dims = (((1,), (1,)), ((), ()))
    gates = mixed = None
    if mix_in is not None:
        log_betas, drops, carries, vxbs = mix_in
        sums = [jnp.dot(drop, tt, preferred_element_type=F32) for drop in drops]
    if gate_in is not None:
        q2s, kxbs, brows, mask = gate_in
        zs = [lax.dot_general(q2, stack_heads(kxb), nt_dims, preferred_element_type=F32) + brow
              for q2, kxb, brow in zip(q2s, kxbs, brows)]
    if mix_in is not None:
        nk = sums[0].shape[1] // 2
        probs = [jnp.exp2(lb - s[:, :nk] - carry).astype(BF16) for lb, s, carry in zip(log_betas, sums, carries)]
    if gate_in is not None:
        gates = [(lb, lk.astype(BF16)) for lb, lk in (_log2_gates(z, mask) for z in zs)]
    if mix_in is not None:
        outs = [jnp.dot(a, stack_heads(vxb), preferred_element_type=F32) for a, vxb in zip(probs, vxbs)]
        mixed = [(out, carry + s[:, nk:]) for out, carry, s in zip(outs, carries, sums)]
    return gates, mixed


def _proj_seq_kernel(x_ref, hist_ref, nrm_ref, win_ref, gq_ref, gk_ref, hsum_ref, wpool_ref, ps_ref, *refs,
                     tm, p0, full_count, lead):
    if lead:
        klead_ref, vlead_ref, q_ref, k_ref, v_ref, kx_ref, vx_ref, po_ref, tail_ref, ext_ref, kv_buf, kv_sem = refs
    else:
        q_ref, k_ref, v_ref, kx_ref, vx_ref, po_ref, tail_ref, ext_ref = refs
    blk = pl.program_id(1)
    if lead:
        seq = pl.program_id(0)
        step = seq * pl.num_programs(1) + blk
        slot = step & 1
        n_steps = pl.num_programs(0) * pl.num_programs(1)

        def block_copy(which, hbm_ref, s, b, j):
            return pltpu.make_async_copy(kv_buf.at[which, s], hbm_ref.at[b, pl.ds(lead + j * tm, tm), :],
                                         kv_sem.at[which, s])

        @pl.when(step >= 2)
        def _():
            block_copy(0, k_ref, slot, 0, 0).wait()
            block_copy(1, v_ref, slot, 0, 0).wait()

    u, q, k, v = _project(x_ref[...], nrm_ref[...], win_ref[...], gq_ref[...], gk_ref[...], hsum_ref[...])
    q_ref[...] = q.astype(q_ref.dtype)
    if lead:
        kv_buf[0, slot] = k
        kv_buf[1, slot] = v
    else:
        k_ref[...] = k
        v_ref[...] = v
    _split_heads(k, kx_ref)
    _split_heads(v, vx_ref)

    @pl.when(blk == 0)
    def _():
        ext_ref[0:HIST_ROWS, :] = hist_ref[...]

    ext_ref[HIST_ROWS:HIST_ROWS + tm, :] = u
    for g, w in enumerate(POOL_WINDOWS):
        e = ext_ref[:, g * POOL_GROUP:(g + 1) * POOL_GROUP]
        s = e
        sh = 1
        while sh < w:
            s = s + pltpu.roll(s, sh, axis=0)
            sh *= 2
        win = s[HIST_ROWS:, :]
        tok = e[HIST_ROWS:, :]
        if full_count:
            d = win * (1.0 / w) - tok
        else:
            pos = p0 + lax.broadcasted_iota(jnp.int32, (tm, 1), 0)
            cnt = jnp.minimum(w, pos + 1).astype(F32)
            d = win / cnt - tok
        y = jnp.dot(d.astype(BF16), wpool_ref[g], preferred_element_type=F32)
        y = y * ps_ref[:, g * POOL_GROUP:(g + 1) * POOL_GROUP]
        po_ref[:, g * POOL_GROUP:(g + 1) * POOL_GROUP] = y.astype(po_ref.dtype)
    last = ext_ref[tm:tm + HIST_ROWS, :]
    ext_ref[0:HIST_ROWS, :] = last

    @pl.when(blk == pl.num_programs(1) - 1)
    def _():
        tail_ref[...] = last

    if lead:
        block_copy(0, k_ref, slot, seq, blk).start()
        block_copy(1, v_ref, slot, seq, blk).start()

        @pl.when(blk == 0)
        def _():
            for which, (src, dst) in enumerate(((klead_ref, k_ref), (vlead_ref, v_ref))):
                cp = pltpu.make_async_copy(src, dst.at[seq, pl.ds(0, lead), :], kv_sem.at[which, 2])
                cp.start()
                cp.wait()

        @pl.when(step == n_steps - 1)
        def _():
            @pl.when(step >= 1)
            def _():
                block_copy(0, k_ref, 1 - slot, 0, 0).wait()
                block_copy(1, v_ref, 1 - slot, 0, 0).wait()

            block_copy(0, k_ref, slot, 0, 0).wait()
            block_copy(1, v_ref, slot, 0, 0).wait()


def _proj_seq(x, hist, cw, *, tm, p0, full_count, kv_lead=None):
    b, l, _ = x.shape
    nb = l // tm
    lead = 0 if kv_lead is None else p0
    const = lambda shape: pl.BlockSpec(shape, lambda i, j: (0,) * len(shape))
    row = lambda width: pl.BlockSpec((None, tm, width), lambda i, j: (i, j, 0))
    kv_out = pl.BlockSpec(memory_space=pl.ANY) if lead else row(ATTN_WIDTH)
    kern = functools.partial(_proj_seq_kernel, tm=tm, p0=p0, full_count=full_count, lead=lead)
    scratch = [pltpu.VMEM((HIST_ROWS + tm, POOL_WIDTH), F32)]
    if lead:
        scratch += [pltpu.VMEM((2, 2, tm, ATTN_WIDTH), F32), pltpu.SemaphoreType.DMA((2, 3))]
    return pl.pallas_call(
        kern,
        grid=(b, nb),
        in_specs=[row(D_MODEL), const((HIST_ROWS, POOL_WIDTH)), const((1, D_MODEL)),
                  const((D_MODEL, 2 * D_MODEL)), const((1, ATTN_WIDTH)), const((1, ATTN_WIDTH)),
                  const((ATTN_WIDTH, ATTN_WIDTH)), const((4, POOL_GROUP, POOL_GROUP)), const((1, POOL_WIDTH))]
                 + ([const((lead, ATTN_WIDTH))] * 2 if lead else []),
        out_specs=[row(ATTN_WIDTH), kv_out, kv_out, row(2 * ATTN_WIDTH), row(2 * ATTN_WIDTH),
                   row(POOL_WIDTH), pl.BlockSpec((None, HIST_ROWS, POOL_WIDTH), lambda i, j: (i, 0, 0))],
        out_shape=[jax.ShapeDtypeStruct((b, l, ATTN_WIDTH), BF16),
                   jax.ShapeDtypeStruct((b, lead + l, ATTN_WIDTH), F32),
                   jax.ShapeDtypeStruct((b, lead + l, ATTN_WIDTH), F32),
                   jax.ShapeDtypeStruct((b, l, 2 * ATTN_WIDTH), BF16),
                   jax.ShapeDtypeStruct((b, l, 2 * ATTN_WIDTH), BF16),
                   jax.ShapeDtypeStruct((b, l, POOL_WIDTH), BF16),
                   jax.ShapeDtypeStruct((b, HIST_ROWS, POOL_WIDTH), F32)],
        scratch_shapes=scratch,
        compiler_params=_cparams(("arbitrary", "arbitrary")),
        name="proj_seq",
    )(x, hist, cw["nrm"], cw["w_in"], cw["gq"], cw["gk"], cw["hsum"], cw["w_pool"], cw["pool_scale"],
      *(kv_lead or ()))


def _proj_step_kernel(x_ref, hist_ref, nrm_ref, win_ref, gq_ref, gk_ref, hsum_ref, wpool_ref, ps_ref,
                      q_ref, k_ref, v_ref, po_ref, tail_ref, *, nseq, nt):
    u, q, k, v = _project(x_ref[...], nrm_ref[...], win_ref[...], gq_ref[...], gk_ref[...], hsum_ref[...])
    q_ref[...] = q.astype(q_ref.dtype)
    k_ref[...] = k
    v_ref[...] = v
    ext = [hist_ref[r] for r in range(POOL_HIST)] + [u[t * nseq:(t + 1) * nseq, :] for t in range(nt)]
    for r in range(POOL_HIST):
        tail_ref[r] = ext[nt + r]
    for g, w in enumerate(POOL_WINDOWS):
        sl = slice(g * POOL_GROUP, (g + 1) * POOL_GROUP)
        ds = []
        for t in range(nt):
            win = ext[POOL_HIST + t][:, sl]
            for i in range(1, w):
                win = win + ext[POOL_HIST + t - i][:, sl]
            ds.append(win * (1.0 / w) - ext[POOL_HIST + t][:, sl])
        d = jnp.concatenate(ds, axis=0)
        y = jnp.dot(d.astype(BF16), wpool_ref[g], preferred_element_type=F32) * ps_ref[:, sl]
        po_ref[:, sl] = y.astype(po_ref.dtype)


def _proj_step(x_tm, hist_tm, cw, *, nseq, nt):
    rows = nseq * nt
    kern = functools.partial(_proj_step_kernel, nseq=nseq, nt=nt)
    return pl.pallas_call(
        kern,
        out_shape=[jax.ShapeDtypeStruct((rows, ATTN_WIDTH), BF16),
                   jax.ShapeDtypeStruct((rows, ATTN_WIDTH), F32),
                   jax.ShapeDtypeStruct((rows, ATTN_WIDTH), F32),
                   jax.ShapeDtypeStruct((rows, POOL_WIDTH), BF16),
                   jax.ShapeDtypeStruct((POOL_HIST, nseq, POOL_WIDTH), F32)],
        compiler_params=pltpu.CompilerParams(vmem_limit_bytes=VMEM_LIMIT),
        name="proj_step",
    )(x_tm, hist_tm, cw["nrm"], cw["w_in"], cw["gq"], cw["gk"], cw["hsum"], cw["w_pool"], cw["pool_scale"])


def _attn_prompt_kernel(q_ref, kx_ref, vx_ref, kxm_ref, vxm_ref, bias_ref, tt_ref, o_ref,
                        carry_ref, acc_ref, lb_ref, lk_ref):
    qi = pl.program_id(1)
    tt = tt_ref[...]
    nq = ATT_BLOCK
    pairs = range(N_PAIRS)
    lanes = [slice(p * LANES, (p + 1) * LANES) for p in pairs]
    wide = [slice(2 * p * LANES, 2 * (p + 1) * LANES) for p in pairs]

    def rows(j):
        return pl.ds(pl.multiple_of(j * ATT_BLOCK, ATT_BLOCK), ATT_BLOCK)

    def gate_in(k_of, mask):
        return [q_ref[:, l] for l in lanes], [k_of(w) for w in wide], [bias_ref[:, w] for w in wide], mask

    def pending():
        return [lb_ref[p] for p in pairs], [lk_ref[p] for p in pairs]

    def keep(gates):
        lbs, drops = gates
        for p in pairs:
            lb_ref[p] = lbs[p]
            lk_ref[p] = drops[p]

    def step(gin, gates_prev, v_of=None, last=False):
        min_ = accs = None
        if gates_prev is not None:
            lbs, drops = gates_prev
            min_ = (lbs, drops, [carry_ref[p] for p in pairs], [v_of(w) for w in wide])
            accs = [acc_ref[:, l] for l in lanes]
        gates, mixed = _stick_step(tt, gin, min_)
        if mixed is not None:
            for p, (out, carry) in enumerate(mixed):
                if last:
                    o_ref[:, lanes[p]] = (accs[p] + out).astype(o_ref.dtype)
                else:
                    carry_ref[p] = carry
                    acc_ref[:, lanes[p]] = accs[p] + out
        return None if gates is None else ([g[0] for g in gates], [g[1] for g in gates])

    def token_tile(j):
        return gate_in(lambda w: kx_ref[rows(j), w], None)

    carry_ref[...] = jnp.zeros_like(carry_ref)
    acc_ref[...] = jnp.zeros_like(acc_ref)
    r_i = lax.broadcasted_iota(jnp.int32, (nq, 2 * ATT_BLOCK), 0)
    c_i = lax.broadcasted_iota(jnp.int32, (nq, 2 * ATT_BLOCK), 1) & (ATT_BLOCK - 1)
    keep(step(gate_in(lambda w: kx_ref[rows(qi), w], c_i < r_i), None))

    def tiles_from(j, n):
        g = pending()
        for k in range(n):
            g = step(token_tile(j - k), g, lambda w, k=k: vx_ref[rows(j - k + 1), w])
        keep(g)

    def unrolled_trip(t, _):
        tiles_from(qi - 1 - ATT_UNROLL * t, ATT_UNROLL)
        return 0

    n_trips = qi // ATT_UNROLL
    lax.fori_loop(0, n_trips, unrolled_trip, 0)

    def single_trip(t, _):
        tiles_from(qi - 1 - ATT_UNROLL * n_trips - t, 1)
        return 0

    lax.fori_loop(0, qi - ATT_UNROLL * n_trips, single_trip, 0)

    g = step(gate_in(lambda w: kxm_ref[:, w], c_i < N_META), pending(), lambda w: vx_ref[0:ATT_BLOCK, w])
    step(None, g, lambda w: vxm_ref[:, w], last=True)


def _attn_prompt(q, kx, vx, kxm, vxm, bias_rows, tt):
    b, l, _ = q.shape
    nq = l // ATT_BLOCK
    once = pl.Buffered(1)
    return pl.pallas_call(
        _attn_prompt_kernel,
        grid=(b, nq),
        in_specs=[pl.BlockSpec((None, ATT_BLOCK, ATTN_WIDTH), lambda i, j: (i, j, 0)),
                  pl.BlockSpec((None, l, 2 * ATTN_WIDTH), lambda i, j: (i, 0, 0), pipeline_mode=once),
                  pl.BlockSpec((None, l, 2 * ATTN_WIDTH), lambda i, j: (i, 0, 0), pipeline_mode=once),
                  pl.BlockSpec((ATT_BLOCK, 2 * ATTN_WIDTH), lambda i, j: (0, 0)),
                  pl.BlockSpec((ATT_BLOCK, 2 * ATTN_WIDTH), lambda i, j: (0, 0)),
                  pl.BlockSpec((1, 2 * ATTN_WIDTH), lambda i, j: (0, 0)),
                  pl.BlockSpec((2 * ATT_BLOCK, 4 * ATT_BLOCK), lambda i, j: (0, 0))],
        out_specs=pl.BlockSpec((None, ATT_BLOCK, ATTN_WIDTH), lambda i, j: (i, j, 0)),
        out_shape=jax.ShapeDtypeStruct((b, l, ATTN_WIDTH), BF16),
        scratch_shapes=[pltpu.VMEM((N_PAIRS, ATT_BLOCK, 2 * ATT_BLOCK), F32),
                        pltpu.VMEM((ATT_BLOCK, ATTN_WIDTH), F32),
                        pltpu.VMEM((N_PAIRS, ATT_BLOCK, 2 * ATT_BLOCK), F32),
                        pltpu.VMEM((N_PAIRS, ATT_BLOCK, 2 * ATT_BLOCK), BF16)],
        compiler_params=_cparams(("arbitrary", "arbitrary")),
        name="attn_prompt",
    )(q, kx, vx, kxm, vxm, bias_rows, tt)


def _attn_sample_kernel(pt_ref, qbd_ref, kn_ref, vn_ref, bias_ref, tt_ref, *refs, n_pages, nt):
    del pt_ref
    k_refs = refs[:n_pages]
    v_refs = refs[n_pages:2 * n_pages]
    o_ref = refs[2 * n_pages]
    qbd = qbd_ref[...]
    tt = tt_ref[...]
    bias = bias_ref[...]
    nc = qbd.shape[0]
    nn = (((1,), (0,)), ((), ()))
    nt_dims = (((1,), (1,)), ((), ()))

    pad = jnp.zeros((PAGE - kn_ref.shape[0], ATTN_WIDTH), F32)
    kn = jnp.concatenate([kn_ref[...], pad], axis=0).astype(BF16)
    vn = jnp.concatenate([vn_ref[...], pad], axis=0).astype(BF16)
    t_i = lax.broadcasted_iota(jnp.int32, (nc, PAGE), 0) >> 3
    k_i = lax.broadcasted_iota(jnp.int32, (nc, PAGE), 1)
    z_new = lax.dot_general(qbd, kn, nt_dims, preferred_element_type=F32) + bias
    zs = [z_new]
    for j in range(n_pages - 1, -1, -1):
        zs.append(lax.dot_general(qbd, k_refs[j][...].astype(BF16), nn, preferred_element_type=F32) + bias)
    gates = [_log2_gates(z, k_i < t_i if i == 0 else None) for i, z in enumerate(zs)]
    sums = [jnp.dot(lk.astype(BF16), tt, preferred_element_type=F32) for _, lk in gates]
    carry = jnp.zeros((nc, PAGE), F32)
    acc = None
    for i, ((lb, _), s) in enumerate(zip(gates, sums)):
        a = jnp.exp2(lb - s[:, :PAGE] - carry).astype(BF16)
        carry = carry + s[:, PAGE:]
        if i == 0:
            acc = jnp.dot(a, vn, preferred_element_type=F32)
        else:
            vt = v_refs[n_pages - i][...].astype(BF16)
            acc = acc + lax.dot_general(a, vt, nt_dims, preferred_element_type=F32)
    h_i = lax.broadcasted_iota(jnp.int32, (nc, ATTN_WIDTH), 0) & (N_HEADS - 1)
    l_i = lax.broadcasted_iota(jnp.int32, (nc, ATTN_WIDTH), 1) >> 6
    acc = jnp.where(h_i == l_i, acc, 0.0)
    for t in range(nt):
        o_ref[t:t + 1, :] = jnp.sum(acc[t * N_HEADS:(t + 1) * N_HEADS, :], axis=0, keepdims=True)


def _attn_sample(page_table, qbd, kn, vn, bias_col, tt, cache_k, cache_v):
    nseq, n_pages = page_table.shape
    nt = qbd.shape[1] // N_HEADS
    pt = page_table.reshape(-1)

    def page_spec(j):
        return pl.BlockSpec((None, ATTN_WIDTH, PAGE), lambda i, pt_ref: (pt_ref[i * n_pages + j], 0, 0))

    grid_spec = pltpu.PrefetchScalarGridSpec(
        num_scalar_prefetch=1,
        grid=(nseq,),
        in_specs=[pl.BlockSpec((None, nt * N_HEADS, ATTN_WIDTH), lambda i, pt_ref: (i, 0, 0)),
                  pl.BlockSpec((None, 8, ATTN_WIDTH), lambda i, pt_ref: (i, 0, 0)),
                  pl.BlockSpec((None, 8, ATTN_WIDTH), lambda i, pt_ref: (i, 0, 0)),
                  pl.BlockSpec((nt * N_HEADS, 1), lambda i, pt_ref: (0, 0)),
                  pl.BlockSpec((PAGE, 2 * PAGE), lambda i, pt_ref: (0, 0))]
                 + [page_spec(j) for j in range(n_pages)] + [page_spec(j) for j in range(n_pages)],
        out_specs=pl.BlockSpec((None, nt, ATTN_WIDTH), lambda i, pt_ref: (i, 0, 0)),
    )
    kern = functools.partial(_attn_sample_kernel, n_pages=n_pages, nt=nt)
    return pl.pallas_call(
        kern,
        grid_spec=grid_spec,
        out_shape=jax.ShapeDtypeStruct((nseq, nt, ATTN_WIDTH), F32),
        compiler_params=_cparams(("arbitrary",)),
        name="attn_sample",
    )(pt, qbd, kn, vn, bias_col, tt, *([cache_k] * n_pages), *([cache_v] * n_pages))


def _merge_router_kernel(xp_ref, xs_ref, pop_ref, pos_ref, atp_ref, ats_ref, wop_ref, woa_ref, nrm_ref, wrh_ref,
                         wrl_ref, tri_ref, x1_ref, h_ref, meta_ref, cout_ref, carry_ref, *, nb_prompt):
    i = pl.program_id(0)

    @pl.when(i == 0)
    def _():
        carry_ref[...] = jnp.zeros_like(carry_ref)

    is_p = i < nb_prompt
    x = jnp.where(is_p, xp_ref[...], xs_ref[...])
    po = jnp.where(is_p, pop_ref[...], pos_ref[...])
    at = jnp.where(is_p, atp_ref[...], ats_ref[...])
    y = (jnp.dot(po, wop_ref[...], preferred_element_type=F32)
         + jnp.dot(at, woa_ref[...], preferred_element_type=F32))
    x1 = x + y
    x1_ref[...] = x1
    h = _rms_rows(x1, nrm_ref[...])
    h_ref[...] = h
    hi = h.astype(BF16)
    lo = (h - hi.astype(F32)).astype(BF16)
    both = jnp.dot(hi, jnp.concatenate([wrh_ref[...], wrl_ref[...]], axis=1), preferred_element_type=F32)
    logits = both[:, :LANES] + both[:, LANES:] + jnp.dot(lo, wrh_ref[...], preferred_element_type=F32)
    tm = logits.shape[0]
    lane = lax.broadcasted_iota(jnp.int32, (tm, LANES), 1).astype(F32)
    ninf = -jnp.inf
    is_g = lane < N_GROUPS
    gl = jnp.where(is_g, logits, ninf)
    gmax = jnp.max(gl, axis=-1, keepdims=True)
    g_sel = jnp.min(jnp.where(gl == gmax, lane, float(LANES)), axis=-1, keepdims=True)
    psum = jnp.sum(jnp.where(is_g, jnp.exp(logits - gmax), 0.0), axis=-1, keepdims=True)
    g_w = 1.0 / psum
    e_lo = N_GROUPS + g_sel * EXPERTS_PER_GROUP
    in_group = (lane >= e_lo) & (lane < e_lo + EXPERTS_PER_GROUP)
    el = jnp.where(in_group, logits, ninf)
    v1 = jnp.max(el, axis=-1, keepdims=True)
    i1 = jnp.min(jnp.where(el == v1, lane, float(LANES)), axis=-1, keepdims=True)
    el2 = jnp.where(lane == i1, ninf, el)
    v2 = jnp.max(el2, axis=-1, keepdims=True)
    i2 = jnp.min(jnp.where(el2 == v2, lane, float(LANES)), axis=-1, keepdims=True)
    e21 = jnp.exp(v2 - v1)
    den = 1.0 + e21
    gate1 = g_w * (1.0 / den)
    gate2 = g_w * (e21 / den)
    e1 = i1 - N_GROUPS
    e2 = i2 - N_GROUPS
    onehot = jnp.where((lane == e1) | (lane == e2), 1.0, 0.0)
    before = jnp.dot(tri_ref[...], onehot.astype(BF16), preferred_element_type=F32) + carry_ref[...]
    rank1 = jnp.sum(jnp.where(lane == e1, before, 0.0), axis=-1, keepdims=True)
    rank2 = jnp.sum(jnp.where(lane == e2, before, 0.0), axis=-1, keepdims=True)
    carry = carry_ref[...] + jnp.sum(onehot, axis=0, keepdims=True)
    carry_ref[...] = carry
    cout_ref[...] = carry
    meta = jnp.where(lane == 0, e1, 0.0)
    meta = jnp.where(lane == 1, e2, meta)
    meta = jnp.where(lane == 2, rank1, meta)
    meta = jnp.where(lane == 3, rank2, meta)
    meta = jnp.where(lane == 4, gate1, meta)
    meta = jnp.where(lane == 5, gate2, meta)
    meta_ref[...] = meta


def _merge_router(x_p, x_s, po_p, po_s, at_p, at_s, cw):
    tm = MERGE_ROWS
    nbp = x_p.shape[0] // tm
    nbs = x_s.shape[0] // tm
    const = lambda shape: pl.BlockSpec(shape, lambda i: (0,) * len(shape))
    prow = lambda width: pl.BlockSpec((tm, width), lambda i: (jnp.minimum(i, nbp - 1), 0))
    srow = lambda width: pl.BlockSpec((tm, width), lambda i: (jnp.maximum(i - nbp, 0), 0))
    row = lambda width: pl.BlockSpec((tm, width), lambda i: (i, 0))
    t_all = (nbp + nbs) * tm
    return pl.pallas_call(
        functools.partial(_merge_router_kernel, nb_prompt=nbp),
        grid=(nbp + nbs,),
        in_specs=[prow(D_MODEL), srow(D_MODEL), prow(POOL_WIDTH), srow(POOL_WIDTH), prow(ATTN_WIDTH),
                  srow(ATTN_WIDTH), const((POOL_WIDTH, D_MODEL)), const((ATTN_WIDTH, D_MODEL)),
                  const((1, D_MODEL)), const((D_MODEL, LANES)), const((D_MODEL, LANES)), const((tm, tm))],
        out_specs=[row(D_MODEL), row(D_MODEL), row(LANES), const((1, LANES))],
        out_shape=[jax.ShapeDtypeStruct((t_all, D_MODEL), F32),
                   jax.ShapeDtypeStruct((t_all, D_MODEL), F32),
                   jax.ShapeDtypeStruct((t_all, LANES), F32),
                   jax.ShapeDtypeStruct((1, LANES), F32)],
        scratch_shapes=[pltpu.VMEM((1, LANES), F32)],
        compiler_params=_cparams(("arbitrary",)),
        name="merge_router",
    )(x_p, x_s, po_p, po_s, at_p, at_s, cw["w_out_pool"], cw["w_out_attn"], cw["nrm_ffn"], cw["w_router_hi"],
      cw["w_router_lo"], cw["tri_merge"])


def _dispatch_kernel(fill_r0_ref, fill_nr_ref, fill_g0_ref, fill_ng_ref, dest_ref, h_hbm, hs_hbm,
                     zbuf, hbuf, sem, in_sem, out_sem):
    i = pl.program_id(0)
    n = pl.num_programs(0)
    tm = hbuf.shape[1]
    slot = lax.rem(i, 3)
    prev = lax.rem(i + 2, 3)

    def row_of(r):
        return hs_hbm.at[r >> 3, pl.ds(r & (ROW_GROUP - 1), 1), :]

    def fetch(blk, s):
        return pltpu.make_async_copy(h_hbm.at[pl.ds(blk * tm, tm), :], hbuf.at[s], in_sem.at[s])

    def drain(s):
        for _ in range(2):
            pltpu.make_async_copy(zbuf, hs_hbm.at[pl.ds(0, tm // ROW_GROUP)], out_sem.at[s]).wait()

    @pl.when(i == 0)
    def _():
        fetch(0, 0).start()

        @pl.when(n > 1)
        def _():
            fetch(1, 1).start()

        zbuf[...] = jnp.zeros_like(zbuf)

        def fill(wait):
            def body(blk, _):
                for k in range(ROW_GROUP - 1):
                    @pl.when(k < fill_nr_ref[blk])
                    def _():
                        cp = pltpu.make_async_copy(zbuf.at[0, pl.ds(0, 1), :], row_of(fill_r0_ref[blk] + k), sem)
                        cp.wait() if wait else cp.start()

                ng = fill_ng_ref[blk]

                @pl.when(ng > 0)
                def _():
                    cp = pltpu.make_async_copy(zbuf.at[pl.ds(0, ng)], hs_hbm.at[pl.ds(fill_g0_ref[blk], ng)], sem)
                    cp.wait() if wait else cp.start()
                return 0

            lax.fori_loop(0, fill_r0_ref.shape[0], body, 0)

        fill(False)
        fill(True)

    fetch(i, slot).wait()

    def rows8(g, _):
        for k in range(ROW_GROUP):
            r = g * ROW_GROUP + k
            for e in range(2):
                pltpu.make_async_copy(hbuf.at[slot, pl.ds(r, 1), :], row_of(dest_ref[0, 0, 2 * r + e]),
                                      out_sem.at[slot]).start()
        return 0

    lax.fori_loop(0, tm // ROW_GROUP, rows8, 0)

    @pl.when(i >= 1)
    def _():
        drain(prev)

    @pl.when(i + 2 < n)
    def _():
        fetch(i + 2, prev).start()

    @pl.when(i == n - 1)
    def _():
        drain(slot)


def _dispatch(fills, dest, h_all, n_sorted):
    t_all = h_all.shape[0]
    tm = EXPERT_ROWS
    grid_spec = pltpu.PrefetchScalarGridSpec(
        num_scalar_prefetch=4,
        grid=(t_all // tm,),
        in_specs=[pl.BlockSpec((1, 1, 2 * tm), lambda i, *_: (i, 0, 0), memory_space=pltpu.SMEM),
                  pl.BlockSpec(memory_space=pl.ANY)],
        out_specs=pl.BlockSpec(memory_space=pl.ANY),
        scratch_shapes=[pltpu.VMEM((tm // ROW_GROUP, ROW_GROUP, D_MODEL), F32),
                        pltpu.VMEM((3, tm, D_MODEL), F32),
                        pltpu.SemaphoreType.DMA(()), pltpu.SemaphoreType.DMA((3,)), pltpu.SemaphoreType.DMA((3,))],
    )
    return pl.pallas_call(
        _dispatch_kernel,
        grid_spec=grid_spec,
        out_shape=jax.ShapeDtypeStruct((n_sorted // ROW_GROUP, ROW_GROUP, D_MODEL), F32),
        compiler_params=_cparams(("arbitrary",)),
        name="dispatch",
    )(*fills, dest.reshape(t_all // tm, 1, 2 * tm), h_all)


def _expert_kernel(bexp_ref, nval_ref, hs_hbm, wg_ref, wu_ref, wd_ref, ys_ref, xbuf, wgb, wub, wdb, sem):
    i = pl.program_id(0)
    last = pl.num_programs(0) - 1
    slot = i & 1
    groups = xbuf.shape[1]
    g_cur = lax.shift_right_logical(nval_ref[i], 3)
    g_next = lax.shift_right_logical(nval_ref[jnp.minimum(i + 1, last)], 3)

    def fetch(blk, s, ng):
        return pltpu.make_async_copy(hs_hbm.at[pl.ds(blk * groups, ng)], xbuf.at[s, pl.ds(0, ng)], sem.at[s])

    @pl.when(i == 0)
    def _():
        xbuf[...] = jnp.zeros_like(xbuf)

        @pl.when(g_cur > 0)
        def _():
            fetch(0, 0, g_cur).start()

    @pl.when((i < last) & (g_next > 0))
    def _():
        fetch(i + 1, 1 - slot, g_next).start()

    @pl.when(g_cur > 0)
    def _():
        changed = jnp.logical_or(i == 0, bexp_ref[i] != bexp_ref[jnp.maximum(i - 1, 0)])

        @pl.when(changed)
        def _():
            wgb[...] = wg_ref[...].astype(BF16)
            wub[...] = wu_ref[...].astype(BF16)
            wdb[...] = wd_ref[...].astype(BF16)

        fetch(i, slot, g_cur).wait()
        xb = xbuf[slot].reshape(groups * ROW_GROUP, D_MODEL).astype(BF16)
        g = jnp.dot(xb, wgb[...], preferred_element_type=F32)
        u = jnp.dot(xb, wub[...], preferred_element_type=F32)
        mid = (g * jax.nn.sigmoid(g)) * u
        ys_ref[...] = jnp.dot(mid.astype(BF16), wdb[...], preferred_element_type=F32)

    @pl.when(g_cur == 0)
    def _():
        ys_ref[...] = jnp.zeros_like(ys_ref)


def _experts(blk_exp, nvalid, hs, w_gate, w_up, w_down):
    nblk = blk_exp.shape[0]
    rows = EXPERT_ROWS
    grid_spec = pltpu.PrefetchScalarGridSpec(
        num_scalar_prefetch=2,
        grid=(nblk,),
        in_specs=[pl.BlockSpec(memory_space=pl.ANY),
                  pl.BlockSpec((None, D_MODEL, D_EXPERT), lambda i, be, nv: (be[i], 0, 0)),
                  pl.BlockSpec((None, D_MODEL, D_EXPERT), lambda i, be, nv: (be[i], 0, 0)),
                  pl.BlockSpec((None, D_EXPERT, D_MODEL), lambda i, be, nv: (be[i], 0, 0))],
        out_specs=pl.BlockSpec((rows, D_MODEL), lambda i, be, nv: (i, 0)),
        scratch_shapes=[pltpu.VMEM((2, rows // ROW_GROUP, ROW_GROUP, D_MODEL), F32),
                        pltpu.VMEM((D_MODEL, D_EXPERT), BF16), pltpu.VMEM((D_MODEL, D_EXPERT), BF16),
                        pltpu.VMEM((D_EXPERT, D_MODEL), BF16),
                        pltpu.SemaphoreType.DMA((2,))],
    )
    return pl.pallas_call(
        _expert_kernel,
        grid_spec=grid_spec,
        out_shape=jax.ShapeDtypeStruct((nblk * rows, D_MODEL), F32),
        compiler_params=_cparams(("arbitrary",)),
        name="experts",
    )(blk_exp, nvalid, hs, w_gate, w_up, w_down)


def _combine_kernel(dcur_ref, dnxt_ref, x1_ref, meta_ref, ys_hbm, op_ref, os_ref, gbuf, sem, *, nb_prompt):
    i = pl.program_id(0)
    last = pl.num_programs(0) - 1
    slot = i & 1
    tm = x1_ref.shape[0]

    def start_gather(d_ref, s):
        def rows8(g, _):
            for k in range(ROW_GROUP):
                r = g * ROW_GROUP + k
                for e in range(2):
                    pltpu.make_async_copy(ys_hbm.at[pl.ds(d_ref[0, 0, 2 * r + e], 1), :],
                                          gbuf.at[s, e, pl.ds(r, 1), :], sem.at[s]).start()
            return 0

        lax.fori_loop(0, tm // ROW_GROUP, rows8, 0)

    @pl.when(i == 0)
    def _():
        start_gather(dcur_ref, 0)

    @pl.when(i < last)
    def _():
        start_gather(dnxt_ref, 1 - slot)

    for e in range(2):
        pltpu.make_async_copy(ys_hbm.at[pl.ds(0, tm), :], gbuf.at[slot, e], sem.at[slot]).wait()
    meta = meta_ref[...]
    y = x1_ref[...] + meta[:, 4:5] * gbuf[slot, 0] + meta[:, 5:6] * gbuf[slot, 1]

    @pl.when(i < nb_prompt)
    def _():
        op_ref[...] = y

    @pl.when(i >= nb_prompt)
    def _():
        os_ref[...] = y


def _combine(x1, ys, meta, dest, t_prompt):
    t_all = x1.shape[0]
    tm = COMBINE_ROWS
    nb = t_all // tm
    nbp = t_prompt // tm
    dest = dest.reshape(nb, 1, 2 * tm)
    grid_spec = pl.GridSpec(
        grid=(nb,),
        in_specs=[pl.BlockSpec((1, 1, 2 * tm), lambda i: (i, 0, 0), memory_space=pltpu.SMEM),
                  pl.BlockSpec((1, 1, 2 * tm), lambda i: (jnp.minimum(i + 1, nb - 1), 0, 0),
                               memory_space=pltpu.SMEM),
                  pl.BlockSpec((tm, D_MODEL), lambda i: (i, 0)),
                  pl.BlockSpec((tm, LANES), lambda i: (i, 0)),
                  pl.BlockSpec(memory_space=pl.ANY)],
        out_specs=[pl.BlockSpec((tm, D_MODEL), lambda i: (jnp.minimum(i, nbp - 1), 0)),
                   pl.BlockSpec((tm, D_MODEL), lambda i: (jnp.maximum(i - nbp, 0), 0))],
        scratch_shapes=[pltpu.VMEM((2, 2, tm, D_MODEL), F32), pltpu.SemaphoreType.DMA((2,))],
    )
    return pl.pallas_call(
        functools.partial(_combine_kernel, nb_prompt=nbp),
        grid_spec=grid_spec,
        out_shape=[jax.ShapeDtypeStruct((t_prompt, D_MODEL), F32),
                   jax.ShapeDtypeStruct((t_all - t_prompt, D_MODEL), F32)],
        compiler_params=_cparams(("arbitrary",)),
        name="combine",
    )(dest, dest, x1, meta, ys)


def _prep_weights(norm_mix, w_in, g_q, g_k, w_pool, pool_scale, w_out, norm_ffn, w_group, w_expert):
    lane = jnp.arange(ATTN_WIDTH)
    hsum = (lane[:, None] // HEAD_DIM == lane[None, :] // HEAD_DIM).astype(BF16)
    w_router = jnp.zeros((D_MODEL, LANES), F32)
    w_router = w_router.at[:, 0:N_GROUPS].set(w_group).at[:, N_GROUPS:N_GROUPS + N_EXPERTS].set(w_expert)
    w_router_hi = w_router.astype(BF16)
    w_router_lo = (w_router - w_router_hi.astype(F32)).astype(BF16)
    r = jnp.arange(MERGE_ROWS)
    return {
        "nrm": norm_mix.reshape(1, D_MODEL),
        "w_in": w_in.astype(BF16),
        "gq": (jnp.tile(g_q, N_HEADS) * (HEAD_DIM ** -0.5 * LOG2E)).reshape(1, ATTN_WIDTH),
        "gk": jnp.tile(g_k, N_HEADS).reshape(1, ATTN_WIDTH),
        "hsum": hsum,
        "w_pool": w_pool.astype(BF16),
        "pool_scale": pool_scale.reshape(1, POOL_WIDTH),
        "w_out_pool": w_out[:POOL_WIDTH].astype(BF16),
        "w_out_attn": w_out[POOL_WIDTH:].astype(BF16),
        "nrm_ffn": norm_ffn.reshape(1, D_MODEL),
        "w_router_hi": w_router_hi,
        "w_router_lo": w_router_lo,
        "tri_merge": (r[None, :] < r[:, None]).astype(BF16),
    }


def _later_sums(n_keys, n_heads):
    idx = jnp.arange(n_heads * n_keys)
    same = idx[:, None] // n_keys == idx[None, :] // n_keys
    later = same & (idx[:, None] > idx[None, :])
    return jnp.concatenate([later, same], axis=1).astype(BF16)


def kernel(x_prompt, x_sample, cache_k, cache_v, state_pool, page_table, meta_tokens, norm_mix, w_in, g_q, g_k,
           sb_bias, w_pool, pool_scale, w_out, norm_ffn, w_group, w_expert, w_gate, w_up, w_down):
    nb, seq, _ = x_prompt.shape
    nseq, nt, _ = x_sample.shape
    n_phys = cache_k.shape[1]
    t_prompt = nb * seq
    t_sample = nseq * nt
    t_all = t_prompt + t_sample

    cw = _prep_weights(norm_mix[0], w_in[0], g_q[0], g_k[0], w_pool[0], pool_scale[0], w_out[0], norm_ffn[0],
                       w_group[0], w_expert[0])
    bias = sb_bias[0].astype(F32) * LOG2E

    zero_hist = jnp.zeros((HIST_ROWS, POOL_WIDTH), F32)
    _, k_m, v_m, kx_m, vx_m, _, tail_m = _proj_seq(meta_tokens[None], zero_hist, cw, tm=N_META, p0=0,
                                                   full_count=False)
    pad_rows = ((0, ATT_BLOCK - N_META), (0, 0))
    kxm = jnp.pad(kx_m[0], pad_rows)
    vxm = jnp.pad(vx_m[0], pad_rows)

    q_p, k_p, v_p, kx_p, vx_p, po_p, tail_p = _proj_seq(x_prompt, tail_m[0], cw, tm=PROJ_ROWS, p0=N_META,
                                                        full_count=True, kv_lead=(k_m[0], v_m[0]))
    bias_rows = jnp.repeat(bias, ATT_BLOCK)[None]
    attn_p = _attn_prompt(q_p, kx_p, vx_p, kxm, vxm, bias_rows, _later_sums(ATT_BLOCK, 2))

    x_s_tm = jnp.transpose(x_sample, (1, 0, 2)).reshape(t_sample, D_MODEL)
    hist_tm = jnp.transpose(state_pool[0], (1, 0, 2))
    q_s, k_s, v_s, po_s, tail_s = _proj_step(x_s_tm, hist_tm, cw, nseq=nseq, nt=nt)
    to_seq = lambda a: jnp.transpose(a.reshape(nt, nseq, a.shape[-1]), (1, 0, 2))
    q_sb, k_sb, v_sb, po_sb = to_seq(q_s), to_seq(k_s), to_seq(v_s), to_seq(po_s)
    head_of_lane = jnp.arange(ATTN_WIDTH) // HEAD_DIM
    head_mask = head_of_lane[None, :] == jnp.arange(N_HEADS)[:, None]
    qbd = jnp.where(head_mask[None, None], q_sb[:, :, None, :], 0).reshape(nseq, nt * N_HEADS, ATTN_WIDTH)
    pad_new = ((0, 0), (0, 8 - nt), (0, 0))
    kn = jnp.pad(k_sb, pad_new)
    vn = jnp.pad(v_sb, pad_new)
    bias_col = jnp.tile(bias, nt).reshape(nt * N_HEADS, 1)
    page_t = lambda c: jnp.transpose(c[0], (0, 2, 3, 1)).reshape(n_phys, ATTN_WIDTH, PAGE)
    attn_s = _attn_sample(page_table, qbd, kn, vn, bias_col, _later_sums(PAGE, 1), page_t(cache_k), page_t(cache_v))

    x1, h_all, meta, counts_f = _merge_router(
        x_prompt.reshape(t_prompt, D_MODEL), x_sample.reshape(t_sample, D_MODEL),
        po_p.reshape(t_prompt, POOL_WIDTH), po_sb.reshape(t_sample, POOL_WIDTH),
        attn_p.reshape(t_prompt, ATTN_WIDTH), attn_s.reshape(t_sample, ATTN_WIDTH).astype(BF16), cw)

    pair_exp = meta[:, 0:2].astype(jnp.int32)
    pair_rank = meta[:, 2:4].astype(jnp.int32)
    counts = counts_f[0, :N_EXPERTS].astype(jnp.int32)
    rows = EXPERT_ROWS
    padded = (counts + rows - 1) // rows * rows
    pad_end = jnp.cumsum(padded)
    pad_start = pad_end - padded
    nblk = -(-(2 * t_all) // rows) + N_EXPERTS
    of_expert = pair_exp[..., None] == jnp.arange(N_EXPERTS, dtype=jnp.int32)
    dest = jnp.sum(jnp.where(of_expert, pad_start, 0), axis=-1) + pair_rank
    blk_row0 = jnp.arange(nblk, dtype=jnp.int32) * rows
    blk_exp = jnp.minimum(jnp.sum(blk_row0[:, None] >= pad_end[None, :], axis=1), N_EXPERTS - 1).astype(jnp.int32)
    blk_of = blk_exp[:, None] == jnp.arange(N_EXPERTS, dtype=jnp.int32)
    blk_count = jnp.sum(jnp.where(blk_of, counts, 0), axis=1)
    blk_start = jnp.sum(jnp.where(blk_of, pad_start, 0), axis=1)
    n_rows = jnp.clip(blk_count - (blk_row0 - blk_start), 0, rows)
    n_rows = jnp.where(blk_row0 < pad_end[-1], n_rows, 0).astype(jnp.int32)
    n_tiled = (n_rows + ROW_GROUP - 1) // ROW_GROUP * ROW_GROUP
    fills = (blk_row0 + n_rows, n_tiled - n_rows, (blk_row0 + n_tiled) // ROW_GROUP, (rows - n_tiled) // ROW_GROUP)

    hs = _dispatch(fills, dest, h_all, nblk * rows)
    ys = _experts(blk_exp, n_tiled, hs, w_gate[0], w_up[0], w_down[0])
    y_prompt, y_sample = _combine(x1, ys, meta, dest, t_prompt)
    y_prompt = y_prompt.reshape(nb, seq, D_MODEL)
    y_sample = y_sample.reshape(nseq, nt, D_MODEL)

    heads = lambda a: a.reshape(a.shape[:-1] + (N_HEADS, HEAD_DIM))
    pool_sample = jnp.transpose(tail_s, (1, 0, 2))
    return (y_prompt, y_sample, heads(k_p)[None], heads(v_p)[None], tail_p[:, 1:][None],
            heads(k_sb)[None], heads(v_sb)[None], pool_sample[None])
```
